```python
import math
import jax, jax.numpy as jnp
from jax import lax
import numpy as np

D_MODEL = 2048
BATCH = 2
SEQ = 4096
DEPTH = 2
DEC_BATCH = 32
DEC_SEQ = 64
PAST_LEN = 2048

CHUNK = 64
QBLOCK = 128
ML_HEADS = 4
ML_QK_DIM = 128
ML_V_DIM = 256
DA_HEADS = 4
DA_HEAD_DIM = 128
DA_V_DIM = 2 * DA_HEAD_DIM
CONV_W = 4
D_FF = 5504
EPS = 1e-6

ML_WIDTH = ML_HEADS * ML_V_DIM
DA_WIDTH = DA_HEADS * DA_V_DIM
MIX_WIDTH = ML_WIDTH + DA_WIDTH
ML_CONV_CH = 2 * ML_HEADS * ML_QK_DIM
DA_QK_WIDTH = DA_HEADS * 2 * DA_HEAD_DIM
IN_SIZES = (ML_CONV_CH, ML_WIDTH, ML_WIDTH, ML_HEADS, ML_HEADS, DA_QK_WIDTH, DA_QK_WIDTH, DA_WIDTH)
SPLIT_IDX = tuple(int(s) for s in np.cumsum(IN_SIZES)[:-1])
N_IN = int(sum(IN_SIZES))

kernel_name = "hybrid_mlstm_diffattn_streaming_step"


def rmsnorm(x, g):
    xf = x.astype(jnp.float32)
    y = xf * lax.rsqrt(jnp.mean(xf * xf, axis=-1, keepdims=True) + EPS)
    return (y * g.astype(jnp.float32)).astype(x.dtype)


def swiglu(x, wg, wu, wd):
    return (jax.nn.silu(x @ wg) * (x @ wu)) @ wd


def causal_conv(x, buf, w, b):
    L = x.shape[1]
    xp = jnp.concatenate([buf.astype(x.dtype), x], axis=1)
    y = b + xp[:, 0:L] * w[0]
    for j in range(1, CONV_W):
        y = y + xp[:, j:j + L] * w[j]
    return y, xp[:, -(CONV_W - 1):]


def mlstm_block(carry, xs):
    C, n, m = carry
    q, k, v, ig, lf = xs
    L = q.shape[2]
    b = jnp.cumsum(lf, axis=-1)
    causal = jnp.tril(jnp.ones((L, L), dtype=bool))
    d = jnp.where(causal, b[..., :, None] - b[..., None, :] + ig[..., None, :], -jnp.inf)
    inter = b + m[..., None]
    m_t = jnp.maximum(inter, jnp.max(d, axis=-1))
    w_intra = jnp.exp(d - m_t[..., None])
    w_inter = jnp.exp(inter - m_t)
    s = jnp.einsum("bhtd,bhsd->bhts", q, k) * w_intra
    num = jnp.einsum("bhts,bhsv->bhtv", s, v) + w_inter[..., None] * jnp.einsum("bhvd,bhtd->bhtv", C, q)
    den = jnp.sum(s, axis=-1) + w_inter * jnp.einsum("bhd,bhtd->bht", n, q)
    h = num / jnp.maximum(jnp.abs(den), jnp.exp(-m_t))[..., None]
    m_new = m_t[..., -1]
    w_state = jnp.exp(b[..., -1:] - b + ig - m_new[..., None])
    decay = jnp.exp(b[..., -1] + m - m_new)
    C_new = decay[..., None, None] * C + jnp.einsum("bhs,bhsv,bhsd->bhvd", w_state, v, k)
    n_new = decay[..., None] * n + jnp.einsum("bhs,bhsd->bhd", w_state, k)
    return (C_new, n_new, m_new), h


def mlstm_scan(q, k, v, ig, lf, C, n, m, block):
    B, H, S = q.shape[:3]
    nb = S // block

    def to_blocks(a):
        a = a.reshape(a.shape[:2] + (nb, block) + a.shape[3:])
        return jnp.moveaxis(a, 2, 0)

    (C, n, m), hs = lax.scan(mlstm_block, (C, n, m), tuple(to_blocks(a) for a in (q, k, v, ig, lf)))
    hs = jnp.moveaxis(hs, 0, 2).reshape(B, H, S, ML_V_DIM)
    return hs, (C, n, m)


def diff_attn_block(q, k, v, q_pos, k_pos, lam):
    scale = DA_HEAD_DIM ** -0.5
    q1, q2 = jnp.split(q, 2, axis=-1)
    k1, k2 = jnp.split(k, 2, axis=-1)
    mask = (q_pos // CHUNK)[:, None] >= (k_pos // CHUNK)[None, :]

    def probs(qa, ka):
        sc = jnp.einsum("bqhd,bkhd->bhqk", qa, ka) * scale
        return jax.nn.softmax(jnp.where(mask, sc, -jnp.inf), axis=-1)

    a = probs(q1, k1) - lam * probs(q2, k2)
    return jnp.einsum("bhqk,bkhv->bqhv", a, v)


def diff_attn_prompt(q, k, v, lam):
    B, S = q.shape[:2]
    nb = S // QBLOCK
    qb = jnp.moveaxis(q.reshape(B, nb, QBLOCK, DA_HEADS, 2 * DA_HEAD_DIM), 1, 0)
    pos = jnp.arange(S)
    qpos = pos.reshape(nb, QBLOCK)
    out = lax.map(lambda a: diff_attn_block(a[0], k, v, a[1], pos, lam), (qb, qpos))
    return jnp.moveaxis(out, 0, 1).reshape(B, S, DA_HEADS, DA_V_DIM)


def mixer(hn, w_in, b_i, b_f, conv_w, conv_b, ml_g, lq1, lk1, lq2, lk2, da_g, w_out,
          lam_init, conv_buf, C, n, m, k_cache, v_cache):
    B, L, _ = hn.shape
    dt = hn.dtype
    f32 = jnp.float32
    qk_raw, ml_v, ml_o, ml_i, ml_f, da_q, da_k, da_v = jnp.split(hn @ w_in, SPLIT_IDX, axis=-1)
    qk, conv_new = causal_conv(qk_raw, conv_buf, conv_w, conv_b)
    ml_q, ml_k = jnp.split(jax.nn.silu(qk), 2, axis=-1)

    def heads(a, d):
        return a.reshape(B, L, -1, d).transpose(0, 2, 1, 3).astype(f32)

    q = heads(ml_q, ML_QK_DIM)
    k = heads(ml_k, ML_QK_DIM) * (ML_QK_DIM ** -0.5)
    v = heads(ml_v, ML_V_DIM)
    ig = (ml_i + b_i).astype(f32).transpose(0, 2, 1)
    lf = jax.nn.log_sigmoid((ml_f + b_f).astype(f32)).transpose(0, 2, 1)
    block = CHUNK if k_cache is None else L
    h, (C, n, m) = mlstm_scan(q, k, v, ig, lf, C.astype(f32), n.astype(f32), m.astype(f32), block)
    h = h.transpose(0, 2, 1, 3).reshape(B, L, ML_WIDTH)
    ml_out = rmsnorm(jax.nn.sigmoid(ml_o.astype(f32)) * h, ml_g)
    dq = da_q.reshape(B, L, DA_HEADS, 2 * DA_HEAD_DIM)
    dk = da_k.reshape(B, L, DA_HEADS, 2 * DA_HEAD_DIM)
    dv = da_v.reshape(B, L, DA_HEADS, DA_V_DIM)
    lam = (jnp.exp(jnp.sum(lq1.astype(f32) * lk1.astype(f32)))
           - jnp.exp(jnp.sum(lq2.astype(f32) * lk2.astype(f32))) + lam_init)
    if k_cache is None:
        att = diff_attn_prompt(dq.astype(f32), dk.astype(f32), dv.astype(f32), lam)
    else:
        P = k_cache.shape[1]
        k_all = jnp.concatenate([k_cache.astype(f32), dk.astype(f32)], axis=1)
        v_all = jnp.concatenate([v_cache.astype(f32), dv.astype(f32)], axis=1)
        att = diff_attn_block(dq.astype(f32), k_all, v_all, P + jnp.arange(L), jnp.arange(P + L), lam)
    da_out = (rmsnorm(att, da_g) * (1.0 - lam_init)).reshape(B, L, DA_WIDTH)
    y = jnp.concatenate([ml_out, da_out], axis=-1).astype(dt) @ w_out
    return y, (dk, dv, C.astype(dt), n.astype(dt), m.astype(dt), conv_new)


def layer(x, lp, lam_init, conv_buf, C, n, m, k_cache, v_cache):
    (f1_pre, f1_wg, f1_wu, f1_wd, f1_post, mx_pre, w_in, b_i, b_f, conv_w, conv_b, ml_g,
     lq1, lk1, lq2, lk2, da_g, w_out, mx_post, f2_pre, f2_wg, f2_wu, f2_wd, f2_post) = lp
    x = x + 0.5 * rmsnorm(swiglu(rmsnorm(x, f1_pre), f1_wg, f1_wu, f1_wd), f1_post)
    y, st = mixer(rmsnorm(x, mx_pre), w_in, b_i, b_f, conv_w, conv_b, ml_g, lq1, lk1, lq2, lk2,
                  da_g, w_out, lam_init, conv_buf, C, n, m, k_cache, v_cache)
    x = x + rmsnorm(y, mx_post)
    x = x + 0.5 * rmsnorm(swiglu(rmsnorm(x, f2_pre), f2_wg, f2_wu, f2_wd), f2_post)
    return x, st


def setup_inputs(seed: int = 0) -> dict:
    key = jax.random.key(seed)
    ks = iter(jax.random.split(key, 40))

    def nrm(shape, scale):
        return jax.random.normal(next(ks), shape, jnp.float32) * scale

    def gain(shape):
        return 1.0 + nrm(shape, 0.01)

    dsc = D_MODEL ** -0.5
    fsc = D_FF ** -0.5
    return {
        "x_prompt": nrm((BATCH, SEQ, D_MODEL), 1.0),
        "x_sample": nrm((DEC_BATCH, DEC_SEQ, D_MODEL), 1.0),
        "cache_k": nrm((DEPTH, DEC_BATCH, PAST_LEN, DA_HEADS, 2 * DA_HEAD_DIM), 1.0),
        "cache_v": nrm((DEPTH, DEC_BATCH, PAST_LEN, DA_HEADS, DA_V_DIM), 1.0),
        "state_C": nrm((DEPTH, DEC_BATCH, ML_HEADS, ML_V_DIM, ML_QK_DIM), 0.1),
        "state_n": nrm((DEPTH, DEC_BATCH, ML_HEADS, ML_QK_DIM), 0.1),
        "state_m": nrm((DEPTH, DEC_BATCH, ML_HEADS), 1.0),
        "state_conv": nrm((DEPTH, DEC_BATCH, CONV_W - 1, ML_CONV_CH), 1.0),
        "ffn1_pre_g": gain((DEPTH, D_MODEL)),
        "ffn1_wg": nrm((DEPTH, D_MODEL, D_FF), dsc),
        "ffn1_wu": nrm((DEPTH, D_MODEL, D_FF), dsc),
        "ffn1_wd": nrm((DEPTH, D_FF, D_MODEL), fsc),
        "ffn1_post_g": gain((DEPTH, D_MODEL)),
        "mix_pre_g": gain((DEPTH, D_MODEL)),
        "w_in": nrm((DEPTH, D_MODEL, N_IN), dsc),
        "b_i": nrm((DEPTH, ML_HEADS), 0.1),
        "b_f": jnp.linspace(3.0, 6.0, ML_HEADS)[None, :] + nrm((DEPTH, ML_HEADS), 0.1),
        "conv_w": nrm((DEPTH, CONV_W, ML_CONV_CH), CONV_W ** -0.5),
        "conv_b": nrm((DEPTH, ML_CONV_CH), 0.01),
        "ml_norm_g": gain((DEPTH, ML_WIDTH)),
        "lam_q1": nrm((DEPTH, DA_HEAD_DIM), 0.1),
        "lam_k1": nrm((DEPTH, DA_HEAD_DIM), 0.1),
        "lam_q2": nrm((DEPTH, DA_HEAD_DIM), 0.1),
        "lam_k2": nrm((DEPTH, DA_HEAD_DIM), 0.1),
        "da_norm_g": gain((DEPTH, DA_V_DIM)),
        "w_out": nrm((DEPTH, MIX_WIDTH, D_MODEL), MIX_WIDTH ** -0.5),
        "mix_post_g": gain((DEPTH, D_MODEL)),
        "ffn2_pre_g": gain((DEPTH, D_MODEL)),
        "ffn2_wg": nrm((DEPTH, D_MODEL, D_FF), dsc),
        "ffn2_wu": nrm((DEPTH, D_MODEL, D_FF), dsc),
        "ffn2_wd": nrm((DEPTH, D_FF, D_MODEL), fsc),
        "ffn2_post_g": gain((DEPTH, D_MODEL)),
    }


def reference(x_prompt, x_sample, cache_k, cache_v, state_C, state_n, state_m, state_conv,
              ffn1_pre_g, ffn1_wg, ffn1_wu, ffn1_wd, ffn1_post_g,
              mix_pre_g, w_in, b_i, b_f, conv_w, conv_b, ml_norm_g,
              lam_q1, lam_k1, lam_q2, lam_k2, da_norm_g, w_out, mix_post_g,
              ffn2_pre_g, ffn2_wg, ffn2_wu, ffn2_wd, ffn2_post_g):
    f32 = jnp.float32
    xp, xs = x_prompt, x_sample
    Bp = xp.shape[0]
    zero_conv = jnp.zeros((Bp, CONV_W - 1, ML_CONV_CH), xp.dtype)
    zero_C = jnp.zeros((Bp, ML_HEADS, ML_V_DIM, ML_QK_DIM), f32)
    zero_n = jnp.zeros((Bp, ML_HEADS, ML_QK_DIM), f32)
    zero_m = jnp.zeros((Bp, ML_HEADS), f32)
    weights = (ffn1_pre_g, ffn1_wg, ffn1_wu, ffn1_wd, ffn1_post_g,
               mix_pre_g, w_in, b_i, b_f, conv_w, conv_b, ml_norm_g,
               lam_q1, lam_k1, lam_q2, lam_k2, da_norm_g, w_out, mix_post_g,
               ffn2_pre_g, ffn2_wg, ffn2_wu, ffn2_wd, ffn2_post_g)
    p_states, s_states = [], []
    for li in range(DEPTH):
        lam_init = 0.8 - 0.6 * math.exp(-0.3 * li)
        lp = tuple(w[li] for w in weights)
        xp, st_p = layer(xp, lp, lam_init, zero_conv, zero_C, zero_n, zero_m, None, None)
        xs, st_s = layer(xs, lp, lam_init, state_conv[li], state_C[li], state_n[li], state_m[li],
                         cache_k[li], cache_v[li])
        p_states.append(st_p)
        s_states.append(st_s)
    p_k, p_v, p_C, p_n, p_m, p_conv = (jnp.stack(e) for e in zip(*p_states))
    s_k, s_v, s_C, s_n, s_m, s_conv = (jnp.stack(e) for e in zip(*s_states))
    return (xp, xs, p_k, p_v, p_C, p_n, p_m, p_conv, s_k, s_v, s_C, s_n, s_m, s_conv)
```

```python
import functools
import math

import jax
import jax.numpy as jnp
from jax import lax
from jax.experimental import pallas as pl
from jax.experimental.pallas import tpu as pltpu

F32 = jnp.float32
BF16 = jnp.bfloat16

CHUNK = 64
ML_HEADS = 4
ML_QK_DIM = 128
ML_V_DIM = 256
DA_HEADS = 4
DA_HEAD_DIM = 128
DA_V_DIM = 2 * DA_HEAD_DIM
CONV_W = 4
EPS = 1e-6
ML_WIDTH = ML_HEADS * ML_V_DIM
DA_WIDTH = DA_HEADS * DA_V_DIM
ML_CONV_CH = 2 * ML_HEADS * ML_QK_DIM
DA_QK_WIDTH = DA_HEADS * 2 * DA_HEAD_DIM
PIECE = 1024
N_PIECES = 6
N_GATES = 2 * ML_HEADS
GATE_OFF = 3 * PIECE
assert ML_CONV_CH == ML_WIDTH == DA_QK_WIDTH == DA_WIDTH == PIECE

LANES = 128
CONV_PAD = 8
VMEM_LIMIT = 56 * 1024 * 1024

NEG_INF = float("-inf")
NT = (((1,), (1,)), ((), ()))
TN = (((0,), (0,)), ((), ()))


def _params(*sem):
    return pltpu.CompilerParams(dimension_semantics=sem, vmem_limit_bytes=VMEM_LIMIT)


def _rms(xf):
    return xf * lax.rsqrt(jnp.mean(xf * xf, axis=-1, keepdims=True) + EPS)


def _silu(x):
    return x * jax.nn.sigmoid(x)


def _log_sigmoid(x):
    return jnp.minimum(x, 0.0) - jnp.log1p(jnp.exp(-jnp.abs(x)))


def _tile(n, pref):
    return pref if n % pref == 0 else n


def _ffn_kernel(x_ref, pre_ref, wg_ref, wu_ref, wd_ref, post_ref, o_ref, xn_ref, acc_ref):
    j = pl.program_id(1)

    @pl.when(j == 0)
    def _():
        xn_ref[...] = (_rms(x_ref[...]) * pre_ref[...]).astype(BF16)
        acc_ref[...] = jnp.zeros_like(acc_ref)

    xn = xn_ref[...]
    g = jnp.dot(xn, wg_ref[...], preferred_element_type=F32)
    u = jnp.dot(xn, wu_ref[...], preferred_element_type=F32)
    h = (_silu(g) * u).astype(BF16)
    acc_ref[...] += jnp.dot(h, wd_ref[...], preferred_element_type=F32)

    @pl.when(j == pl.num_programs(1) - 1)
    def _():
        o_ref[...] = x_ref[...] + 0.5 * (_rms(acc_ref[...]) * post_ref[...])


def _ffn(x, pre_g, wg, wu, wd, post_g):
    T, D = x.shape
    Fp = wg.shape[1]
    tm, tf = _tile(T, 512), _tile(Fp, 512)
    return pl.pallas_call(
        _ffn_kernel,
        out_shape=jax.ShapeDtypeStruct((T, D), F32),
        grid=(T // tm, Fp // tf),
        in_specs=[
            pl.BlockSpec((tm, D), lambda i, j: (i, 0)),
            pl.BlockSpec((1, D), lambda i, j: (0, 0)),
            pl.BlockSpec((D, tf), lambda i, j: (0, j)),
            pl.BlockSpec((D, tf), lambda i, j: (0, j)),
            pl.BlockSpec((tf, D), lambda i, j: (j, 0)),
            pl.BlockSpec((1, D), lambda i, j: (0, 0)),
        ],
        out_specs=pl.BlockSpec((tm, D), lambda i, j: (i, 0)),
        scratch_shapes=[pltpu.VMEM((tm, D), BF16), pltpu.VMEM((tm, D), F32)],
        compiler_params=_params("parallel", "arbitrary"),
        name="ffn",
    )(x, pre_g, wg, wu, wd, post_g)


def _inproj_kernel(x_ref, pre_ref, w_ref, wgate_ref, o_ref, gate_ref, xn_ref):
    @pl.when(pl.program_id(1) == 0)
    def _():
        xn = (_rms(x_ref[...]) * pre_ref[...]).astype(BF16)
        xn_ref[...] = xn
        gate_ref[...] = jnp.dot(xn, wgate_ref[...], preferred_element_type=F32)

    o_ref[...] = jnp.dot(xn_ref[...], w_ref[...], preferred_element_type=F32)


def _inproj(x, pre_g, w_main, w_gate):
    T, D = x.shape
    tm = _tile(T, 512)
    return pl.pallas_call(
        _inproj_kernel,
        out_shape=(jax.ShapeDtypeStruct((N_PIECES, T, PIECE), F32),
                   jax.ShapeDtypeStruct((T, LANES), F32)),
        grid=(T // tm, N_PIECES),
        in_specs=[
            pl.BlockSpec((tm, D), lambda i, j: (i, 0)),
            pl.BlockSpec((1, D), lambda i, j: (0, 0)),
            pl.BlockSpec((D, PIECE), lambda i, j: (0, j)),
            pl.BlockSpec((D, LANES), lambda i, j: (0, 0)),
        ],
        out_specs=(pl.BlockSpec((None, tm, PIECE), lambda i, j: (j, i, 0)),
                   pl.BlockSpec((tm, LANES), lambda i, j: (i, 0))),
        scratch_shapes=[pltpu.VMEM((tm, D), BF16)],
        compiler_params=_params("parallel", "arbitrary"),
        name="inproj",
    )(x, pre_g, w_main, w_gate)


def _mlstm_kernel(qk_ref, v_ref, og_ref, gc_ref, gr_ref, bc_ref, br_ref, cw_ref, cb_ref, mlg_ref,
                  c0_ref, n0_ref, m0_ref, conv0_ref,
                  out_ref, cn_ref, nn_ref, mn_ref, convn_ref,
                  xp_ref, c_s, n_s, m_s, *, L):
    c = pl.program_id(1)

    @pl.when(c == 0)
    def _():
        c_s[...] = c0_ref[...]
        n_s[...] = n0_ref[...]
        m_s[...] = m0_ref[...]
        xp_ref[0:CONV_PAD, :] = jnp.zeros((CONV_PAD, ML_CONV_CH), F32)
        xp_ref[CONV_PAD - (CONV_W - 1):CONV_PAD, :] = conv0_ref[...]

    qk = qk_ref[...]
    xp_ref[CONV_PAD:CONV_PAD + L, :] = qk
    cw = cw_ref[...]
    conv = cb_ref[...] + qk * cw[CONV_W - 1:CONV_W, :]
    for t in range(CONV_W - 1):
        off = CONV_PAD - (CONV_W - 1) + t
        conv = conv + xp_ref[off:off + L, :] * cw[t:t + 1, :]
    tail = xp_ref[CONV_PAD + L - (CONV_W - 1):CONV_PAD + L, :]
    xp_ref[CONV_PAD - (CONV_W - 1):CONV_PAD, :] = tail
    act = _silu(conv)

    gcb = gc_ref[...] + bc_ref[...]
    grb = gr_ref[...] + br_ref[...]
    row = lax.broadcasted_iota(jnp.int32, (L, L), 0)
    col = lax.broadcasted_iota(jnp.int32, (L, L), 1)
    causal = col <= row
    b_c = jnp.dot(causal.astype(F32), _log_sigmoid(gcb), precision=lax.Precision.HIGHEST,
                  preferred_element_type=F32)
    b_r = jnp.dot(_log_sigmoid(grb), (row <= col).astype(F32), precision=lax.Precision.HIGHEST,
                  preferred_element_type=F32)

    v_all = v_ref[...]
    hs = []
    for h in range(ML_HEADS):
        q = act[:, h * ML_QK_DIM:(h + 1) * ML_QK_DIM]
        k = act[:, ML_HEADS * ML_QK_DIM + h * ML_QK_DIM:ML_HEADS * ML_QK_DIM + (h + 1) * ML_QK_DIM]
        k = k * (ML_QK_DIM ** -0.5)
        qb = q.astype(BF16)
        vb = v_all[:, h * ML_V_DIM:(h + 1) * ML_V_DIM].astype(BF16)
        ig_r = grb[h:h + 1, :]
        ig_c = gcb[:, h:h + 1]
        bh_r = b_r[ML_HEADS + h:ML_HEADS + h + 1, :]
        bh_c = b_c[:, ML_HEADS + h:ML_HEADS + h + 1]
        m_prev = m_s[:, h:h + 1]
        n_prev = n_s[h:h + 1, :]
        c_prev = c_s[h]

        d = jnp.where(causal, bh_c - bh_r + ig_r, NEG_INF)
        inter = bh_c + m_prev
        m_t = jnp.maximum(inter, jnp.max(d, axis=-1, keepdims=True))
        w_intra = jnp.exp(d - m_t)
        w_inter = jnp.exp(inter - m_t)
        s = lax.dot_general(qb, k.astype(BF16), NT, preferred_element_type=F32) * w_intra
        num = (jnp.dot(s.astype(BF16), vb, preferred_element_type=F32)
               + w_inter * lax.dot_general(qb, c_prev.astype(BF16), NT, preferred_element_type=F32))
        den = (jnp.sum(s, axis=-1, keepdims=True)
               + w_inter * jnp.sum(q * n_prev, axis=-1, keepdims=True))
        hs.append(num / jnp.maximum(jnp.abs(den), jnp.exp(-m_t)))

        m_new = m_t[L - 1:L, :]
        b_last = bh_c[L - 1:L, :]
        w_state = jnp.exp(b_last - bh_c + ig_c - m_new)
        decay = jnp.exp(b_last + m_prev - m_new)
        kw = k * w_state
        c_s[h] = decay * c_prev + lax.dot_general(vb, kw.astype(BF16), TN, preferred_element_type=F32)
        n_s[h:h + 1, :] = decay * n_prev + jnp.sum(kw, axis=0, keepdims=True)
        m_s[:, h:h + 1] = m_new

    gated = jax.nn.sigmoid(og_ref[...]) * jnp.concatenate(hs, axis=-1)
    out_ref[...] = (_rms(gated) * mlg_ref[...]).astype(BF16)

    @pl.when(c == pl.num_programs(1) - 1)
    def _():
        cn_ref[...] = c_s[...]
        nn_ref[...] = n_s[...]
        mn_ref[...] = m_s[...]
        convn_ref[...] = tail


def _mlstm(proj, gates, row0, n_streams, seq, L, bias_c, bias_r, conv_w, conv_b, ml_g, c0, n0, m0, conv0):
    nc = seq // L
    base = row0 // L
    rows = n_streams * seq
    g_rows = gates[row0:row0 + rows, :N_GATES].reshape(n_streams * nc, L, N_GATES).transpose(0, 2, 1)

    def tok(piece):
        return pl.BlockSpec((None, L, PIECE), lambda s, c: (piece, base + s * nc + c, 0))

    def const(shape):
        return pl.BlockSpec(shape, lambda s, c: (0,) * len(shape))

    state_specs = (
        pl.BlockSpec((None, ML_HEADS, ML_V_DIM, ML_QK_DIM), lambda s, c: (s, 0, 0, 0)),
        pl.BlockSpec((None, ML_HEADS, ML_QK_DIM), lambda s, c: (s, 0, 0)),
        pl.BlockSpec((None, 1, ML_HEADS), lambda s, c: (s, 0, 0)),
        pl.BlockSpec((None, CONV_W - 1, ML_CONV_CH), lambda s, c: (s, 0, 0)),
    )
    return pl.pallas_call(
        functools.partial(_mlstm_kernel, L=L),
        out_shape=(
            jax.ShapeDtypeStruct((rows, ML_WIDTH), BF16),
            jax.ShapeDtypeStruct((n_streams, ML_HEADS, ML_V_DIM, ML_QK_DIM), F32),
            jax.ShapeDtypeStruct((n_streams, ML_HEADS, ML_QK_DIM), F32),
            jax.ShapeDtypeStruct((n_streams, 1, ML_HEADS), F32),
            jax.ShapeDtypeStruct((n_streams, CONV_W - 1, ML_CONV_CH), F32),
        ),
        grid=(n_streams, nc),
        in_specs=[
            tok(0), tok(1), tok(2),
            pl.BlockSpec((L, LANES), lambda s, c: (base + s * nc + c, 0)),
            pl.BlockSpec((None, N_GATES, L), lambda s, c: (s * nc + c, 0, 0)),
            const((1, LANES)), const((N_GATES, 1)), const((CONV_W, ML_CONV_CH)), const((1, ML_CONV_CH)),
            const((1, ML_WIDTH)),
            *state_specs,
        ],
        out_specs=(pl.BlockSpec((L, ML_WIDTH), lambda s, c: (s * nc + c, 0)), *state_specs),
        scratch_shapes=[
            pltpu.VMEM((CONV_PAD + L, ML_CONV_CH), F32),
            pltpu.VMEM((ML_HEADS, ML_V_DIM, ML_QK_DIM), F32),
            pltpu.VMEM((ML_HEADS, ML_QK_DIM), F32),
            pltpu.VMEM((1, ML_HEADS), F32),
        ],
        compiler_params=_params("arbitrary", "arbitrary"),
        name="mlstm",
    )(proj, proj, proj, gates, g_rows, bias_c, bias_r, conv_w, conv_b, ml_g, c0, n0, m0, conv0)


def _lambda(lq1_ref, lk1_ref, lq2_ref, lk2_ref, lam_init):
    return (jnp.exp(jnp.sum(lq1_ref[...] * lk1_ref[...], axis=-1, keepdims=True))
            - jnp.exp(jnp.sum(lq2_ref[...] * lk2_ref[...], axis=-1, keepdims=True)) + lam_init)


def _softmax_step(qb, kb, vb, mask, m_ref, l_ref, a_ref):
    s = lax.dot_general(qb, kb, NT, preferred_element_type=F32)
    if mask is not None:
        s = jnp.where(mask, s, NEG_INF)
    m_prev = m_ref[...]
    m_new = jnp.maximum(m_prev, jnp.max(s, axis=-1, keepdims=True))
    alpha = jnp.exp(m_prev - m_new)
    p = jnp.exp(s - m_new)
    l_ref[...] = alpha * l_ref[...] + jnp.sum(p, axis=-1, keepdims=True)
    a_ref[...] = alpha * a_ref[...] + jnp.dot(p.astype(BF16), vb, preferred_element_type=F32)
    m_ref[...] = m_new


def _da_finish(a1, l1, a2, l2, lam, g, lam_init):
    att = a1 / l1 - lam * (a2 / l2)
    return ((_rms(att) * g) * (1.0 - lam_init)).astype(BF16)


def _attn_prompt_kernel(q_ref, k_ref, v_ref, lq1_ref, lk1_ref, lq2_ref, lk2_ref, g_ref, o_ref,
                        qb_ref, m1, l1, a1, m2, l2, a2, *, lam_init, tq):
    i = pl.program_id(2)
    j = pl.program_id(3)
    hd = DA_HEAD_DIM

    @pl.when(j == 0)
    def _():
        qb_ref[...] = (q_ref[...] * (hd ** -0.5)).astype(BF16)
        for m, l, a in ((m1, l1, a1), (m2, l2, a2)):
            m[...] = jnp.full_like(m, NEG_INF)
            l[...] = jnp.zeros_like(l)
            a[...] = jnp.zeros_like(a)

    def step(mask):
        kb = k_ref[...].astype(BF16)
        vb = v_ref[...].astype(BF16)
        _softmax_step(qb_ref[:, :hd], kb[:, :hd], vb, mask, m1, l1, a1)
        _softmax_step(qb_ref[:, hd:], kb[:, hd:], vb, mask, m2, l2, a2)

    @pl.when(j < i)
    def _():
        step(None)

    @pl.when(j == i)
    def _():
        rc = lax.broadcasted_iota(jnp.int32, (tq, tq), 0) // CHUNK
        cc = lax.broadcasted_iota(jnp.int32, (tq, tq), 1) // CHUNK
        step(rc >= cc)
        lam = _lambda(lq1_ref, lk1_ref, lq2_ref, lk2_ref, lam_init)
        o_ref[...] = _da_finish(a1[...], l1[...], a2[...], l2[...], lam, g_ref[...], lam_init)


def _attn_prompt(proj, n_streams, seq, lq1, lk1, lq2, lk2, da_g, lam_init):
    tq = _tile(seq, 512)
    nq = seq // tq
    dq = 2 * DA_HEAD_DIM

    def const(shape):
        return pl.BlockSpec(shape, lambda b, h, i, j: (0,) * len(shape))

    return pl.pallas_call(
        functools.partial(_attn_prompt_kernel, lam_init=lam_init, tq=tq),
        out_shape=jax.ShapeDtypeStruct((n_streams * seq, DA_WIDTH), BF16),
        grid=(n_streams, DA_HEADS, nq, nq),
        in_specs=[
            pl.BlockSpec((None, tq, dq), lambda b, h, i, j: (3, b * nq + i, h)),
            pl.BlockSpec((None, tq, dq), lambda b, h, i, j: (4, b * nq + jnp.minimum(i, j), h)),
            pl.BlockSpec((None, tq, DA_V_DIM), lambda b, h, i, j: (5, b * nq + jnp.minimum(i, j), h)),
            const((1, DA_HEAD_DIM)), const((1, DA_HEAD_DIM)), const((1, DA_HEAD_DIM)), const((1, DA_HEAD_DIM)),
            const((1, DA_V_DIM)),
        ],
        out_specs=pl.BlockSpec((tq, DA_V_DIM), lambda b, h, i, j: (b * nq + i, h)),
        scratch_shapes=[
            pltpu.VMEM((tq, dq), BF16),
            pltpu.VMEM((tq, 1), F32), pltpu.VMEM((tq, 1), F32), pltpu.VMEM((tq, DA_V_DIM), F32),
            pltpu.VMEM((tq, 1), F32), pltpu.VMEM((tq, 1), F32), pltpu.VMEM((tq, DA_V_DIM), F32),
        ],
        compiler_params=_params("parallel", "parallel", "arbitrary", "arbitrary"),
        name="attn_prompt",
    )(proj, proj, proj, lq1, lk1, lq2, lk2, da_g)


def _attn_sample_kernel(q_ref, kn_ref, vn_ref, ck_ref, cv_ref, lq1_ref, lk1_ref, lq2_ref, lk2_ref, g_ref,
                        o_ref, qb_ref, m_s, l_s, a_s, *, lam_init):
    j = pl.program_id(1)
    hd = DA_HEAD_DIM

    @pl.when(j == 0)
    def _():
        qb_ref[...] = (q_ref[...] * (hd ** -0.5)).astype(BF16)
        m_s[...] = jnp.full_like(m_s, NEG_INF)
        l_s[...] = jnp.zeros_like(l_s)
        a_s[...] = jnp.zeros_like(a_s)

    def update(k_src, v_src):
        for h in range(DA_HEADS):
            vb = v_src[:, h * DA_V_DIM:(h + 1) * DA_V_DIM].astype(BF16)
            for t in range(2):
                lo = h * 2 * hd + t * hd
                kb = k_src[:, lo:lo + hd].astype(BF16)
                idx = 2 * h + t
                _softmax_step(qb_ref[:, lo:lo + hd], kb, vb, None, m_s.at[idx], l_s.at[idx], a_s.at[idx])

    update(ck_ref, cv_ref)

    @pl.when(j == pl.num_programs(1) - 1)
    def _():
        update(kn_ref, vn_ref)
        lam = _lambda(lq1_ref, lk1_ref, lq2_ref, lk2_ref, lam_init)
        for h in range(DA_HEADS):
            o_ref[:, h * DA_V_DIM:(h + 1) * DA_V_DIM] = _da_finish(
                a_s[2 * h], l_s[2 * h], a_s[2 * h + 1], l_s[2 * h + 1], lam, g_ref[...], lam_init)


def _attn_sample(proj, row0, n_streams, seq, cache_k, cache_v, li, lq1, lk1, lq2, lk2, da_g, lam_init):
    past = cache_k.shape[2]
    tk = _tile(past, 512)
    base = row0 // seq

    def const(shape):
        return pl.BlockSpec(shape, lambda s, j: (0,) * len(shape))

    def new(piece):
        return pl.BlockSpec((None, seq, PIECE), lambda s, j: (piece, base + s, 0))

    cache_spec = pl.BlockSpec((None, None, tk, PIECE), lambda s, j: (li, s, j, 0))
    return pl.pallas_call(
        functools.partial(_attn_sample_kernel, lam_init=lam_init),
        out_shape=jax.ShapeDtypeStruct((n_streams * seq, DA_WIDTH), BF16),
        grid=(n_streams, past // tk),
        in_specs=[
            new(3), new(4), new(5), cache_spec, cache_spec,
            const((1, DA_HEAD_DIM)), const((1, DA_HEAD_DIM)), const((1, DA_HEAD_DIM)), const((1, DA_HEAD_DIM)),
            const((1, DA_V_DIM)),
        ],
        out_specs=pl.BlockSpec((seq, DA_WIDTH), lambda s, j: (s, 0)),
        scratch_shapes=[
            pltpu.VMEM((seq, DA_QK_WIDTH), BF16),
            pltpu.VMEM((2 * DA_HEADS, seq, 1), F32),
            pltpu.VMEM((2 * DA_HEADS, seq, 1), F32),
            pltpu.VMEM((2 * DA_HEADS, seq, DA_V_DIM), F32),
        ],
        compiler_params=_params("parallel", "arbitrary"),
        name="attn_sample",
    )(proj, proj, proj, cache_k, cache_v, lq1, lk1, lq2, lk2, da_g)


def _outproj_kernel(x_ref, ml_ref, da_ref, w1_ref, w2_ref, post_ref, o_ref):
    y = (jnp.dot(ml_ref[...], w1_ref[...], preferred_element_type=F32)
         + jnp.dot(da_ref[...], w2_ref[...], preferred_element_type=F32))
    o_ref[...] = x_ref[...] + _rms(y) * post_ref[...]


def _outproj(x, ml_out, da_out, w_out, post_g):
    T, D = x.shape
    tm = _tile(T, 512)
    return pl.pallas_call(
        _outproj_kernel,
        out_shape=jax.ShapeDtypeStruct((T, D), F32),
        grid=(T // tm,),
        in_specs=[
            pl.BlockSpec((tm, D), lambda i: (i, 0)),
            pl.BlockSpec((tm, ML_WIDTH), lambda i: (i, 0)),
            pl.BlockSpec((tm, DA_WIDTH), lambda i: (i, 0)),
            pl.BlockSpec((ML_WIDTH, D), lambda i: (0, 0)),
            pl.BlockSpec((DA_WIDTH, D), lambda i: (1, 0)),
            pl.BlockSpec((1, D), lambda i: (0, 0)),
        ],
        out_specs=pl.BlockSpec((tm, D), lambda i: (i, 0)),
        compiler_params=_params("parallel"),
        name="outproj",
    )(x, ml_out, da_out, w_out, w_out, post_g)


def _ffn_weights(wg, wu, wd):
    f = wg.shape[1]
    fp = -(-f // 512) * 512 if f > 512 else f
    pad = fp - f
    return (jnp.pad(wg.astype(BF16), ((0, 0), (0, pad))), jnp.pad(wu.astype(BF16), ((0, 0), (0, pad))),
            jnp.pad(wd.astype(BF16), ((0, pad), (0, 0))))


def kernel(x_prompt, x_sample, cache_k, cache_v, state_C, state_n, state_m, state_conv,
           ffn1_pre_g, ffn1_wg, ffn1_wu, ffn1_wd, ffn1_post_g,
           mix_pre_g, w_in, b_i, b_f, conv_w, conv_b, ml_norm_g,
           lam_q1, lam_k1, lam_q2, lam_k2, da_norm_g, w_out, mix_post_g,
           ffn2_pre_g, ffn2_wg, ffn2_wu, ffn2_wd, ffn2_post_g):
    Bp, S, D = x_prompt.shape
    Bs, Ls, _ = x_sample.shape
    depth = w_in.shape[0]
    past = cache_k.shape[2]
    Tp, Ts = Bp * S, Bs * Ls
    assert Ls == CHUNK and past % CHUNK == 0 and S % CHUNK == 0 and Tp % Ls == 0
    Lp = 256 if S % 256 == 0 else CHUNK

    x = jnp.concatenate([x_prompt.reshape(Tp, D), x_sample.reshape(Ts, D)], axis=0)
    ck = cache_k.reshape(depth, Bs, past, DA_QK_WIDTH)
    cv = cache_v.reshape(depth, Bs, past, DA_WIDTH)
    zero_c = jnp.zeros((Bp, ML_HEADS, ML_V_DIM, ML_QK_DIM), F32)
    zero_n = jnp.zeros((Bp, ML_HEADS, ML_QK_DIM), F32)
    zero_m = jnp.zeros((Bp, 1, ML_HEADS), F32)
    zero_conv = jnp.zeros((Bp, CONV_W - 1, ML_CONV_CH), F32)

    def row(v):
        return v.reshape(1, -1)

    p_st, s_st = [], []
    for li in range(depth):
        lam_init = 0.8 - 0.6 * math.exp(-0.3 * li)
        x = _ffn(x, row(ffn1_pre_g[li]), *_ffn_weights(ffn1_wg[li], ffn1_wu[li], ffn1_wd[li]), row(ffn1_post_g[li]))

        w = w_in[li]
        w_main = jnp.concatenate([w[:, :GATE_OFF], w[:, GATE_OFF + N_GATES:]], axis=1).astype(BF16)
        w_gate = jnp.pad(w[:, GATE_OFF:GATE_OFF + N_GATES].astype(BF16), ((0, 0), (0, LANES - N_GATES)))
        proj, gates = _inproj(x, row(mix_pre_g[li]), w_main, w_gate)

        bias = jnp.concatenate([b_i[li], b_f[li]])
        bias_c = jnp.pad(bias, (0, LANES - N_GATES)).reshape(1, LANES)
        bias_r = bias.reshape(N_GATES, 1)
        ml_args = (bias_c, bias_r, conv_w[li], row(conv_b[li]), row(ml_norm_g[li]))
        ml_p, p_c, p_n, p_m, p_conv = _mlstm(proj, gates, 0, Bp, S, Lp, *ml_args, zero_c, zero_n, zero_m, zero_conv)
        ml_s, s_c, s_n, s_m, s_conv = _mlstm(proj, gates, Tp, Bs, Ls, Ls, *ml_args, state_C[li], state_n[li],
                                             state_m[li].reshape(Bs, 1, ML_HEADS), state_conv[li])

        lam_args = (row(lam_q1[li]), row(lam_k1[li]), row(lam_q2[li]), row(lam_k2[li]), row(da_norm_g[li]), lam_init)
        da_p = _attn_prompt(proj, Bp, S, *lam_args)
        da_s = _attn_sample(proj, Tp, Bs, Ls, ck, cv, li, *lam_args)

        x = _outproj(x, jnp.concatenate([ml_p, ml_s], axis=0), jnp.concatenate([da_p, da_s], axis=0),
                     w_out[li].astype(BF16), row(mix_post_g[li]))
        x = _ffn(x, row(ffn2_pre_g[li]), *_ffn_weights(ffn2_wg[li], ffn2_wu[li], ffn2_wd[li]), row(ffn2_post_g[li]))

        dk, dv = proj[4], proj[5]
        p_st.append((dk[:Tp].reshape(Bp, S, DA_HEADS, 2 * DA_HEAD_DIM), dv[:Tp].reshape(Bp, S, DA_HEADS, DA_V_DIM),
                     p_c, p_n, p_m.reshape(Bp, ML_HEADS), p_conv))
        s_st.append((dk[Tp:].reshape(Bs, Ls, DA_HEADS, 2 * DA_HEAD_DIM), dv[Tp:].reshape(Bs, Ls, DA_HEADS, DA_V_DIM),
                     s_c, s_n, s_m.reshape(Bs, ML_HEADS), s_conv))

    p_out = tuple(jnp.stack(e) for e in zip(*p_st))
    s_out = tuple(jnp.stack(e) for e in zip(*s_st))
    return (x[:Tp].reshape(Bp, S, D), x[Tp:].reshape(Bs, Ls, D), *p_out, *s_out)
```

```python
import functools
import math

import jax
import jax.numpy as jnp
from jax import lax
from jax.experimental import pallas as pl
from jax.experimental.pallas import tpu as pltpu

F32 = jnp.float32
BF16 = jnp.bfloat16

CHUNK = 64
ML_HEADS = 4
ML_QK_DIM = 128
ML_V_DIM = 256
DA_HEADS = 4
DA_HEAD_DIM = 128
DA_V_DIM = 2 * DA_HEAD_DIM
CONV_W = 4
EPS = 1e-6
ML_WIDTH = ML_HEADS * ML_V_DIM
DA_WIDTH = DA_HEADS * DA_V_DIM
ML_CONV_CH = 2 * ML_HEADS * ML_QK_DIM
DA_QK_WIDTH = DA_HEADS * 2 * DA_HEAD_DIM
PIECE = 1024
N_PIECES = 6
N_GATES = 2 * ML_HEADS
GATE_OFF = 3 * PIECE
assert ML_CONV_CH == ML_WIDTH == DA_QK_WIDTH == DA_WIDTH == PIECE

LANES = 128
CONV_PAD = 8
VMEM_LIMIT = 56 * 1024 * 1024

NEG_INF = float("-inf")
NT = (((1,), (1,)), ((), ()))
TN = (((0,), (0,)), ((), ()))


def _params(*sem):
    return pltpu.CompilerParams(dimension_semantics=sem, vmem_limit_bytes=VMEM_LIMIT)


def _rms(xf):
    return xf * lax.rsqrt(jnp.mean(xf * xf, axis=-1, keepdims=True) + EPS)


def _silu(x):
    return x * jax.nn.sigmoid(x)


def _log_sigmoid(x):
    return jnp.minimum(x, 0.0) - jnp.log1p(jnp.exp(-jnp.abs(x)))


def _tile(n, pref):
    return pref if n % pref == 0 else n


def _ffn_kernel(x_ref, pre_ref, wg_ref, wu_ref, wd_ref, post_ref, o_ref, xn_ref, acc_ref):
    j = pl.program_id(1)

    @pl.when(j == 0)
    def _():
        xn_ref[...] = (_rms(x_ref[...]) * pre_ref[...]).astype(BF16)
        acc_ref[...] = jnp.zeros_like(acc_ref)

    xn = xn_ref[...]
    g = jnp.dot(xn, wg_ref[...], preferred_element_type=F32)
    u = jnp.dot(xn, wu_ref[...], preferred_element_type=F32)
    h = (_silu(g) * u).astype(BF16)
    acc_ref[...] += jnp.dot(h, wd_ref[...], preferred_element_type=F32)

    @pl.when(j == pl.num_programs(1) - 1)
    def _():
        o_ref[...] = x_ref[...] + 0.5 * (_rms(acc_ref[...]) * post_ref[...])


def _ffn(x, pre_g, wg, wu, wd, post_g):
    T, D = x.shape
    Fp = wg.shape[1]
    tm, tf = _tile(T, 512), _tile(Fp, 512)
    return pl.pallas_call(
        _ffn_kernel,
        out_shape=jax.ShapeDtypeStruct((T, D), F32),
        grid=(T // tm, Fp // tf),
        in_specs=[
            pl.BlockSpec((tm, D), lambda i, j: (i, 0)),
            pl.BlockSpec((1, D), lambda i, j: (0, 0)),
            pl.BlockSpec((D, tf), lambda i, j: (0, j)),
            pl.BlockSpec((D, tf), lambda i, j: (0, j)),
            pl.BlockSpec((tf, D), lambda i, j: (j, 0)),
            pl.BlockSpec((1, D), lambda i, j: (0, 0)),
        ],
        out_specs=pl.BlockSpec((tm, D), lambda i, j: (i, 0)),
        scratch_shapes=[pltpu.VMEM((tm, D), BF16), pltpu.VMEM((tm, D), F32)],
        compiler_params=_params("parallel", "arbitrary"),
        name="ffn",
    )(x, pre_g, wg, wu, wd, post_g)


def _inproj_kernel(x_ref, pre_ref, w_ref, wgate_ref, o_ref, gate_ref, xn_ref):
    @pl.when(pl.program_id(1) == 0)
    def _():
        xn = (_rms(x_ref[...]) * pre_ref[...]).astype(BF16)
        xn_ref[...] = xn
        gate_ref[...] = jnp.dot(xn, wgate_ref[...], preferred_element_type=F32)

    o_ref[...] = jnp.dot(xn_ref[...], w_ref[...], preferred_element_type=F32)


def _inproj(x, pre_g, w_main, w_gate):
    T, D = x.shape
    tm = _tile(T, 512)
    return pl.pallas_call(
        _inproj_kernel,
        out_shape=(jax.ShapeDtypeStruct((N_PIECES, T, PIECE), F32),
                   jax.ShapeDtypeStruct((T, LANES), F32)),
        grid=(T // tm, N_PIECES),
        in_specs=[
            pl.BlockSpec((tm, D), lambda i, j: (i, 0)),
            pl.BlockSpec((1, D), lambda i, j: (0, 0)),
            pl.BlockSpec((D, PIECE), lambda i, j: (0, j)),
            pl.BlockSpec((D, LANES), lambda i, j: (0, 0)),
        ],
        out_specs=(pl.BlockSpec((None, tm, PIECE), lambda i, j: (j, i, 0)),
                   pl.BlockSpec((tm, LANES), lambda i, j: (i, 0))),
        scratch_shapes=[pltpu.VMEM((tm, D), BF16)],
        compiler_params=_params("parallel", "arbitrary"),
        name="inproj",
    )(x, pre_g, w_main, w_gate)


def _mlstm_kernel(qk_ref, v_ref, og_ref, gc_ref, gr_ref, bc_ref, br_ref, cw_ref, cb_ref, mlg_ref,
                  c0_ref, n0_ref, m0_ref, conv0_ref,
                  out_ref, cn_ref, nn_ref, mn_ref, convn_ref,
                  xp_ref, c_s, n_s, m_s, *, L):
    c = pl.program_id(1)

    @pl.when(c == 0)
    def _():
        c_s[...] = c0_ref[...]
        n_s[...] = n0_ref[...]
        m_s[...] = m0_ref[...]
        xp_ref[0:CONV_PAD, :] = jnp.zeros((CONV_PAD, ML_CONV_CH), F32)
        xp_ref[CONV_PAD - (CONV_W - 1):CONV_PAD, :] = conv0_ref[...]

    qk = qk_ref[...]
    xp_ref[CONV_PAD:CONV_PAD + L, :] = qk
    cw = cw_ref[...]
    conv = cb_ref[...] + qk * cw[CONV_W - 1:CONV_W, :]
    for t in range(CONV_W - 1):
        off = CONV_PAD - (CONV_W - 1) + t
        conv = conv + xp_ref[off:off + L, :] * cw[t:t + 1, :]
    tail = xp_ref[CONV_PAD + L - (CONV_W - 1):CONV_PAD + L, :]
    xp_ref[CONV_PAD - (CONV_W - 1):CONV_PAD, :] = tail
    act = _silu(conv)

    gcb = gc_ref[...] + bc_ref[...]
    grb = gr_ref[...] + br_ref[...]
    row = lax.broadcasted_iota(jnp.int32, (L, L), 0)
    col = lax.broadcasted_iota(jnp.int32, (L, L), 1)
    causal = col <= row
    b_c = jnp.dot(causal.astype(F32), _log_sigmoid(gcb), precision=lax.Precision.HIGHEST,
                  preferred_element_type=F32)
    b_r = jnp.dot(_log_sigmoid(grb), (row <= col).astype(F32), precision=lax.Precision.HIGHEST,
                  preferred_element_type=F32)

    v_all = v_ref[...]
    hs = []
    for h in range(ML_HEADS):
        q = act[:, h * ML_QK_DIM:(h + 1) * ML_QK_DIM]
        k = act[:, ML_HEADS * ML_QK_DIM + h * ML_QK_DIM:ML_HEADS * ML_QK_DIM + (h + 1) * ML_QK_DIM]
        k = k * (ML_QK_DIM ** -0.5)
        qb = q.astype(BF16)
        vb = v_all[:, h * ML_V_DIM:(h + 1) * ML_V_DIM].astype(BF16)
        ig_r = grb[h:h + 1, :]
        ig_c = gcb[:, h:h + 1]
        bh_r = b_r[ML_HEADS + h:ML_HEADS + h + 1, :]
        bh_c = b_c[:, ML_HEADS + h:ML_HEADS + h + 1]
        m_prev = m_s[:, h:h + 1]
        n_prev = n_s[h:h + 1, :]
        c_prev = c_s[h]

        d = jnp.where(causal, bh_c - bh_r + ig_r, NEG_INF)
        inter = bh_c + m_prev
        m_t = jnp.maximum(inter, jnp.max(d, axis=-1, keepdims=True))
        w_intra = jnp.exp(d - m_t)
        w_inter = jnp.exp(inter - m_t)
        s = lax.dot_general(qb, k.astype(BF16), NT, preferred_element_type=F32) * w_intra
        num = (jnp.dot(s.astype(BF16), vb, preferred_element_type=F32)
               + w_inter * lax.dot_general(qb, c_prev.astype(BF16), NT, preferred_element_type=F32))
        den = (jnp.sum(s, axis=-1, keepdims=True)
               + w_inter * jnp.sum(q * n_prev, axis=-1, keepdims=True))
        hs.append(num / jnp.maximum(jnp.abs(den), jnp.exp(-m_t)))

        m_new = m_t[L - 1:L, :]
        b_last = bh_c[L - 1:L, :]
        w_state = jnp.exp(b_last - bh_c + ig_c - m_new)
        decay = jnp.exp(b_last + m_prev - m_new)
        kw = k * w_state
        c_s[h] = decay * c_prev + lax.dot_general(vb, kw.astype(BF16), TN, preferred_element_type=F32)
        n_s[h:h + 1, :] = decay * n_prev + jnp.sum(kw, axis=0, keepdims=True)
        m_s[:, h:h + 1] = m_new

    gated = jax.nn.sigmoid(og_ref[...]) * jnp.concatenate(hs, axis=-1)
    out_ref[...] = (_rms(gated) * mlg_ref[...]).astype(BF16)

    @pl.when(c == pl.num_programs(1) - 1)
    def _():
        cn_ref[...] = c_s[...]
        nn_ref[...] = n_s[...]
        mn_ref[...] = m_s[...]
        convn_ref[...] = tail


def _mlstm(proj, gates, row0, n_streams, seq, L, bias_c, bias_r, conv_w, conv_b, ml_g, c0, n0, m0, conv0):
    nc = seq // L
    base = row0 // L
    rows = n_streams * seq
    g_rows = gates[row0:row0 + rows, :N_GATES].reshape(n_streams * nc, L, N_GATES).transpose(0, 2, 1)

    def tok(piece):
        return pl.BlockSpec((None, L, PIECE), lambda s, c: (piece, base + s * nc + c, 0))

    def const(shape):
        return pl.BlockSpec(shape, lambda s, c: (0,) * len(shape))

    state_specs = (
        pl.BlockSpec((None, ML_HEADS, ML_V_DIM, ML_QK_DIM), lambda s, c: (s, 0, 0, 0)),
        pl.BlockSpec((None, ML_HEADS, ML_QK_DIM), lambda s, c: (s, 0, 0)),
        pl.BlockSpec((None, 1, ML_HEADS), lambda s, c: (s, 0, 0)),
        pl.BlockSpec((None, CONV_W - 1, ML_CONV_CH), lambda s, c: (s, 0, 0)),
    )
    return pl.pallas_call(
        functools.partial(_mlstm_kernel, L=L),
        out_shape=(
            jax.ShapeDtypeStruct((rows, ML_WIDTH), BF16),
            jax.ShapeDtypeStruct((n_streams, ML_HEADS, ML_V_DIM, ML_QK_DIM), F32),
            jax.ShapeDtypeStruct((n_streams, ML_HEADS, ML_QK_DIM), F32),
            jax.ShapeDtypeStruct((n_streams, 1, ML_HEADS), F32),
            jax.ShapeDtypeStruct((n_streams, CONV_W - 1, ML_CONV_CH), F32),
        ),
        grid=(n_streams, nc),
        in_specs=[
            tok(0), tok(1), tok(2),
            pl.BlockSpec((L, LANES), lambda s, c: (base + s * nc + c, 0)),
            pl.BlockSpec((None, N_GATES, L), lambda s, c: (s * nc + c, 0, 0)),
            const((1, LANES)), const((N_GATES, 1)), const((CONV_W, ML_CONV_CH)), const((1, ML_CONV_CH)),
            const((1, ML_WIDTH)),
            *state_specs,
        ],
        out_specs=(pl.BlockSpec((L, ML_WIDTH), lambda s, c: (s * nc + c, 0)), *state_specs),
        scratch_shapes=[
            pltpu.VMEM((CONV_PAD + L, ML_CONV_CH), F32),
            pltpu.VMEM((ML_HEADS, ML_V_DIM, ML_QK_DIM), F32),
            pltpu.VMEM((ML_HEADS, ML_QK_DIM), F32),
            pltpu.VMEM((1, ML_HEADS), F32),
        ],
        compiler_params=_params("arbitrary", "arbitrary"),
        name="mlstm",
    )(proj, proj, proj, gates, g_rows, bias_c, bias_r, conv_w, conv_b, ml_g, c0, n0, m0, conv0)


def _lambda(lq1_ref, lk1_ref, lq2_ref, lk2_ref, lam_init):
    return (jnp.exp(jnp.sum(lq1_ref[...] * lk1_ref[...], axis=-1, keepdims=True))
            - jnp.exp(jnp.sum(lq2_ref[...] * lk2_ref[...], axis=-1, keepdims=True)) + lam_init)


def _lane_tiles(x):
    return [x[:, c * LANES:(c + 1) * LANES] for c in range(x.shape[1] // LANES)]


def _fold(op, x):
    return functools.reduce(op, _lane_tiles(x))


def _exp2_rows(s, m_b):
    return jnp.concatenate([jnp.exp2(t - m_b) for t in _lane_tiles(s)], axis=-1)


def _replicate(col):
    return jnp.broadcast_to(col, (col.shape[0], LANES))


def _da_finish(out1, out2, lam, g, lam_init):
    att = out1 - lam * out2
    return ((_rms(att) * g) * (1.0 - lam_init)).astype(BF16)


LOG2E = math.log2(math.e)
Q_SCALE = DA_HEAD_DIM ** -0.5 * LOG2E


def _attn_prompt_kernel(q_ref, k_ref, v_ref, lq1_ref, lk1_ref, lq2_ref, lk2_ref, g_ref, o_ref,
                        kb_ref, vb_ref, s_ref, m_ref, l_ref, acc_ref, *, lam_init, tq):
    i = pl.program_id(2)
    hd = DA_HEAD_DIM

    @pl.when(i == 0)
    def _():
        kb_ref[...] = k_ref[...].astype(BF16)
        vb_ref[...] = v_ref[...].astype(BF16)

    qs = (q_ref[...] * Q_SCALE).astype(BF16)

    def rows(ref, j):
        return ref[pl.ds(pl.multiple_of(j * tq, tq), tq), :]

    def scores(kj, t):
        return lax.dot_general(qs[:, t * hd:(t + 1) * hd], kj[:, t * hd:(t + 1) * hd], NT,
                               preferred_element_type=F32)

    def note_scores(j, t, s):
        s_ref[t, j] = s
        m_ref[t] = jnp.maximum(m_ref[t], _fold(jnp.maximum, s))

    m_ref[...] = jnp.full_like(m_ref, NEG_INF)

    def sweep_max(j, carry):
        kj = rows(kb_ref, j)
        for t in range(2):
            note_scores(j, t, scores(kj, t))
        return carry

    lax.fori_loop(0, i, sweep_max, 0)
    diag = (lax.broadcasted_iota(jnp.int32, (tq, tq), 0) // CHUNK
            >= lax.broadcasted_iota(jnp.int32, (tq, tq), 1) // CHUNK)
    kd = rows(kb_ref, i)
    for t in range(2):
        note_scores(i, t, jnp.where(diag, scores(kd, t), NEG_INF))
        m_ref[t] = _replicate(jnp.max(m_ref[t], axis=-1, keepdims=True))

    l_ref[...] = jnp.zeros_like(l_ref)
    acc_ref[...] = jnp.zeros_like(acc_ref)

    def sweep_pv(j, carry):
        vj = rows(vb_ref, j)
        for t in range(2):
            p = _exp2_rows(s_ref[t, j], m_ref[t])
            acc_ref[t] += jnp.dot(p.astype(BF16), vj, preferred_element_type=F32)
            l_ref[t] += _fold(jnp.add, p)
        return carry

    lax.fori_loop(0, i + 1, sweep_pv, 0)
    outs = [acc_ref[t] / jnp.sum(l_ref[t], axis=-1, keepdims=True) for t in range(2)]
    lam = _lambda(lq1_ref, lk1_ref, lq2_ref, lk2_ref, lam_init)
    o_ref[...] = _da_finish(outs[0], outs[1], lam, g_ref[...], lam_init)


def _attn_prompt(proj, n_streams, seq, lq1, lk1, lq2, lk2, da_g, lam_init):
    tq = _tile(seq, 512)
    nq = seq // tq
    dq = 2 * DA_HEAD_DIM

    def const(shape):
        return pl.BlockSpec(shape, lambda b, h, i: (0,) * len(shape))

    return pl.pallas_call(
        functools.partial(_attn_prompt_kernel, lam_init=lam_init, tq=tq),
        out_shape=jax.ShapeDtypeStruct((n_streams * seq, DA_WIDTH), BF16),
        grid=(n_streams, DA_HEADS, nq),
        in_specs=[
            pl.BlockSpec((None, tq, dq), lambda b, h, i: (3, b * nq + i, h)),
            pl.BlockSpec((None, seq, dq), lambda b, h, i: (4, b, h)),
            pl.BlockSpec((None, seq, DA_V_DIM), lambda b, h, i: (5, b, h)),
            const((1, DA_HEAD_DIM)), const((1, DA_HEAD_DIM)), const((1, DA_HEAD_DIM)), const((1, DA_HEAD_DIM)),
            const((1, DA_V_DIM)),
        ],
        out_specs=pl.BlockSpec((tq, DA_V_DIM), lambda b, h, i: (b * nq + i, h)),
        scratch_shapes=[
            pltpu.VMEM((seq, dq), BF16),
            pltpu.VMEM((seq, DA_V_DIM), BF16),
            pltpu.VMEM((2, nq, tq, tq), F32),
            pltpu.VMEM((2, tq, LANES), F32),
            pltpu.VMEM((2, tq, LANES), F32),
            pltpu.VMEM((2, tq, DA_V_DIM), F32),
        ],
        compiler_params=_params("parallel", "parallel", "arbitrary"),
        name="attn_prompt",
    )(proj, proj, proj, lq1, lk1, lq2, lk2, da_g)


def _attn_sample_kernel(q_ref, kn_ref, vn_ref, ck_ref, cv_ref, lq1_ref, lk1_ref, lq2_ref, lk2_ref, g_ref,
                        o_ref, kbuf, vbuf, sem, *, lam_init, li):
    hd = DA_HEAD_DIM
    s = pl.program_id(0)
    slot = s % 2

    def cache_copies(stream, slot_, h):
        return (pltpu.make_async_copy(ck_ref.at[li, stream, :, h, :], kbuf.at[slot_, h], sem.at[0, slot_, h]),
                pltpu.make_async_copy(cv_ref.at[li, stream, :, h, :], vbuf.at[slot_, h], sem.at[1, slot_, h]))

    def start_stream(stream, slot_):
        for h in range(DA_HEADS):
            for cp in cache_copies(stream, slot_, h):
                cp.start()

    @pl.when(s == 0)
    def _():
        start_stream(0, 0)

    @pl.when(s + 1 < pl.num_programs(0))
    def _():
        start_stream(s + 1, 1 - slot)

    lam = _lambda(lq1_ref, lk1_ref, lq2_ref, lk2_ref, lam_init)
    for h in range(DA_HEADS):
        cols = slice(h * DA_V_DIM, (h + 1) * DA_V_DIM)
        qs = (q_ref[:, cols] * Q_SCALE).astype(BF16)
        for cp in cache_copies(s, slot, h):
            cp.wait()
        kc, vc = kbuf[slot, h].astype(BF16), vbuf[slot, h].astype(BF16)
        kn, vn = kn_ref[:, cols].astype(BF16), vn_ref[:, cols].astype(BF16)
        outs = []
        for t in range(2):
            sl = slice(t * hd, (t + 1) * hd)
            sc = lax.dot_general(qs[:, sl], kc[:, sl], NT, preferred_element_type=F32)
            sn = lax.dot_general(qs[:, sl], kn[:, sl], NT, preferred_element_type=F32)
            m = jnp.maximum(jnp.max(_fold(jnp.maximum, sc), axis=-1, keepdims=True),
                            jnp.max(sn, axis=-1, keepdims=True))
            pc = _exp2_rows(sc, _replicate(m))
            pn = jnp.exp2(sn - m)
            l = jnp.sum(_fold(jnp.add, pc), axis=-1, keepdims=True) + jnp.sum(pn, axis=-1, keepdims=True)
            acc = (jnp.dot(pc.astype(BF16), vc, preferred_element_type=F32)
                   + jnp.dot(pn.astype(BF16), vn, preferred_element_type=F32))
            outs.append(acc / l)
        o_ref[:, cols] = _da_finish(outs[0], outs[1], lam, g_ref[...], lam_init)


def _attn_sample(proj, row0, n_streams, seq, cache_k, cache_v, li, lq1, lk1, lq2, lk2, da_g, lam_init):
    past = cache_k.shape[2]
    base = row0 // seq
    dq = 2 * DA_HEAD_DIM

    def const(shape):
        return pl.BlockSpec(shape, lambda s: (0,) * len(shape))

    def new(piece):
        return pl.BlockSpec((None, seq, PIECE), lambda s: (piece, base + s, 0))

    cache_spec = pl.BlockSpec(memory_space=pl.ANY)
    return pl.pallas_call(
        functools.partial(_attn_sample_kernel, lam_init=lam_init, li=li),
        out_shape=jax.ShapeDtypeStruct((n_streams * seq, DA_WIDTH), BF16),
        grid=(n_streams,),
        in_specs=[
            new(3), new(4), new(5), cache_spec, cache_spec,
            const((1, DA_HEAD_DIM)), const((1, DA_HEAD_DIM)), const((1, DA_HEAD_DIM)), const((1, DA_HEAD_DIM)),
            const((1, DA_V_DIM)),
        ],
        out_specs=pl.BlockSpec((seq, DA_WIDTH), lambda s: (s, 0)),
        scratch_shapes=[
            pltpu.VMEM((2, DA_HEADS, past, dq), F32),
            pltpu.VMEM((2, DA_HEADS, past, DA_V_DIM), F32),
            pltpu.SemaphoreType.DMA((2, 2, DA_HEADS)),
        ],
        compiler_params=_params("arbitrary"),
        name="attn_sample",
    )(proj, proj, proj, cache_k, cache_v, lq1, lk1, lq2, lk2, da_g)


def _outproj_kernel(x_ref, ml_ref, da_ref, w1_ref, w2_ref, post_ref, o_ref):
    y = (jnp.dot(ml_ref[...], w1_ref[...], preferred_element_type=F32)
         + jnp.dot(da_ref[...], w2_ref[...], preferred_element_type=F32))
    o_ref[...] = x_ref[...] + _rms(y) * post_ref[...]


def _outproj(x, ml_out, da_out, w_out, post_g):
    T, D = x.shape
    tm = _tile(T, 512)
    return pl.pallas_call(
        _outproj_kernel,
        out_shape=jax.ShapeDtypeStruct((T, D), F32),
        grid=(T // tm,),
        in_specs=[
            pl.BlockSpec((tm, D), lambda i: (i, 0)),
            pl.BlockSpec((tm, ML_WIDTH), lambda i: (i, 0)),
            pl.BlockSpec((tm, DA_WIDTH), lambda i: (i, 0)),
            pl.BlockSpec((ML_WIDTH, D), lambda i: (0, 0)),
            pl.BlockSpec((DA_WIDTH, D), lambda i: (1, 0)),
            pl.BlockSpec((1, D), lambda i: (0, 0)),
        ],
        out_specs=pl.BlockSpec((tm, D), lambda i: (i, 0)),
        compiler_params=_params("parallel"),
        name="outproj",
    )(x, ml_out, da_out, w_out, w_out, post_g)


def _ffn_weights(wg, wu, wd):
    f = wg.shape[1]
    fp = -(-f // 512) * 512 if f > 512 else f
    pad = fp - f
    return (jnp.pad(wg.astype(BF16), ((0, 0), (0, pad))), jnp.pad(wu.astype(BF16), ((0, 0), (0, pad))),
            jnp.pad(wd.astype(BF16), ((0, pad), (0, 0))))


def kernel(x_prompt, x_sample, cache_k, cache_v, state_C, state_n, state_m, state_conv,
           ffn1_pre_g, ffn1_wg, ffn1_wu, ffn1_wd, ffn1_post_g,
           mix_pre_g, w_in, b_i, b_f, conv_w, conv_b, ml_norm_g,
           lam_q1, lam_k1, lam_q2, lam_k2, da_norm_g, w_out, mix_post_g,
           ffn2_pre_g, ffn2_wg, ffn2_wu, ffn2_wd, ffn2_post_g):
    Bp, S, D = x_prompt.shape
    Bs, Ls, _ = x_sample.shape
    depth = w_in.shape[0]
    past = cache_k.shape[2]
    Tp, Ts = Bp * S, Bs * Ls
    assert Ls == CHUNK and past % LANES == 0 and S % CHUNK == 0 and Tp % Ls == 0
    Lp = 256 if S % 256 == 0 else CHUNK

    x = jnp.concatenate([x_prompt.reshape(Tp, D), x_sample.reshape(Ts, D)], axis=0)
    zero_c = jnp.zeros((Bp, ML_HEADS, ML_V_DIM, ML_QK_DIM), F32)
    zero_n = jnp.zeros((Bp, ML_HEADS, ML_QK_DIM), F32)
    zero_m = jnp.zeros((Bp, 1, ML_HEADS), F32)
    zero_conv = jnp.zeros((Bp, CONV_W - 1, ML_CONV_CH), F32)

    def row(v):
        return v.reshape(1, -1)

    p_st, s_st = [], []
    for li in range(depth):
        lam_init = 0.8 - 0.6 * math.exp(-0.3 * li)
        x = _ffn(x, row(ffn1_pre_g[li]), *_ffn_weights(ffn1_wg[li], ffn1_wu[li], ffn1_wd[li]), row(ffn1_post_g[li]))

        w = w_in[li]
        w_main = jnp.concatenate([w[:, :GATE_OFF], w[:, GATE_OFF + N_GATES:]], axis=1).astype(BF16)
        w_gate = jnp.pad(w[:, GATE_OFF:GATE_OFF + N_GATES].astype(BF16), ((0, 0), (0, LANES - N_GATES)))
        proj, gates = _inproj(x, row(mix_pre_g[li]), w_main, w_gate)

        bias = jnp.concatenate([b_i[li], b_f[li]])
        bias_c = jnp.pad(bias, (0, LANES - N_GATES)).reshape(1, LANES)
        bias_r = bias.reshape(N_GATES, 1)
        ml_args = (bias_c, bias_r, conv_w[li], row(conv_b[li]), row(ml_norm_g[li]))
        ml_p, p_c, p_n, p_m, p_conv = _mlstm(proj, gates, 0, Bp, S, Lp, *ml_args, zero_c, zero_n, zero_m, zero_conv)
        ml_s, s_c, s_n, s_m, s_conv = _mlstm(proj, gates, Tp, Bs, Ls, Ls, *ml_args, state_C[li], state_n[li],
                                             state_m[li].reshape(Bs, 1, ML_HEADS), state_conv[li])

        lam_args = (row(lam_q1[li]), row(lam_k1[li]), row(lam_q2[li]), row(lam_k2[li]), row(da_norm_g[li]), lam_init)
        da_p = _attn_prompt(proj, Bp, S, *lam_args)
        da_s = _attn_sample(proj, Tp, Bs, Ls, cache_k, cache_v, li, *lam_args)

        x = _outproj(x, jnp.concatenate([ml_p, ml_s], axis=0), jnp.concatenate([da_p, da_s], axis=0),
                     w_out[li].astype(BF16), row(mix_post_g[li]))
        x = _ffn(x, row(ffn2_pre_g[li]), *_ffn_weights(ffn2_wg[li], ffn2_wu[li], ffn2_wd[li]), row(ffn2_post_g[li]))

        dk, dv = proj[4], proj[5]
        p_st.append((dk[:Tp].reshape(Bp, S, DA_HEADS, 2 * DA_HEAD_DIM), dv[:Tp].reshape(Bp, S, DA_HEADS, DA_V_DIM),
                     p_c, p_n, p_m.reshape(Bp, ML_HEADS), p_conv))
        s_st.append((dk[Tp:].reshape(Bs, Ls, DA_HEADS, 2 * DA_HEAD_DIM), dv[Tp:].reshape(Bs, Ls, DA_HEADS, DA_V_DIM),
                     s_c, s_n, s_m.reshape(Bs, ML_HEADS), s_conv))

    p_out = tuple(jnp.stack(e) for e in zip(*p_st))
    s_out = tuple(jnp.stack(e) for e in zip(*s_st))
    return (x[:Tp].reshape(Bp, S, D), x[Tp:].reshape(Bs, Ls, D), *p_out, *s_out)
```

```python
import functools
import math

import jax
import jax.numpy as jnp
from jax import lax
from jax.experimental import pallas as pl
from jax.experimental.pallas import tpu as pltpu

F32 = jnp.float32
BF16 = jnp.bfloat16

CHUNK = 64
ML_HEADS = 4
ML_QK_DIM = 128
ML_V_DIM = 256
DA_HEADS = 4
DA_HEAD_DIM = 128
DA_V_DIM = 2 * DA_HEAD_DIM
CONV_W = 4
EPS = 1e-6
ML_WIDTH = ML_HEADS * ML_V_DIM
DA_WIDTH = DA_HEADS * DA_V_DIM
ML_CONV_CH = 2 * ML_HEADS * ML_QK_DIM
DA_QK_WIDTH = DA_HEADS * 2 * DA_HEAD_DIM
PIECE = 1024
N_PIECES = 6
N_GATES = 2 * ML_HEADS
GATE_OFF = 3 * PIECE
assert ML_CONV_CH == ML_WIDTH == DA_QK_WIDTH == DA_WIDTH == PIECE

LANES = 128
CONV_PAD = 8
VMEM_LIMIT = 56 * 1024 * 1024

NEG_INF = float("-inf")
NT = (((1,), (1,)), ((), ()))
TN = (((0,), (0,)), ((), ()))


def _params(*sem):
    return pltpu.CompilerParams(dimension_semantics=sem, vmem_limit_bytes=VMEM_LIMIT)


def _rms(xf):
    return xf * lax.rsqrt(jnp.mean(xf * xf, axis=-1, keepdims=True) + EPS)


def _silu(x):
    return x * jax.nn.sigmoid(x)


def _log_sigmoid(x):
    return jnp.minimum(x, 0.0) - jnp.log1p(jnp.exp(-jnp.abs(x)))


def _tile(n, pref):
    return pref if n % pref == 0 else n


def _ffn_kernel(x_ref, pre_ref, wg_ref, wu_ref, wd_ref, post_ref, o_ref, xn_ref, *, tail):
    j = pl.program_id(1)
    last = pl.num_programs(1) - 1

    @pl.when(j == 0)
    def _():
        xn_ref[...] = (_rms(x_ref[...]) * pre_ref[...]).astype(BF16)
        o_ref[...] = jnp.zeros_like(o_ref)

    def slab(valid):
        xn = xn_ref[...]
        g = jnp.dot(xn, wg_ref[...].astype(BF16), preferred_element_type=F32)
        u = jnp.dot(xn, wu_ref[...].astype(BF16), preferred_element_type=F32)
        h = _silu(g) * u
        wd = wd_ref[...]
        if valid:
            h = jnp.where(lax.broadcasted_iota(jnp.int32, h.shape, 1) < valid, h, 0.0)
            wd = jnp.where(lax.broadcasted_iota(jnp.int32, wd.shape, 0) < valid, wd, 0.0)
        o_ref[...] += jnp.dot(h.astype(BF16), wd.astype(BF16), preferred_element_type=F32)

    if tail:
        pl.when(j < last)(lambda: slab(0))
        pl.when(j == last)(lambda: slab(tail))
    else:
        slab(0)

    @pl.when(j == last)
    def _():
        o_ref[...] = x_ref[...] + 0.5 * (_rms(o_ref[...]) * post_ref[...])


def _ffn(x, pre_g, wg, wu, wd, post_g, li):
    T, D = x.shape
    F = wg.shape[2]
    tm = _tile(T, 1024)
    tf = 256 if F > 256 else F
    nf = pl.cdiv(F, tf)
    return pl.pallas_call(
        functools.partial(_ffn_kernel, tail=F % tf),
        out_shape=jax.ShapeDtypeStruct((T, D), F32),
        grid=(T // tm, nf),
        in_specs=[
            pl.BlockSpec((tm, D), lambda i, j: (i, 0)),
            pl.BlockSpec((1, D), lambda i, j: (0, 0)),
            pl.BlockSpec((None, D, tf), lambda i, j: (li, 0, j)),
            pl.BlockSpec((None, D, tf), lambda i, j: (li, 0, j)),
            pl.BlockSpec((None, tf, D), lambda i, j: (li, j, 0)),
            pl.BlockSpec((1, D), lambda i, j: (0, 0)),
        ],
        out_specs=pl.BlockSpec((tm, D), lambda i, j: (i, 0)),
        scratch_shapes=[pltpu.VMEM((tm, D), BF16)],
        compiler_params=_params("parallel", "arbitrary"),
        name="ffn",
    )(x, pre_g, wg, wu, wd, post_g)


def _inproj_kernel(x_ref, pre_ref, w_ref, wgate_ref, o_ref, gate_ref, xn_ref):
    @pl.when(pl.program_id(1) == 0)
    def _():
        xn = (_rms(x_ref[...]) * pre_ref[...]).astype(BF16)
        xn_ref[...] = xn
        gate_ref[...] = jnp.dot(xn, wgate_ref[...], preferred_element_type=F32)

    o_ref[...] = jnp.dot(xn_ref[...], w_ref[...], preferred_element_type=F32)


def _inproj(x, pre_g, w_main, w_gate):
    T, D = x.shape
    tm = _tile(T, 1024)
    return pl.pallas_call(
        _inproj_kernel,
        out_shape=(jax.ShapeDtypeStruct((N_PIECES, T, PIECE), F32),
                   jax.ShapeDtypeStruct((T, LANES), F32)),
        grid=(T // tm, N_PIECES),
        in_specs=[
            pl.BlockSpec((tm, D), lambda i, j: (i, 0)),
            pl.BlockSpec((1, D), lambda i, j: (0, 0)),
            pl.BlockSpec((D, PIECE), lambda i, j: (0, j)),
            pl.BlockSpec((D, LANES), lambda i, j: (0, 0)),
        ],
        out_specs=(pl.BlockSpec((None, tm, PIECE), lambda i, j: (j, i, 0)),
                   pl.BlockSpec((tm, LANES), lambda i, j: (i, 0))),
        scratch_shapes=[pltpu.VMEM((tm, D), BF16)],
        compiler_params=_params("parallel", "arbitrary"),
        name="inproj",
    )(x, pre_g, w_main, w_gate)


def _mlstm_kernel(qk_ref, v_ref, og_ref, gc_ref, gr_ref, bc_ref, br_ref, cw_ref, cb_ref, mlg_ref,
                  c0_ref, n0_ref, m0_ref, conv0_ref,
                  out_ref, cn_ref, nn_ref, mn_ref, convn_ref,
                  xp_ref, c_s, n_s, m_s, *, L):
    c = pl.program_id(1)

    @pl.when(c == 0)
    def _():
        c_s[...] = c0_ref[...]
        n_s[...] = n0_ref[...]
        m_s[...] = m0_ref[...]
        xp_ref[0:CONV_PAD, :] = jnp.zeros((CONV_PAD, ML_CONV_CH), F32)
        xp_ref[CONV_PAD - (CONV_W - 1):CONV_PAD, :] = conv0_ref[...]

    qk = qk_ref[...]
    xp_ref[CONV_PAD:CONV_PAD + L, :] = qk
    cw = cw_ref[...]
    conv = cb_ref[...] + qk * cw[CONV_W - 1:CONV_W, :]
    for t in range(CONV_W - 1):
        off = CONV_PAD - (CONV_W - 1) + t
        conv = conv + xp_ref[off:off + L, :] * cw[t:t + 1, :]
    tail = xp_ref[CONV_PAD + L - (CONV_W - 1):CONV_PAD + L, :]
    xp_ref[CONV_PAD - (CONV_W - 1):CONV_PAD, :] = tail
    act = _silu(conv)

    gcb = gc_ref[...] + bc_ref[...]
    grb = gr_ref[...] + br_ref[...]
    row = lax.broadcasted_iota(jnp.int32, (L, L), 0)
    col = lax.broadcasted_iota(jnp.int32, (L, L), 1)
    causal = col <= row
    b_c = jnp.dot(causal.astype(F32), _log_sigmoid(gcb), precision=lax.Precision.HIGHEST,
                  preferred_element_type=F32)
    b_r = jnp.dot(_log_sigmoid(grb), (row <= col).astype(F32), precision=lax.Precision.HIGHEST,
                  preferred_element_type=F32)

    v_all = v_ref[...]
    hs = []
    for h in range(ML_HEADS):
        q = act[:, h * ML_QK_DIM:(h + 1) * ML_QK_DIM]
        k = act[:, ML_HEADS * ML_QK_DIM + h * ML_QK_DIM:ML_HEADS * ML_QK_DIM + (h + 1) * ML_QK_DIM]
        k = k * (ML_QK_DIM ** -0.5)
        qb = q.astype(BF16)
        vb = v_all[:, h * ML_V_DIM:(h + 1) * ML_V_DIM].astype(BF16)
        ig_r = grb[h:h + 1, :]
        ig_c = gcb[:, h:h + 1]
        bh_r = b_r[ML_HEADS + h:ML_HEADS + h + 1, :]
        bh_c = b_c[:, ML_HEADS + h:ML_HEADS + h + 1]
        m_prev = m_s[:, h:h + 1]
        n_prev = n_s[h:h + 1, :]
        c_prev = c_s[h]

        d = jnp.where(causal, bh_c - bh_r + ig_r, NEG_INF)
        inter = bh_c + m_prev
        m_t = jnp.maximum(inter, jnp.max(d, axis=-1, keepdims=True))
        w_intra = jnp.exp(d - m_t)
        w_inter = jnp.exp(inter - m_t)
        s = lax.dot_general(qb, k.astype(BF16), NT, preferred_element_type=F32) * w_intra
        num = (jnp.dot(s.astype(BF16), vb, preferred_element_type=F32)
               + w_inter * lax.dot_general(qb, c_prev.astype(BF16), NT, preferred_element_type=F32))
        den = (jnp.sum(s, axis=-1, keepdims=True)
               + w_inter * jnp.sum(q * n_prev, axis=-1, keepdims=True))
        hs.append(num / jnp.maximum(jnp.abs(den), jnp.exp(-m_t)))

        m_new = m_t[L - 1:L, :]
        b_last = bh_c[L - 1:L, :]
        w_state = jnp.exp(b_last - bh_c + ig_c - m_new)
        decay = jnp.exp(b_last + m_prev - m_new)
        kw = k * w_state
        c_s[h] = decay * c_prev + lax.dot_general(vb, kw.astype(BF16), TN, preferred_element_type=F32)
        n_s[h:h + 1, :] = decay * n_prev + jnp.sum(kw, axis=0, keepdims=True)
        m_s[:, h:h + 1] = m_new

    gated = jax.nn.sigmoid(og_ref[...]) * jnp.concatenate(hs, axis=-1)
    out_ref[...] = (_rms(gated) * mlg_ref[...]).astype(BF16)

    @pl.when(c == pl.num_programs(1) - 1)
    def _():
        cn_ref[...] = c_s[...]
        nn_ref[...] = n_s[...]
        mn_ref[...] = m_s[...]
        convn_ref[...] = tail


def _mlstm(proj, gates, row0, n_streams, seq, L, bias_c, bias_r, conv_w, conv_b, ml_g, c0, n0, m0, conv0):
    nc = seq // L
    base = row0 // L
    rows = n_streams * seq
    g_rows = gates[row0:row0 + rows, :N_GATES].reshape(n_streams * nc, L, N_GATES).transpose(0, 2, 1)

    def tok(piece):
        return pl.BlockSpec((None, L, PIECE), lambda s, c: (piece, base + s * nc + c, 0))

    def const(shape):
        return pl.BlockSpec(shape, lambda s, c: (0,) * len(shape))

    state_specs = (
        pl.BlockSpec((None, ML_HEADS, ML_V_DIM, ML_QK_DIM), lambda s, c: (s, 0, 0, 0)),
        pl.BlockSpec((None, ML_HEADS, ML_QK_DIM), lambda s, c: (s, 0, 0)),
        pl.BlockSpec((None, 1, ML_HEADS), lambda s, c: (s, 0, 0)),
        pl.BlockSpec((None, CONV_W - 1, ML_CONV_CH), lambda s, c: (s, 0, 0)),
    )
    return pl.pallas_call(
        functools.partial(_mlstm_kernel, L=L),
        out_shape=(
            jax.ShapeDtypeStruct((rows, ML_WIDTH), BF16),
            jax.ShapeDtypeStruct((n_streams, ML_HEADS, ML_V_DIM, ML_QK_DIM), F32),
            jax.ShapeDtypeStruct((n_streams, ML_HEADS, ML_QK_DIM), F32),
            jax.ShapeDtypeStruct((n_streams, 1, ML_HEADS), F32),
            jax.ShapeDtypeStruct((n_streams, CONV_W - 1, ML_CONV_CH), F32),
        ),
        grid=(n_streams, nc),
        in_specs=[
            tok(0), tok(1), tok(2),
            pl.BlockSpec((L, LANES), lambda s, c: (base + s * nc + c, 0)),
            pl.BlockSpec((None, N_GATES, L), lambda s, c: (s * nc + c, 0, 0)),
            const((1, LANES)), const((N_GATES, 1)), const((CONV_W, ML_CONV_CH)), const((1, ML_CONV_CH)),
            const((1, ML_WIDTH)),
            *state_specs,
        ],
        out_specs=(pl.BlockSpec((L, ML_WIDTH), lambda s, c: (s * nc + c, 0)), *state_specs),
        scratch_shapes=[
            pltpu.VMEM((CONV_PAD + L, ML_CONV_CH), F32),
            pltpu.VMEM((ML_HEADS, ML_V_DIM, ML_QK_DIM), F32),
            pltpu.VMEM((ML_HEADS, ML_QK_DIM), F32),
            pltpu.VMEM((1, ML_HEADS), F32),
        ],
        compiler_params=_params("arbitrary", "arbitrary"),
        name="mlstm",
    )(proj, proj, proj, gates, g_rows, bias_c, bias_r, conv_w, conv_b, ml_g, c0, n0, m0, conv0)


def _lambda(lq1_ref, lk1_ref, lq2_ref, lk2_ref, lam_init):
    return (jnp.exp(jnp.sum(lq1_ref[...] * lk1_ref[...], axis=-1, keepdims=True))
            - jnp.exp(jnp.sum(lq2_ref[...] * lk2_ref[...], axis=-1, keepdims=True)) + lam_init)


def _lane_tiles(x):
    return [x[:, c * LANES:(c + 1) * LANES] for c in range(x.shape[1] // LANES)]


def _fold(op, x):
    return functools.reduce(op, _lane_tiles(x))


def _exp2_rows(s, m_b):
    return jnp.concatenate([jnp.exp2(t - m_b) for t in _lane_tiles(s)], axis=-1)


def _replicate(col):
    return jnp.broadcast_to(col, (col.shape[0], LANES))


def _da_finish(out1, out2, lam, g, lam_init):
    att = out1 - lam * out2
    return ((_rms(att) * g) * (1.0 - lam_init)).astype(BF16)


LOG2E = math.log2(math.e)
Q_SCALE = DA_HEAD_DIM ** -0.5 * LOG2E


def _attn_prompt_kernel(q_ref, k_ref, v_ref, lq1_ref, lk1_ref, lq2_ref, lk2_ref, g_ref, o_ref,
                        kb_ref, vb_ref, s_ref, m_ref, l_ref, acc_ref, *, lam_init, tq):
    i = pl.program_id(2)
    hd = DA_HEAD_DIM

    @pl.when(i == 0)
    def _():
        kb_ref[...] = k_ref[...].astype(BF16)
        vb_ref[...] = v_ref[...].astype(BF16)

    qs = (q_ref[...] * Q_SCALE).astype(BF16)

    def rows(ref, j):
        return ref[pl.ds(pl.multiple_of(j * tq, tq), tq), :]

    def scores(kj, t):
        return lax.dot_general(qs[:, t * hd:(t + 1) * hd], kj[:, t * hd:(t + 1) * hd], NT,
                               preferred_element_type=F32)

    def note_scores(j, t, s):
        s_ref[t, j] = s
        m_ref[t] = jnp.maximum(m_ref[t], _fold(jnp.maximum, s))

    m_ref[...] = jnp.full_like(m_ref, NEG_INF)

    def sweep_max(j, carry):
        kj = rows(kb_ref, j)
        for t in range(2):
            note_scores(j, t, scores(kj, t))
        return carry

    lax.fori_loop(0, i, sweep_max, 0)
    diag = (lax.broadcasted_iota(jnp.int32, (tq, tq), 0) // CHUNK
            >= lax.broadcasted_iota(jnp.int32, (tq, tq), 1) // CHUNK)
    kd = rows(kb_ref, i)
    for t in range(2):
        note_scores(i, t, jnp.where(diag, scores(kd, t), NEG_INF))
        m_ref[t] = _replicate(jnp.max(m_ref[t], axis=-1, keepdims=True))

    l_ref[...] = jnp.zeros_like(l_ref)
    acc_ref[...] = jnp.zeros_like(acc_ref)

    def sweep_pv(j, carry):
        vj = rows(vb_ref, j)
        for t in range(2):
            p = _exp2_rows(s_ref[t, j], m_ref[t])
            acc_ref[t] += jnp.dot(p.astype(BF16), vj, preferred_element_type=F32)
            l_ref[t] += _fold(jnp.add, p)
        return carry

    lax.fori_loop(0, i + 1, sweep_pv, 0)
    outs = [acc_ref[t] / jnp.sum(l_ref[t], axis=-1, keepdims=True) for t in range(2)]
    lam = _lambda(lq1_ref, lk1_ref, lq2_ref, lk2_ref, lam_init)
    o_ref[...] = _da_finish(outs[0], outs[1], lam, g_ref[...], lam_init)


def _attn_prompt(proj, n_streams, seq, lq1, lk1, lq2, lk2, da_g, lam_init):
    tq = _tile(seq, 512)
    nq = seq // tq
    dq = 2 * DA_HEAD_DIM

    def const(shape):
        return pl.BlockSpec(shape, lambda b, h, i: (0,) * len(shape))

    return pl.pallas_call(
        functools.partial(_attn_prompt_kernel, lam_init=lam_init, tq=tq),
        out_shape=jax.ShapeDtypeStruct((n_streams * seq, DA_WIDTH), BF16),
        grid=(n_streams, DA_HEADS, nq),
        in_specs=[
            pl.BlockSpec((None, tq, dq), lambda b, h, i: (3, b * nq + i, h)),
            pl.BlockSpec((None, seq, dq), lambda b, h, i: (4, b, h)),
            pl.BlockSpec((None, seq, DA_V_DIM), lambda b, h, i: (5, b, h)),
            const((1, DA_HEAD_DIM)), const((1, DA_HEAD_DIM)), const((1, DA_HEAD_DIM)), const((1, DA_HEAD_DIM)),
            const((1, DA_V_DIM)),
        ],
        out_specs=pl.BlockSpec((tq, DA_V_DIM), lambda b, h, i: (b * nq + i, h)),
        scratch_shapes=[
            pltpu.VMEM((seq, dq), BF16),
            pltpu.VMEM((seq, DA_V_DIM), BF16),
            pltpu.VMEM((2, nq, tq, tq), F32),
            pltpu.VMEM((2, tq, LANES), F32),
            pltpu.VMEM((2, tq, LANES), F32),
            pltpu.VMEM((2, tq, DA_V_DIM), F32),
        ],
        compiler_params=_params("parallel", "parallel", "arbitrary"),
        name="attn_prompt",
    )(proj, proj, proj, lq1, lk1, lq2, lk2, da_g)


def _attn_sample_kernel(q_ref, kn_ref, vn_ref, ck_ref, cv_ref, lq1_ref, lk1_ref, lq2_ref, lk2_ref, g_ref,
                        o_ref, kbuf, vbuf, sem, *, lam_init, li):
    hd = DA_HEAD_DIM
    s = pl.program_id(0)
    slot = s % 2

    def cache_copies(stream, slot_, h):
        return (pltpu.make_async_copy(ck_ref.at[li, stream, :, h, :], kbuf.at[slot_, h], sem.at[0, slot_, h]),
                pltpu.make_async_copy(cv_ref.at[li, stream, :, h, :], vbuf.at[slot_, h], sem.at[1, slot_, h]))

    def start_stream(stream, slot_):
        for h in range(DA_HEADS):
            for cp in cache_copies(stream, slot_, h):
                cp.start()

    @pl.when(s == 0)
    def _():
        start_stream(0, 0)

    @pl.when(s + 1 < pl.num_programs(0))
    def _():
        start_stream(s + 1, 1 - slot)

    lam = _lambda(lq1_ref, lk1_ref, lq2_ref, lk2_ref, lam_init)
    for h in range(DA_HEADS):
        cols = slice(h * DA_V_DIM, (h + 1) * DA_V_DIM)
        qs = (q_ref[:, cols] * Q_SCALE).astype(BF16)
        for cp in cache_copies(s, slot, h):
            cp.wait()
        kc, vc = kbuf[slot, h].astype(BF16), vbuf[slot, h].astype(BF16)
        kn, vn = kn_ref[:, cols].astype(BF16), vn_ref[:, cols].astype(BF16)
        outs = []
        for t in range(2):
            sl = slice(t * hd, (t + 1) * hd)
            sc = lax.dot_general(qs[:, sl], kc[:, sl], NT, preferred_element_type=F32)
            sn = lax.dot_general(qs[:, sl], kn[:, sl], NT, preferred_element_type=F32)
            m = jnp.maximum(jnp.max(_fold(jnp.maximum, sc), axis=-1, keepdims=True),
                            jnp.max(sn, axis=-1, keepdims=True))
            pc = _exp2_rows(sc, _replicate(m))
            pn = jnp.exp2(sn - m)
            l = jnp.sum(_fold(jnp.add, pc), axis=-1, keepdims=True) + jnp.sum(pn, axis=-1, keepdims=True)
            acc = (jnp.dot(pc.astype(BF16), vc, preferred_element_type=F32)
                   + jnp.dot(pn.astype(BF16), vn, preferred_element_type=F32))
            outs.append(acc / l)
        o_ref[:, cols] = _da_finish(outs[0], outs[1], lam, g_ref[...], lam_init)


def _attn_sample(proj, row0, n_streams, seq, cache_k, cache_v, li, lq1, lk1, lq2, lk2, da_g, lam_init):
    past = cache_k.shape[2]
    base = row0 // seq
    dq = 2 * DA_HEAD_DIM

    def const(shape):
        return pl.BlockSpec(shape, lambda s: (0,) * len(shape))

    def new(piece):
        return pl.BlockSpec((None, seq, PIECE), lambda s: (piece, base + s, 0))

    cache_spec = pl.BlockSpec(memory_space=pl.ANY)
    return pl.pallas_call(
        functools.partial(_attn_sample_kernel, lam_init=lam_init, li=li),
        out_shape=jax.ShapeDtypeStruct((n_streams * seq, DA_WIDTH), BF16),
        grid=(n_streams,),
        in_specs=[
            new(3), new(4), new(5), cache_spec, cache_spec,
            const((1, DA_HEAD_DIM)), const((1, DA_HEAD_DIM)), const((1, DA_HEAD_DIM)), const((1, DA_HEAD_DIM)),
            const((1, DA_V_DIM)),
        ],
        out_specs=pl.BlockSpec((seq, DA_WIDTH), lambda s: (s, 0)),
        scratch_shapes=[
            pltpu.VMEM((2, DA_HEADS, past, dq), F32),
            pltpu.VMEM((2, DA_HEADS, past, DA_V_DIM), F32),
            pltpu.SemaphoreType.DMA((2, 2, DA_HEADS)),
        ],
        compiler_params=_params("arbitrary"),
        name="attn_sample",
    )(proj, proj, proj, cache_k, cache_v, lq1, lk1, lq2, lk2, da_g)


def _outproj_kernel(x_ref, ml_ref, da_ref, w1_ref, w2_ref, post_ref, o_ref):
    y = (jnp.dot(ml_ref[...], w1_ref[...], preferred_element_type=F32)
         + jnp.dot(da_ref[...], w2_ref[...], preferred_element_type=F32))
    o_ref[...] = x_ref[...] + _rms(y) * post_ref[...]


def _outproj(x, ml_out, da_out, w_out, post_g):
    T, D = x.shape
    tm = _tile(T, 512)
    return pl.pallas_call(
        _outproj_kernel,
        out_shape=jax.ShapeDtypeStruct((T, D), F32),
        grid=(T // tm,),
        in_specs=[
            pl.BlockSpec((tm, D), lambda i: (i, 0)),
            pl.BlockSpec((tm, ML_WIDTH), lambda i: (i, 0)),
            pl.BlockSpec((tm, DA_WIDTH), lambda i: (i, 0)),
            pl.BlockSpec((ML_WIDTH, D), lambda i: (0, 0)),
            pl.BlockSpec((DA_WIDTH, D), lambda i: (1, 0)),
            pl.BlockSpec((1, D), lambda i: (0, 0)),
        ],
        out_specs=pl.BlockSpec((tm, D), lambda i: (i, 0)),
        compiler_params=_params("parallel"),
        name="outproj",
    )(x, ml_out, da_out, w_out, w_out, post_g)


def kernel(x_prompt, x_sample, cache_k, cache_v, state_C, state_n, state_m, state_conv,
           ffn1_pre_g, ffn1_wg, ffn1_wu, ffn1_wd, ffn1_post_g,
           mix_pre_g, w_in, b_i, b_f, conv_w, conv_b, ml_norm_g,
           lam_q1, lam_k1, lam_q2, lam_k2, da_norm_g, w_out, mix_post_g,
           ffn2_pre_g, ffn2_wg, ffn2_wu, ffn2_wd, ffn2_post_g):
    Bp, S, D = x_prompt.shape
    Bs, Ls, _ = x_sample.shape
    depth = w_in.shape[0]
    past = cache_k.shape[2]
    Tp, Ts = Bp * S, Bs * Ls
    assert Ls == CHUNK and past % LANES == 0 and S % CHUNK == 0 and Tp % Ls == 0
    Lp = 256 if S % 256 == 0 else CHUNK

    xp, xs = x_prompt.reshape(Tp, D), x_sample.reshape(Ts, D)
    zero_state = (jnp.zeros((Bp, ML_HEADS, ML_V_DIM, ML_QK_DIM), F32), jnp.zeros((Bp, ML_HEADS, ML_QK_DIM), F32),
                  jnp.zeros((Bp, 1, ML_HEADS), F32), jnp.zeros((Bp, CONV_W - 1, ML_CONV_CH), F32))

    def row(v):
        return v.reshape(1, -1)

    p_st, s_st = [], []
    for li in range(depth):
        lam_init = 0.8 - 0.6 * math.exp(-0.3 * li)
        ffn1 = (row(ffn1_pre_g[li]), ffn1_wg, ffn1_wu, ffn1_wd, row(ffn1_post_g[li]), li)
        ffn2 = (row(ffn2_pre_g[li]), ffn2_wg, ffn2_wu, ffn2_wd, row(ffn2_post_g[li]), li)
        w = w_in[li]
        w_main = jnp.concatenate([w[:, :GATE_OFF], w[:, GATE_OFF + N_GATES:]], axis=1).astype(BF16)
        w_gate = jnp.pad(w[:, GATE_OFF:GATE_OFF + N_GATES].astype(BF16), ((0, 0), (0, LANES - N_GATES)))
        w_o = w_out[li].astype(BF16)
        bias = jnp.concatenate([b_i[li], b_f[li]])
        bias_c = jnp.pad(bias, (0, LANES - N_GATES)).reshape(1, LANES)
        bias_r = bias.reshape(N_GATES, 1)
        ml_args = (bias_c, bias_r, conv_w[li], row(conv_b[li]), row(ml_norm_g[li]))
        lam_args = (row(lam_q1[li]), row(lam_k1[li]), row(lam_q2[li]), row(lam_k2[li]), row(da_norm_g[li]), lam_init)

        xp = _ffn(xp, *ffn1)
        xs = _ffn(xs, *ffn1)
        proj_p, gates_p = _inproj(xp, row(mix_pre_g[li]), w_main, w_gate)
        proj_s, gates_s = _inproj(xs, row(mix_pre_g[li]), w_main, w_gate)
        ml_p, p_c, p_n, p_m, p_conv = _mlstm(proj_p, gates_p, 0, Bp, S, Lp, *ml_args, *zero_state)
        ml_s, s_c, s_n, s_m, s_conv = _mlstm(proj_s, gates_s, 0, Bs, Ls, Ls, *ml_args, state_C[li], state_n[li],
                                             state_m[li].reshape(Bs, 1, ML_HEADS), state_conv[li])
        da_p = _attn_prompt(proj_p, Bp, S, *lam_args)
        da_s = _attn_sample(proj_s, 0, Bs, Ls, cache_k, cache_v, li, *lam_args)
        xp = _ffn(_outproj(xp, ml_p, da_p, w_o, row(mix_post_g[li])), *ffn2)
        xs = _ffn(_outproj(xs, ml_s, da_s, w_o, row(mix_post_g[li])), *ffn2)

        p_st.append((proj_p[4].reshape(Bp, S, DA_HEADS, 2 * DA_HEAD_DIM), proj_p[5].reshape(Bp, S, DA_HEADS, DA_V_DIM),
                     p_c, p_n, p_m.reshape(Bp, ML_HEADS), p_conv))
        s_st.append((proj_s[4].reshape(Bs, Ls, DA_HEADS, 2 * DA_HEAD_DIM), proj_s[5].reshape(Bs, Ls, DA_HEADS, DA_V_DIM),
                     s_c, s_n, s_m.reshape(Bs, ML_HEADS), s_conv))

    p_out = tuple(jnp.stack(e) for e in zip(*p_st))
    s_out = tuple(jnp.stack(e) for e in zip(*s_st))
    return (xp.reshape(Bp, S, D), xs.reshape(Bs, Ls, D), *p_out, *s_out)
```

```python
import functools
import math

import jax
import jax.numpy as jnp
from jax import lax
from jax.experimental import pallas as pl
from jax.experimental.pallas import tpu as pltpu

F32 = jnp.float32
BF16 = jnp.bfloat16

CHUNK = 64
ML_HEADS = 4
ML_QK_DIM = 128
ML_V_DIM = 256
DA_HEADS = 4
DA_HEAD_DIM = 128
DA_V_DIM = 2 * DA_HEAD_DIM
CONV_W = 4
EPS = 1e-6
ML_WIDTH = ML_HEADS * ML_V_DIM
DA_WIDTH = DA_HEADS * DA_V_DIM
ML_CONV_CH = 2 * ML_HEADS * ML_QK_DIM
DA_QK_WIDTH = DA_HEADS * 2 * DA_HEAD_DIM
PIECE = 1024
N_PIECES = 6
N_GATES = 2 * ML_HEADS
GATE_OFF = 3 * PIECE
assert ML_CONV_CH == ML_WIDTH == DA_QK_WIDTH == DA_WIDTH == PIECE

LANES = 128
CONV_PAD = 8
VMEM_LIMIT = 56 * 1024 * 1024

NEG_INF = float("-inf")
NT = (((1,), (1,)), ((), ()))
TN = (((0,), (0,)), ((), ()))


def _params(*sem):
    return pltpu.CompilerParams(dimension_semantics=sem, vmem_limit_bytes=VMEM_LIMIT)


def _rms(xf):
    return xf * lax.rsqrt(jnp.mean(xf * xf, axis=-1, keepdims=True) + EPS)


def _silu(x):
    return x * jax.nn.sigmoid(x)


def _log_sigmoid(x):
    return jnp.minimum(x, 0.0) - jnp.log1p(jnp.exp(-jnp.abs(x)))


def _tile(n, pref):
    return pref if n % pref == 0 else n


def _ffn_kernel(x_ref, pre_ref, wg_ref, wu_ref, wd_ref, post_ref, o_ref, xn_ref, *, tail):
    j = pl.program_id(1)
    last = pl.num_programs(1) - 1

    @pl.when(j == 0)
    def _():
        xn_ref[...] = (_rms(x_ref[...]) * pre_ref[...]).astype(BF16)
        o_ref[...] = jnp.zeros_like(o_ref)

    def slab(valid):
        xn = xn_ref[...]
        g = jnp.dot(xn, wg_ref[...].astype(BF16), preferred_element_type=F32)
        u = jnp.dot(xn, wu_ref[...].astype(BF16), preferred_element_type=F32)
        h = _silu(g) * u
        wd = wd_ref[...]
        if valid:
            h = jnp.where(lax.broadcasted_iota(jnp.int32, h.shape, 1) < valid, h, 0.0)
            wd = jnp.where(lax.broadcasted_iota(jnp.int32, wd.shape, 0) < valid, wd, 0.0)
        o_ref[...] += jnp.dot(h.astype(BF16), wd.astype(BF16), preferred_element_type=F32)

    if tail:
        pl.when(j < last)(lambda: slab(0))
        pl.when(j == last)(lambda: slab(tail))
    else:
        slab(0)

    @pl.when(j == last)
    def _():
        o_ref[...] = x_ref[...] + 0.5 * (_rms(o_ref[...]) * post_ref[...])


def _ffn(x, pre_g, wg, wu, wd, post_g, li):
    T, D = x.shape
    F = wg.shape[2]
    tm = _tile(T, 1024)
    tf = 256 if F > 256 else F
    nf = pl.cdiv(F, tf)
    return pl.pallas_call(
        functools.partial(_ffn_kernel, tail=F % tf),
        out_shape=jax.ShapeDtypeStruct((T, D), F32),
        grid=(T // tm, nf),
        in_specs=[
            pl.BlockSpec((tm, D), lambda i, j: (i, 0)),
            pl.BlockSpec((1, D), lambda i, j: (0, 0)),
            pl.BlockSpec((None, D, tf), lambda i, j: (li, 0, j)),
            pl.BlockSpec((None, D, tf), lambda i, j: (li, 0, j)),
            pl.BlockSpec((None, tf, D), lambda i, j: (li, j, 0)),
            pl.BlockSpec((1, D), lambda i, j: (0, 0)),
        ],
        out_specs=pl.BlockSpec((tm, D), lambda i, j: (i, 0)),
        scratch_shapes=[pltpu.VMEM((tm, D), BF16)],
        compiler_params=_params("parallel", "arbitrary"),
        name="ffn",
    )(x, pre_g, wg, wu, wd, post_g)


K_PIECE, V_PIECE = 4, 5


def _inproj_kernel(*refs, li, aliased):
    x_ref, pre_ref, w_ref, wgate_ref = refs[:4]
    o_ref, gate_ref, k5_ref, v5_ref, xn_ref, kv_ref, sem = refs[6 if aliased else 4:]
    i, j = pl.program_id(0), pl.program_id(1)

    def head_copies(slot, dst_ref, tile):
        return [pltpu.make_async_copy(kv_ref.at[slot, :, pl.ds(h * DA_V_DIM, DA_V_DIM)],
                                      dst_ref.at[li, tile, :, h, :], sem.at[slot, h]) for h in range(DA_HEADS)]

    @pl.when(j == 0)
    def _():
        xn = (_rms(x_ref[...]) * pre_ref[...]).astype(BF16)
        xn_ref[...] = xn
        gate_ref[...] = jnp.dot(xn, wgate_ref[...], preferred_element_type=F32)

    @pl.when((j == 0) & (i > 0))
    def _():
        for cp in head_copies(1, v5_ref, i - 1):
            cp.wait()

    res = jnp.dot(xn_ref[...], w_ref[...], preferred_element_type=F32)
    o_ref[...] = res

    @pl.when(j == K_PIECE)
    def _():
        kv_ref[0] = res
        for cp in head_copies(0, k5_ref, i):
            cp.start()

    @pl.when(j == V_PIECE)
    def _():
        for cp in head_copies(0, k5_ref, i):
            cp.wait()
        kv_ref[1] = res
        for cp in head_copies(1, v5_ref, i):
            cp.start()

    @pl.when((j == V_PIECE) & (i == pl.num_programs(0) - 1))
    def _():
        for cp in head_copies(1, v5_ref, i):
            cp.wait()


def _inproj(x, pre_g, w_main, w_gate, li, depth, kv_prev):
    T, D = x.shape
    tm = _tile(T, 1024)
    nt = T // tm
    kv_shape = jax.ShapeDtypeStruct((depth, nt, tm, DA_HEADS, DA_V_DIM), F32)
    any_spec = pl.BlockSpec(memory_space=pl.ANY)
    aliased = kv_prev is not None
    return pl.pallas_call(
        functools.partial(_inproj_kernel, li=li, aliased=aliased),
        out_shape=(jax.ShapeDtypeStruct((N_PIECES, T, PIECE), F32), jax.ShapeDtypeStruct((T, LANES), F32),
                   kv_shape, kv_shape),
        grid=(nt, N_PIECES),
        in_specs=[
            pl.BlockSpec((tm, D), lambda i, j: (i, 0)),
            pl.BlockSpec((1, D), lambda i, j: (0, 0)),
            pl.BlockSpec((None, D, PIECE), lambda i, j: (li, 0, j)),
            pl.BlockSpec((None, D, LANES), lambda i, j: (li, 0, 0)),
        ] + ([any_spec, any_spec] if aliased else []),
        out_specs=(pl.BlockSpec((None, tm, PIECE), lambda i, j: (j, i, 0)),
                   pl.BlockSpec((tm, LANES), lambda i, j: (i, 0)), any_spec, any_spec),
        scratch_shapes=[pltpu.VMEM((tm, D), BF16), pltpu.VMEM((2, tm, PIECE), F32),
                        pltpu.SemaphoreType.DMA((2, DA_HEADS))],
        input_output_aliases={4: 2, 5: 3} if aliased else {},
        compiler_params=_params("arbitrary", "arbitrary"),
        name="inproj",
    )(x, pre_g, w_main, w_gate, *(kv_prev if aliased else ()))


N_STATES = 4


def _mlstm_kernel(*refs, L, aliased):
    (qk_ref, v_ref, og_ref, gc_ref, gr_ref, bc_ref, br_ref, cw_ref, cb_ref, mlg_ref,
     c0_ref, n0_ref, m0_ref, conv0_ref) = refs[:14]
    (out_ref, cn_ref, nn_ref, mn_ref, convn_ref, xp_ref, c_s, n_s, m_s) = refs[14 + (N_STATES if aliased else 0):]
    c = pl.program_id(1)

    @pl.when(c == 0)
    def _():
        c_s[...] = c0_ref[...]
        n_s[...] = n0_ref[...]
        m_s[...] = m0_ref[...]
        xp_ref[0:CONV_PAD, :] = jnp.zeros((CONV_PAD, ML_CONV_CH), F32)
        xp_ref[CONV_PAD - (CONV_W - 1):CONV_PAD, :] = conv0_ref[...]

    qk = qk_ref[...]
    xp_ref[CONV_PAD:CONV_PAD + L, :] = qk
    cw = cw_ref[...]
    conv = cb_ref[...] + qk * cw[CONV_W - 1:CONV_W, :]
    for t in range(CONV_W - 1):
        off = CONV_PAD - (CONV_W - 1) + t
        conv = conv + xp_ref[off:off + L, :] * cw[t:t + 1, :]
    tail = xp_ref[CONV_PAD + L - (CONV_W - 1):CONV_PAD + L, :]
    xp_ref[CONV_PAD - (CONV_W - 1):CONV_PAD, :] = tail
    act = _silu(conv)

    gcb = gc_ref[...] + bc_ref[...]
    grb = gr_ref[...] + br_ref[...]
    row = lax.broadcasted_iota(jnp.int32, (L, L), 0)
    col = lax.broadcasted_iota(jnp.int32, (L, L), 1)
    causal = col <= row
    b_c = jnp.dot(causal.astype(F32), _log_sigmoid(gcb), precision=lax.Precision.HIGHEST,
                  preferred_element_type=F32)
    b_r = jnp.dot(_log_sigmoid(grb), (row <= col).astype(F32), precision=lax.Precision.HIGHEST,
                  preferred_element_type=F32)

    v_all = v_ref[...]
    hs = []
    for h in range(ML_HEADS):
        q = act[:, h * ML_QK_DIM:(h + 1) * ML_QK_DIM]
        k = act[:, ML_HEADS * ML_QK_DIM + h * ML_QK_DIM:ML_HEADS * ML_QK_DIM + (h + 1) * ML_QK_DIM]
        k = k * (ML_QK_DIM ** -0.5)
        qb = q.astype(BF16)
        vb = v_all[:, h * ML_V_DIM:(h + 1) * ML_V_DIM].astype(BF16)
        ig_r = grb[h:h + 1, :]
        ig_c = gcb[:, h:h + 1]
        bh_r = b_r[ML_HEADS + h:ML_HEADS + h + 1, :]
        bh_c = b_c[:, ML_HEADS + h:ML_HEADS + h + 1]
        m_prev = m_s[:, h:h + 1]
        n_prev = n_s[h:h + 1, :]
        c_prev = c_s[h]

        d = jnp.where(causal, bh_c - bh_r + ig_r, NEG_INF)
        inter = bh_c + m_prev
        m_t = jnp.maximum(inter, jnp.max(d, axis=-1, keepdims=True))
        w_intra = jnp.exp(d - m_t)
        w_inter = jnp.exp(inter - m_t)
        s = lax.dot_general(qb, k.astype(BF16), NT, preferred_element_type=F32) * w_intra
        num = (jnp.dot(s.astype(BF16), vb, preferred_element_type=F32)
               + w_inter * lax.dot_general(qb, c_prev.astype(BF16), NT, preferred_element_type=F32))
        den = (jnp.sum(s, axis=-1, keepdims=True)
               + w_inter * jnp.sum(q * n_prev, axis=-1, keepdims=True))
        hs.append(num / jnp.maximum(jnp.abs(den), jnp.exp(-m_t)))

        m_new = m_t[L - 1:L, :]
        b_last = bh_c[L - 1:L, :]
        w_state = jnp.exp(b_last - bh_c + ig_c - m_new)
        decay = jnp.exp(b_last + m_prev - m_new)
        kw = k * w_state
        c_s[h] = decay * c_prev + lax.dot_general(vb, kw.astype(BF16), TN, preferred_element_type=F32)
        n_s[h:h + 1, :] = decay * n_prev + jnp.sum(kw, axis=0, keepdims=True)
        m_s[:, h:h + 1] = m_new

    gated = jax.nn.sigmoid(og_ref[...]) * jnp.concatenate(hs, axis=-1)
    out_ref[...] = (_rms(gated) * mlg_ref[...]).astype(BF16)

    @pl.when(c == pl.num_programs(1) - 1)
    def _():
        cn_ref[...] = c_s[...]
        nn_ref[...] = n_s[...]
        mn_ref[...] = m_s[...]
        convn_ref[...] = tail


def _mlstm(proj, gates, n_streams, seq, L, bias_c, bias_r, conv_w, conv_b, ml_g, state0, li0, li, depth, prev):
    nc = seq // L
    rows = n_streams * seq
    g_rows = gates[:, :N_GATES].reshape(n_streams * nc, L, N_GATES).transpose(0, 2, 1)
    aliased = prev is not None

    def tok(piece):
        return pl.BlockSpec((None, L, PIECE), lambda s, c: (piece, s * nc + c, 0))

    def const(shape):
        return pl.BlockSpec(shape, lambda s, c: (0,) * len(shape))

    state_dims = ((ML_HEADS, ML_V_DIM, ML_QK_DIM), (ML_HEADS, ML_QK_DIM), (1, ML_HEADS), (CONV_W - 1, ML_CONV_CH))

    def state_specs(layer):
        return [pl.BlockSpec((None, None) + d, lambda s, c, n=len(d): (layer, s) + (0,) * n) for d in state_dims]

    any_spec = pl.BlockSpec(memory_space=pl.ANY)
    n_in = 14
    return pl.pallas_call(
        functools.partial(_mlstm_kernel, L=L, aliased=aliased),
        out_shape=(jax.ShapeDtypeStruct((rows, ML_WIDTH), BF16),
                   *(jax.ShapeDtypeStruct((depth, n_streams) + d, F32) for d in state_dims)),
        grid=(n_streams, nc),
        in_specs=[
            tok(0), tok(1), tok(2),
            pl.BlockSpec((L, LANES), lambda s, c: (s * nc + c, 0)),
            pl.BlockSpec((None, N_GATES, L), lambda s, c: (s * nc + c, 0, 0)),
            const((1, LANES)), const((N_GATES, 1)), const((CONV_W, ML_CONV_CH)), const((1, ML_CONV_CH)),
            const((1, ML_WIDTH)),
            *state_specs(li0),
        ] + ([any_spec] * N_STATES if aliased else []),
        out_specs=(pl.BlockSpec((L, ML_WIDTH), lambda s, c: (s * nc + c, 0)), *state_specs(li)),
        scratch_shapes=[
            pltpu.VMEM((CONV_PAD + L, ML_CONV_CH), F32),
            pltpu.VMEM((ML_HEADS, ML_V_DIM, ML_QK_DIM), F32),
            pltpu.VMEM((ML_HEADS, ML_QK_DIM), F32),
            pltpu.VMEM((1, ML_HEADS), F32),
        ],
        input_output_aliases={n_in + k: 1 + k for k in range(N_STATES)} if aliased else {},
        compiler_params=_params("arbitrary", "arbitrary"),
        name="mlstm",
    )(proj, proj, proj, gates, g_rows, bias_c, bias_r, conv_w, conv_b, ml_g, *state0, *(prev if aliased else ()))


def _lambda(lq1_ref, lk1_ref, lq2_ref, lk2_ref, lam_init):
    return (jnp.exp(jnp.sum(lq1_ref[...] * lk1_ref[...], axis=-1, keepdims=True))
            - jnp.exp(jnp.sum(lq2_ref[...] * lk2_ref[...], axis=-1, keepdims=True)) + lam_init)


def _lane_tiles(x):
    return [x[:, c * LANES:(c + 1) * LANES] for c in range(x.shape[1] // LANES)]


def _fold(op, x):
    return functools.reduce(op, _lane_tiles(x))


def _exp2_rows(s, m_b):
    return jnp.concatenate([jnp.exp2(t - m_b) for t in _lane_tiles(s)], axis=-1)


def _replicate(col):
    return jnp.broadcast_to(col, (col.shape[0], LANES))


def _da_finish(out1, out2, lam, g, lam_init):
    att = out1 - lam * out2
    return ((_rms(att) * g) * (1.0 - lam_init)).astype(BF16)


LOG2E = math.log2(math.e)
Q_SCALE = DA_HEAD_DIM ** -0.5 * LOG2E


def _attn_prompt_kernel(q_ref, k_ref, v_ref, lq1_ref, lk1_ref, lq2_ref, lk2_ref, g_ref, o_ref,
                        kb_ref, vb_ref, s_ref, m_ref, l_ref, acc_ref, *, lam_init, tq):
    i = pl.program_id(2)
    hd = DA_HEAD_DIM

    @pl.when(i == 0)
    def _():
        kb_ref[...] = k_ref[...].astype(BF16)
        vb_ref[...] = v_ref[...].astype(BF16)

    qs = (q_ref[...] * Q_SCALE).astype(BF16)

    def rows(ref, j, nb):
        return ref[pl.ds(pl.multiple_of(j * tq, tq), nb * tq), :]

    def scores(kj, t):
        return lax.dot_general(qs[:, t * hd:(t + 1) * hd], kj[:, t * hd:(t + 1) * hd], NT,
                               preferred_element_type=F32)

    def note_scores(j, nb, t, s):
        for b in range(nb):
            s_ref[t, j + b] = s[:, b * tq:(b + 1) * tq]
        m_ref[t] = jnp.maximum(m_ref[t], _fold(jnp.maximum, s))

    def for_blocks(n, fn):
        def pair(k, carry):
            fn(2 * k, 2)
            return carry

        lax.fori_loop(0, n // 2, pair, 0)
        pl.when(n % 2 == 1)(lambda: fn(n - 1, 1))

    m_ref[...] = jnp.full_like(m_ref, NEG_INF)

    def sweep_max(j, nb):
        kj = rows(kb_ref, j, nb)
        for t in range(2):
            note_scores(j, nb, t, scores(kj, t))

    for_blocks(i, sweep_max)
    diag = (lax.broadcasted_iota(jnp.int32, (tq, tq), 0) // CHUNK
            >= lax.broadcasted_iota(jnp.int32, (tq, tq), 1) // CHUNK)
    kd = rows(kb_ref, i, 1)
    for t in range(2):
        note_scores(i, 1, t, jnp.where(diag, scores(kd, t), NEG_INF))
        m_ref[t] = _replicate(jnp.max(m_ref[t], axis=-1, keepdims=True))

    l_ref[...] = jnp.zeros_like(l_ref)
    acc_ref[...] = jnp.zeros_like(acc_ref)

    def sweep_pv(j, nb):
        vj = rows(vb_ref, j, nb)
        for t in range(2):
            p = jnp.concatenate([_exp2_rows(s_ref[t, j + b], m_ref[t]) for b in range(nb)], axis=-1)
            acc_ref[t] += jnp.dot(p.astype(BF16), vj, preferred_element_type=F32)
            l_ref[t] += _fold(jnp.add, p)

    def single(j, carry):
        sweep_pv(j, 1)
        return carry

    lax.fori_loop(0, i + 1, single, 0)
    outs = [acc_ref[t] / jnp.sum(l_ref[t], axis=-1, keepdims=True) for t in range(2)]
    lam = _lambda(lq1_ref, lk1_ref, lq2_ref, lk2_ref, lam_init)
    o_ref[...] = _da_finish(outs[0], outs[1], lam, g_ref[...], lam_init)


def _attn_prompt(proj, n_streams, seq, lq1, lk1, lq2, lk2, da_g, lam_init):
    tq = _tile(seq, 512)
    nq = seq // tq
    dq = 2 * DA_HEAD_DIM

    def const(shape):
        return pl.BlockSpec(shape, lambda b, h, i: (0,) * len(shape))

    return pl.pallas_call(
        functools.partial(_attn_prompt_kernel, lam_init=lam_init, tq=tq),
        out_shape=jax.ShapeDtypeStruct((n_streams * seq, DA_WIDTH), BF16),
        grid=(n_streams, DA_HEADS, nq),
        in_specs=[
            pl.BlockSpec((None, tq, dq), lambda b, h, i: (3, b * nq + i, h)),
            pl.BlockSpec((None, seq, dq), lambda b, h, i: (4, b, h)),
            pl.BlockSpec((None, seq, DA_V_DIM), lambda b, h, i: (5, b, h)),
            const((1, DA_HEAD_DIM)), const((1, DA_HEAD_DIM)), const((1, DA_HEAD_DIM)), const((1, DA_HEAD_DIM)),
            const((1, DA_V_DIM)),
        ],
        out_specs=pl.BlockSpec((tq, DA_V_DIM), lambda b, h, i: (b * nq + i, h)),
        scratch_shapes=[
            pltpu.VMEM((seq, dq), BF16),
            pltpu.VMEM((seq, DA_V_DIM), BF16),
            pltpu.VMEM((2, nq, tq, tq), F32),
            pltpu.VMEM((2, tq, LANES), F32),
            pltpu.VMEM((2, tq, LANES), F32),
            pltpu.VMEM((2, tq, DA_V_DIM), F32),
        ],
        compiler_params=_params("parallel", "parallel", "arbitrary"),
        name="attn_prompt",
    )(proj, proj, proj, lq1, lk1, lq2, lk2, da_g)


def _attn_sample_kernel(q_ref, kn_ref, vn_ref, ck_ref, cv_ref, lq1_ref, lk1_ref, lq2_ref, lk2_ref, g_ref,
                        o_ref, kbuf, vbuf, sem, *, lam_init, li):
    hd = DA_HEAD_DIM
    s = pl.program_id(0)
    slot = s % 2

    def cache_copies(stream, slot_, h):
        return (pltpu.make_async_copy(ck_ref.at[li, stream, :, h, :], kbuf.at[slot_, h], sem.at[0, slot_, h]),
                pltpu.make_async_copy(cv_ref.at[li, stream, :, h, :], vbuf.at[slot_, h], sem.at[1, slot_, h]))

    def start_stream(stream, slot_):
        for h in range(DA_HEADS):
            for cp in cache_copies(stream, slot_, h):
                cp.start()

    @pl.when(s == 0)
    def _():
        start_stream(0, 0)

    @pl.when(s + 1 < pl.num_programs(0))
    def _():
        start_stream(s + 1, 1 - slot)

    lam = _lambda(lq1_ref, lk1_ref, lq2_ref, lk2_ref, lam_init)
    for h in range(DA_HEADS):
        cols = slice(h * DA_V_DIM, (h + 1) * DA_V_DIM)
        qs = (q_ref[:, cols] * Q_SCALE).astype(BF16)
        for cp in cache_copies(s, slot, h):
            cp.wait()
        kc, vc = kbuf[slot, h].astype(BF16), vbuf[slot, h].astype(BF16)
        kn, vn = kn_ref[:, cols].astype(BF16), vn_ref[:, cols].astype(BF16)
        outs = []
        for t in range(2):
            sl = slice(t * hd, (t + 1) * hd)
            sc = lax.dot_general(qs[:, sl], kc[:, sl], NT, preferred_element_type=F32)
            sn = lax.dot_general(qs[:, sl], kn[:, sl], NT, preferred_element_type=F32)
            m = jnp.maximum(jnp.max(_fold(jnp.maximum, sc), axis=-1, keepdims=True),
                            jnp.max(sn, axis=-1, keepdims=True))
            pc = _exp2_rows(sc, _replicate(m))
            pn = jnp.exp2(sn - m)
            l = jnp.sum(_fold(jnp.add, pc), axis=-1, keepdims=True) + jnp.sum(pn, axis=-1, keepdims=True)
            acc = (jnp.dot(pc.astype(BF16), vc, preferred_element_type=F32)
                   + jnp.dot(pn.astype(BF16), vn, preferred_element_type=F32))
            outs.append(acc / l)
        o_ref[:, cols] = _da_finish(outs[0], outs[1], lam, g_ref[...], lam_init)


def _attn_sample(proj, row0, n_streams, seq, cache_k, cache_v, li, lq1, lk1, lq2, lk2, da_g, lam_init):
    past = cache_k.shape[2]
    base = row0 // seq
    dq = 2 * DA_HEAD_DIM

    def const(shape):
        return pl.BlockSpec(shape, lambda s: (0,) * len(shape))

    def new(piece):
        return pl.BlockSpec((None, seq, PIECE), lambda s: (piece, base + s, 0))

    cache_spec = pl.BlockSpec(memory_space=pl.ANY)
    return pl.pallas_call(
        functools.partial(_attn_sample_kernel, lam_init=lam_init, li=li),
        out_shape=jax.ShapeDtypeStruct((n_streams * seq, DA_WIDTH), BF16),
        grid=(n_streams,),
        in_specs=[
            new(3), new(4), new(5), cache_spec, cache_spec,
            const((1, DA_HEAD_DIM)), const((1, DA_HEAD_DIM)), const((1, DA_HEAD_DIM)), const((1, DA_HEAD_DIM)),
            const((1, DA_V_DIM)),
        ],
        out_specs=pl.BlockSpec((seq, DA_WIDTH), lambda s: (s, 0)),
        scratch_shapes=[
            pltpu.VMEM((2, DA_HEADS, past, dq), F32),
            pltpu.VMEM((2, DA_HEADS, past, DA_V_DIM), F32),
            pltpu.SemaphoreType.DMA((2, 2, DA_HEADS)),
        ],
        compiler_params=_params("arbitrary"),
        name="attn_sample",
    )(proj, proj, proj, cache_k, cache_v, lq1, lk1, lq2, lk2, da_g)


def _outproj_kernel(x_ref, ml_ref, da_ref, w1_ref, w2_ref, post_ref, o_ref):
    y = (jnp.dot(ml_ref[...], w1_ref[...], preferred_element_type=F32)
         + jnp.dot(da_ref[...], w2_ref[...], preferred_element_type=F32))
    o_ref[...] = x_ref[...] + _rms(y) * post_ref[...]


def _outproj(x, ml_out, da_out, w_out, post_g, li):
    T, D = x.shape
    tm = _tile(T, 512)
    return pl.pallas_call(
        _outproj_kernel,
        out_shape=jax.ShapeDtypeStruct((T, D), F32),
        grid=(T // tm,),
        in_specs=[
            pl.BlockSpec((tm, D), lambda i: (i, 0)),
            pl.BlockSpec((tm, ML_WIDTH), lambda i: (i, 0)),
            pl.BlockSpec((tm, DA_WIDTH), lambda i: (i, 0)),
            pl.BlockSpec((None, ML_WIDTH, D), lambda i: (li, 0, 0)),
            pl.BlockSpec((None, DA_WIDTH, D), lambda i: (li, 1, 0)),
            pl.BlockSpec((1, D), lambda i: (0, 0)),
        ],
        out_specs=pl.BlockSpec((tm, D), lambda i: (i, 0)),
        compiler_params=_params("parallel"),
        name="outproj",
    )(x, ml_out, da_out, w_out, w_out, post_g)


def kernel(x_prompt, x_sample, cache_k, cache_v, state_C, state_n, state_m, state_conv,
           ffn1_pre_g, ffn1_wg, ffn1_wu, ffn1_wd, ffn1_post_g,
           mix_pre_g, w_in, b_i, b_f, conv_w, conv_b, ml_norm_g,
           lam_q1, lam_k1, lam_q2, lam_k2, da_norm_g, w_out, mix_post_g,
           ffn2_pre_g, ffn2_wg, ffn2_wu, ffn2_wd, ffn2_post_g):
    Bp, S, D = x_prompt.shape
    Bs, Ls, _ = x_sample.shape
    depth = w_in.shape[0]
    past = cache_k.shape[2]
    Tp, Ts = Bp * S, Bs * Ls
    assert Ls == CHUNK and past % LANES == 0 and S % CHUNK == 0 and Tp % Ls == 0
    Lp = 256 if S % 256 == 0 else CHUNK

    xp, xs = x_prompt.reshape(Tp, D), x_sample.reshape(Ts, D)
    zero_state = (jnp.zeros((1, Bp, ML_HEADS, ML_V_DIM, ML_QK_DIM), F32), jnp.zeros((1, Bp, ML_HEADS, ML_QK_DIM), F32),
                  jnp.zeros((1, Bp, 1, ML_HEADS), F32), jnp.zeros((1, Bp, CONV_W - 1, ML_CONV_CH), F32))
    stream_state = (state_C, state_n, state_m.reshape(depth, Bs, 1, ML_HEADS), state_conv)

    def row(v):
        return v.reshape(1, -1)

    w_main = jnp.concatenate([w_in[:, :, :GATE_OFF], w_in[:, :, GATE_OFF + N_GATES:]], axis=2).astype(BF16)
    w_gate = jnp.pad(w_in[:, :, GATE_OFF:GATE_OFF + N_GATES].astype(BF16), ((0, 0), (0, 0), (0, LANES - N_GATES)))
    w_o = w_out.astype(BF16)

    kv_p = kv_s = st_p = st_s = None
    for li in range(depth):
        lam_init = 0.8 - 0.6 * math.exp(-0.3 * li)
        ffn1 = (row(ffn1_pre_g[li]), ffn1_wg, ffn1_wu, ffn1_wd, row(ffn1_post_g[li]), li)
        ffn2 = (row(ffn2_pre_g[li]), ffn2_wg, ffn2_wu, ffn2_wd, row(ffn2_post_g[li]), li)
        bias = jnp.concatenate([b_i[li], b_f[li]])
        bias_c = jnp.pad(bias, (0, LANES - N_GATES)).reshape(1, LANES)
        bias_r = bias.reshape(N_GATES, 1)
        ml_args = (bias_c, bias_r, conv_w[li], row(conv_b[li]), row(ml_norm_g[li]))
        lam_args = (row(lam_q1[li]), row(lam_k1[li]), row(lam_q2[li]), row(lam_k2[li]), row(da_norm_g[li]), lam_init)

        xp = _ffn(xp, *ffn1)
        xs = _ffn(xs, *ffn1)
        proj_p, gates_p, *kv_p = _inproj(xp, row(mix_pre_g[li]), w_main, w_gate, li, depth, kv_p)
        proj_s, gates_s, *kv_s = _inproj(xs, row(mix_pre_g[li]), w_main, w_gate, li, depth, kv_s)
        ml_p, *st_p = _mlstm(proj_p, gates_p, Bp, S, Lp, *ml_args, zero_state, 0, li, depth, st_p)
        ml_s, *st_s = _mlstm(proj_s, gates_s, Bs, Ls, Ls, *ml_args, stream_state, li, li, depth, st_s)
        da_p = _attn_prompt(proj_p, Bp, S, *lam_args)
        da_s = _attn_sample(proj_s, 0, Bs, Ls, cache_k, cache_v, li, *lam_args)
        xp = _ffn(_outproj(xp, ml_p, da_p, w_o, row(mix_post_g[li]), li), *ffn2)
        xs = _ffn(_outproj(xs, ml_s, da_s, w_o, row(mix_post_g[li]), li), *ffn2)

    def states(kv, st, n_streams, seq):
        c, n, m, conv = st
        return (kv[0].reshape(depth, n_streams, seq, DA_HEADS, 2 * DA_HEAD_DIM),
                kv[1].reshape(depth, n_streams, seq, DA_HEADS, DA_V_DIM),
                c, n, m.reshape(depth, n_streams, ML_HEADS), conv)

    return (xp.reshape(Bp, S, D), xs.reshape(Bs, Ls, D), *states(kv_p, st_p, Bp, S), *states(kv_s, st_s, Bs, Ls))
```

```python
import functools
import math

import jax
import jax.numpy as jnp
from jax import lax
from jax.experimental import pallas as pl
from jax.experimental.pallas import tpu as pltpu

F32 = jnp.float32
BF16 = jnp.bfloat16

CHUNK = 64
ML_HEADS = 4
ML_QK_DIM = 128
ML_V_DIM = 256
DA_HEADS = 4
DA_HEAD_DIM = 128
DA_V_DIM = 2 * DA_HEAD_DIM
CONV_W = 4
EPS = 1e-6
ML_WIDTH = ML_HEADS * ML_V_DIM
DA_WIDTH = DA_HEADS * DA_V_DIM
ML_CONV_CH = 2 * ML_HEADS * ML_QK_DIM
DA_QK_WIDTH = DA_HEADS * 2 * DA_HEAD_DIM
PIECE = 1024
N_PIECES = 6
N_GATES = 2 * ML_HEADS
GATE_OFF = 3 * PIECE
assert ML_CONV_CH == ML_WIDTH == DA_QK_WIDTH == DA_WIDTH == PIECE

LANES = 128
CONV_PAD = 8
VMEM_LIMIT = 56 * 1024 * 1024

NEG_INF = float("-inf")
NT = (((1,), (1,)), ((), ()))
TN = (((0,), (0,)), ((), ()))


def _params(*sem):
    return pltpu.CompilerParams(dimension_semantics=sem, vmem_limit_bytes=VMEM_LIMIT)


def _rms(xf):
    return xf * lax.rsqrt(jnp.mean(xf * xf, axis=-1, keepdims=True) + EPS)


def _silu(x):
    return x * jax.nn.sigmoid(x)


def _log_sigmoid(x):
    return jnp.minimum(x, 0.0) - jnp.log1p(jnp.exp(-jnp.abs(x)))


def _tile(n, pref):
    return pref if n % pref == 0 else n


def _ffn_kernel(x_ref, pre_ref, wg_ref, wu_ref, wd_ref, post_ref, o_ref, xn_ref, *, tail):
    j = pl.program_id(1)
    last = pl.num_programs(1) - 1

    @pl.when(j == 0)
    def _():
        xn_ref[...] = (_rms(x_ref[...]) * pre_ref[...]).astype(BF16)
        o_ref[...] = jnp.zeros_like(o_ref)

    def slab(valid):
        xn = xn_ref[...]
        g = jnp.dot(xn, wg_ref[...].astype(BF16), preferred_element_type=F32)
        u = jnp.dot(xn, wu_ref[...].astype(BF16), preferred_element_type=F32)
        h = _silu(g) * u
        wd = wd_ref[...]
        if valid:
            h = jnp.where(lax.broadcasted_iota(jnp.int32, h.shape, 1) < valid, h, 0.0)
            wd = jnp.where(lax.broadcasted_iota(jnp.int32, wd.shape, 0) < valid, wd, 0.0)
        o_ref[...] += jnp.dot(h.astype(BF16), wd.astype(BF16), preferred_element_type=F32)

    if tail:
        pl.when(j < last)(lambda: slab(0))
        pl.when(j == last)(lambda: slab(tail))
    else:
        slab(0)

    @pl.when(j == last)
    def _():
        o_ref[...] = x_ref[...] + 0.5 * (_rms(o_ref[...]) * post_ref[...])


def _ffn(x, pre_g, wg, wu, wd, post_g, li):
    T, D = x.shape
    F = wg.shape[2]
    tm = _tile(T, 1024)
    tf = 256 if F > 256 else F
    nf = pl.cdiv(F, tf)
    return pl.pallas_call(
        functools.partial(_ffn_kernel, tail=F % tf),
        out_shape=jax.ShapeDtypeStruct((T, D), F32),
        grid=(T // tm, nf),
        in_specs=[
            pl.BlockSpec((tm, D), lambda i, j: (i, 0)),
            pl.BlockSpec((1, D), lambda i, j: (0, 0)),
            pl.BlockSpec((None, D, tf), lambda i, j: (li, 0, j)),
            pl.BlockSpec((None, D, tf), lambda i, j: (li, 0, j)),
            pl.BlockSpec((None, tf, D), lambda i, j: (li, j, 0)),
            pl.BlockSpec((1, D), lambda i, j: (0, 0)),
        ],
        out_specs=pl.BlockSpec((tm, D), lambda i, j: (i, 0)),
        scratch_shapes=[pltpu.VMEM((tm, D), BF16)],
        compiler_params=_params("parallel", "arbitrary"),
        name="ffn",
    )(x, pre_g, wg, wu, wd, post_g)


K_PIECE, V_PIECE = 4, 5


def _split_w_in_kernel(w_ref, main_ref, gate_ref):
    w = w_ref[...]
    main_ref[:, :GATE_OFF] = w[:, :GATE_OFF].astype(BF16)
    main_ref[:, GATE_OFF:] = w[:, GATE_OFF + N_GATES:].astype(BF16)
    g = w[:, GATE_OFF:GATE_OFF + LANES]
    gate_ref[...] = jnp.where(lax.broadcasted_iota(jnp.int32, g.shape, 1) < N_GATES, g, 0.0).astype(BF16)


def _split_w_in(w_in):
    depth, D, n_in = w_in.shape
    tr = _tile(D, 256)
    return pl.pallas_call(
        _split_w_in_kernel,
        out_shape=(jax.ShapeDtypeStruct((depth, D, N_PIECES * PIECE), BF16),
                   jax.ShapeDtypeStruct((depth, D, LANES), BF16)),
        grid=(depth, D // tr),
        in_specs=[pl.BlockSpec((None, tr, n_in), lambda l, r: (l, r, 0))],
        out_specs=(pl.BlockSpec((None, tr, N_PIECES * PIECE), lambda l, r: (l, r, 0)),
                   pl.BlockSpec((None, tr, LANES), lambda l, r: (l, r, 0))),
        compiler_params=_params("parallel", "parallel"),
        name="split_w_in",
    )(w_in)


def _inproj_kernel(*refs, li):
    x_ref, pre_ref, w_ref, wgate_ref = refs[:4]
    prev = refs[4:6] if li else ()
    o_ref, gate_ref, k5_ref, v5_ref, xn_ref, kv_ref, sem, carry_sem = refs[4 + len(prev):]
    i, j = pl.program_id(0), pl.program_id(1)
    first = (i == 0) & (j == 0)
    final = (i == pl.num_programs(0) - 1) & (j == pl.num_programs(1) - 1)

    def head_copies(slot, dst_ref, tile):
        return [pltpu.make_async_copy(kv_ref.at[slot, :, pl.ds(h * DA_V_DIM, DA_V_DIM)],
                                      dst_ref.at[li, tile, :, h, :], sem.at[slot, h]) for h in range(DA_HEADS)]

    def carry_copies():
        return [pltpu.make_async_copy(src, dst.at[pl.ds(0, li)], carry_sem.at[n])
                for n, (src, dst) in enumerate(zip(prev, (k5_ref, v5_ref)))]

    @pl.when(first)
    def _():
        for cp in carry_copies():
            cp.start()

    @pl.when(j == 0)
    def _():
        xn = (_rms(x_ref[...]) * pre_ref[...]).astype(BF16)
        xn_ref[...] = xn
        gate_ref[...] = jnp.dot(xn, wgate_ref[...], preferred_element_type=F32)

    @pl.when((j == 0) & (i > 0))
    def _():
        for cp in head_copies(1, v5_ref, i - 1):
            cp.wait()

    res = jnp.dot(xn_ref[...], w_ref[...], preferred_element_type=F32)
    o_ref[...] = res

    @pl.when(j == K_PIECE)
    def _():
        kv_ref[0] = res
        for cp in head_copies(0, k5_ref, i):
            cp.start()

    @pl.when(j == V_PIECE)
    def _():
        for cp in head_copies(0, k5_ref, i):
            cp.wait()
        kv_ref[1] = res
        for cp in head_copies(1, v5_ref, i):
            cp.start()

    @pl.when(final)
    def _():
        for cp in head_copies(1, v5_ref, i) + carry_copies():
            cp.wait()


def _inproj(x, pre_g, w_main, w_gate, li, kv_prev):
    T, D = x.shape
    tm = _tile(T, 1024)
    nt = T // tm
    assert V_PIECE == N_PIECES - 1 and K_PIECE == V_PIECE - 1
    kv_shape = jax.ShapeDtypeStruct((li + 1, nt, tm, DA_HEADS, DA_V_DIM), F32)
    any_spec = pl.BlockSpec(memory_space=pl.ANY)
    return pl.pallas_call(
        functools.partial(_inproj_kernel, li=li),
        out_shape=(jax.ShapeDtypeStruct((N_PIECES, T, PIECE), F32), jax.ShapeDtypeStruct((T, LANES), F32),
                   kv_shape, kv_shape),
        grid=(nt, N_PIECES),
        in_specs=[
            pl.BlockSpec((tm, D), lambda i, j: (i, 0)),
            pl.BlockSpec((1, D), lambda i, j: (0, 0)),
            pl.BlockSpec((None, D, PIECE), lambda i, j: (li, 0, j)),
            pl.BlockSpec((None, D, LANES), lambda i, j: (li, 0, 0)),
        ] + [any_spec] * len(kv_prev),
        out_specs=(pl.BlockSpec((None, tm, PIECE), lambda i, j: (j, i, 0)),
                   pl.BlockSpec((tm, LANES), lambda i, j: (i, 0)), any_spec, any_spec),
        scratch_shapes=[pltpu.VMEM((tm, D), BF16), pltpu.VMEM((2, tm, PIECE), F32),
                        pltpu.SemaphoreType.DMA((2, DA_HEADS)), pltpu.SemaphoreType.DMA((2,))],
        compiler_params=_params("arbitrary", "arbitrary"),
        name="inproj",
    )(x, pre_g, w_main, w_gate, *kv_prev)


def _mlstm_kernel(qk_ref, v_ref, og_ref, gc_ref, gr_ref, bc_ref, br_ref, cw_ref, cb_ref, mlg_ref,
                  c0_ref, n0_ref, m0_ref, conv0_ref,
                  out_ref, cn_ref, nn_ref, mn_ref, convn_ref,
                  xp_ref, c_s, n_s, m_s, *, L):
    c = pl.program_id(1)

    @pl.when(c == 0)
    def _():
        c_s[...] = c0_ref[...]
        n_s[...] = n0_ref[...]
        m_s[...] = m0_ref[...]
        xp_ref[0:CONV_PAD, :] = jnp.zeros((CONV_PAD, ML_CONV_CH), F32)
        xp_ref[CONV_PAD - (CONV_W - 1):CONV_PAD, :] = conv0_ref[...]

    qk = qk_ref[...]
    xp_ref[CONV_PAD:CONV_PAD + L, :] = qk
    cw = cw_ref[...]
    conv = cb_ref[...] + qk * cw[CONV_W - 1:CONV_W, :]
    for t in range(CONV_W - 1):
        off = CONV_PAD - (CONV_W - 1) + t
        conv = conv + xp_ref[off:off + L, :] * cw[t:t + 1, :]
    tail = xp_ref[CONV_PAD + L - (CONV_W - 1):CONV_PAD + L, :]
    xp_ref[CONV_PAD - (CONV_W - 1):CONV_PAD, :] = tail
    act = _silu(conv)

    gcb = gc_ref[...] + bc_ref[...]
    grb = gr_ref[...] + br_ref[...]
    row = lax.broadcasted_iota(jnp.int32, (L, L), 0)
    col = lax.broadcasted_iota(jnp.int32, (L, L), 1)
    causal = col <= row
    b_c = jnp.dot(causal.astype(F32), _log_sigmoid(gcb), precision=lax.Precision.HIGHEST,
                  preferred_element_type=F32)
    b_r = jnp.dot(_log_sigmoid(grb), (row <= col).astype(F32), precision=lax.Precision.HIGHEST,
                  preferred_element_type=F32)

    v_all = v_ref[...]
    hs = []
    for h in range(ML_HEADS):
        q = act[:, h * ML_QK_DIM:(h + 1) * ML_QK_DIM]
        k = act[:, ML_HEADS * ML_QK_DIM + h * ML_QK_DIM:ML_HEADS * ML_QK_DIM + (h + 1) * ML_QK_DIM]
        k = k * (ML_QK_DIM ** -0.5)
        qb = q.astype(BF16)
        vb = v_all[:, h * ML_V_DIM:(h + 1) * ML_V_DIM].astype(BF16)
        ig_r = grb[h:h + 1, :]
        ig_c = gcb[:, h:h + 1]
        bh_r = b_r[ML_HEADS + h:ML_HEADS + h + 1, :]
        bh_c = b_c[:, ML_HEADS + h:ML_HEADS + h + 1]
        m_prev = m_s[:, h:h + 1]
        n_prev = n_s[h:h + 1, :]
        c_prev = c_s[h]

        d = jnp.where(causal, bh_c - bh_r + ig_r, NEG_INF)
        inter = bh_c + m_prev
        m_t = jnp.maximum(inter, jnp.max(d, axis=-1, keepdims=True))
        w_intra = jnp.exp(d - m_t)
        w_inter = jnp.exp(inter - m_t)
        s = lax.dot_general(qb, k.astype(BF16), NT, preferred_element_type=F32) * w_intra
        num = (jnp.dot(s.astype(BF16), vb, preferred_element_type=F32)
               + w_inter * lax.dot_general(qb, c_prev.astype(BF16), NT, preferred_element_type=F32))
        den = (jnp.sum(s, axis=-1, keepdims=True)
               + w_inter * jnp.sum(q * n_prev, axis=-1, keepdims=True))
        hs.append(num / jnp.maximum(jnp.abs(den), jnp.exp(-m_t)))

        m_new = m_t[L - 1:L, :]
        b_last = bh_c[L - 1:L, :]
        w_state = jnp.exp(b_last - bh_c + ig_c - m_new)
        decay = jnp.exp(b_last + m_prev - m_new)
        kw = k * w_state
        c_s[h] = decay * c_prev + lax.dot_general(vb, kw.astype(BF16), TN, preferred_element_type=F32)
        n_s[h:h + 1, :] = decay * n_prev + jnp.sum(kw, axis=0, keepdims=True)
        m_s[:, h:h + 1] = m_new

    gated = jax.nn.sigmoid(og_ref[...]) * jnp.concatenate(hs, axis=-1)
    out_ref[...] = (_rms(gated) * mlg_ref[...]).astype(BF16)

    @pl.when(c == pl.num_programs(1) - 1)
    def _():
        cn_ref[...] = c_s[...]
        nn_ref[...] = n_s[...]
        mn_ref[...] = m_s[...]
        convn_ref[...] = tail


def _mlstm(proj, gates, n_streams, seq, L, bias_c, bias_r, conv_w, conv_b, ml_g, state0, li0):
    nc = seq // L
    rows = n_streams * seq
    g_rows = gates[:, :N_GATES].reshape(n_streams * nc, L, N_GATES).transpose(0, 2, 1)

    def tok(piece):
        return pl.BlockSpec((None, L, PIECE), lambda s, c: (piece, s * nc + c, 0))

    def const(shape):
        return pl.BlockSpec(shape, lambda s, c: (0,) * len(shape))

    state_dims = ((ML_HEADS, ML_V_DIM, ML_QK_DIM), (ML_HEADS, ML_QK_DIM), (1, ML_HEADS), (CONV_W - 1, ML_CONV_CH))

    state_in = [pl.BlockSpec((None, None) + d, lambda s, c, n=len(d): (li0, s) + (0,) * n) for d in state_dims]
    state_out = [pl.BlockSpec((None,) + d, lambda s, c, n=len(d): (s,) + (0,) * n) for d in state_dims]
    return pl.pallas_call(
        functools.partial(_mlstm_kernel, L=L),
        out_shape=(jax.ShapeDtypeStruct((rows, ML_WIDTH), BF16),
                   *(jax.ShapeDtypeStruct((n_streams,) + d, F32) for d in state_dims)),
        grid=(n_streams, nc),
        in_specs=[
            tok(0), tok(1), tok(2),
            pl.BlockSpec((L, LANES), lambda s, c: (s * nc + c, 0)),
            pl.BlockSpec((None, N_GATES, L), lambda s, c: (s * nc + c, 0, 0)),
            const((1, LANES)), const((N_GATES, 1)), const((CONV_W, ML_CONV_CH)), const((1, ML_CONV_CH)),
            const((1, ML_WIDTH)),
            *state_in,
        ],
        out_specs=(pl.BlockSpec((L, ML_WIDTH), lambda s, c: (s * nc + c, 0)), *state_out),
        scratch_shapes=[
            pltpu.VMEM((CONV_PAD + L, ML_CONV_CH), F32),
            pltpu.VMEM((ML_HEADS, ML_V_DIM, ML_QK_DIM), F32),
            pltpu.VMEM((ML_HEADS, ML_QK_DIM), F32),
            pltpu.VMEM((1, ML_HEADS), F32),
        ],
        compiler_params=_params("arbitrary", "arbitrary"),
        name="mlstm",
    )(proj, proj, proj, gates, g_rows, bias_c, bias_r, conv_w, conv_b, ml_g, *state0)


def _lambda(lq1_ref, lk1_ref, lq2_ref, lk2_ref, lam_init):
    return (jnp.exp(jnp.sum(lq1_ref[...] * lk1_ref[...], axis=-1, keepdims=True))
            - jnp.exp(jnp.sum(lq2_ref[...] * lk2_ref[...], axis=-1, keepdims=True)) + lam_init)


def _lane_tiles(x):
    return [x[:, c * LANES:(c + 1) * LANES] for c in range(x.shape[1] // LANES)]


def _fold(op, x):
    return functools.reduce(op, _lane_tiles(x))


def _exp2_rows(s, m_b):
    return jnp.concatenate([jnp.exp2(t - m_b) for t in _lane_tiles(s)], axis=-1)


def _replicate(col):
    return jnp.broadcast_to(col, (col.shape[0], LANES))


def _da_finish(out1, out2, lam, g, lam_init):
    att = out1 - lam * out2
    return ((_rms(att) * g) * (1.0 - lam_init)).astype(BF16)


LOG2E = math.log2(math.e)
Q_SCALE = DA_HEAD_DIM ** -0.5 * LOG2E


def _attn_prompt_kernel(q_ref, k_ref, v_ref, lq1_ref, lk1_ref, lq2_ref, lk2_ref, g_ref, o_ref,
                        kb_ref, vb_ref, s_ref, m_ref, l_ref, acc_ref, *, lam_init, tq):
    i = pl.program_id(2)
    hd = DA_HEAD_DIM

    @pl.when(i == 0)
    def _():
        kb_ref[...] = k_ref[...].astype(BF16)
        vb_ref[...] = v_ref[...].astype(BF16)

    qs = (q_ref[...] * Q_SCALE).astype(BF16)

    def rows(ref, j, nb):
        return ref[pl.ds(pl.multiple_of(j * tq, tq), nb * tq), :]

    def scores(kj, t):
        return lax.dot_general(qs[:, t * hd:(t + 1) * hd], kj[:, t * hd:(t + 1) * hd], NT,
                               preferred_element_type=F32)

    def note_scores(j, nb, t, s):
        for b in range(nb):
            s_ref[t, j + b] = s[:, b * tq:(b + 1) * tq]
        m_ref[t] = jnp.maximum(m_ref[t], _fold(jnp.maximum, s))

    def for_blocks(n, fn):
        def pair(k, carry):
            fn(2 * k, 2)
            return carry

        lax.fori_loop(0, n // 2, pair, 0)
        pl.when(n % 2 == 1)(lambda: fn(n - 1, 1))

    m_ref[...] = jnp.full_like(m_ref, NEG_INF)

    def sweep_max(j, nb):
        kj = rows(kb_ref, j, nb)
        for t in range(2):
            note_scores(j, nb, t, scores(kj, t))

    for_blocks(i, sweep_max)
    diag = (lax.broadcasted_iota(jnp.int32, (tq, tq), 0) // CHUNK
            >= lax.broadcasted_iota(jnp.int32, (tq, tq), 1) // CHUNK)
    kd = rows(kb_ref, i, 1)
    for t in range(2):
        note_scores(i, 1, t, jnp.where(diag, scores(kd, t), NEG_INF))
        m_ref[t] = _replicate(jnp.max(m_ref[t], axis=-1, keepdims=True))

    l_ref[...] = jnp.zeros_like(l_ref)
    acc_ref[...] = jnp.zeros_like(acc_ref)

    def sweep_pv(j, nb):
        vj = rows(vb_ref, j, nb)
        for t in range(2):
            p = jnp.concatenate([_exp2_rows(s_ref[t, j + b], m_ref[t]) for b in range(nb)], axis=-1)
            acc_ref[t] += jnp.dot(p.astype(BF16), vj, preferred_element_type=F32)
            l_ref[t] += _fold(jnp.add, p)

    def single(j, carry):
        sweep_pv(j, 1)
        return carry

    lax.fori_loop(0, i + 1, single, 0)
    outs = [acc_ref[t] / jnp.sum(l_ref[t], axis=-1, keepdims=True) for t in range(2)]
    lam = _lambda(lq1_ref, lk1_ref, lq2_ref, lk2_ref, lam_init)
    o_ref[...] = _da_finish(outs[0], outs[1], lam, g_ref[...], lam_init)


def _attn_prompt(proj, n_streams, seq, lq1, lk1, lq2, lk2, da_g, lam_init):
    tq = _tile(seq, 512)
    nq = seq // tq
    dq = 2 * DA_HEAD_DIM

    def const(shape):
        return pl.BlockSpec(shape, lambda b, h, i: (0,) * len(shape))

    return pl.pallas_call(
        functools.partial(_attn_prompt_kernel, lam_init=lam_init, tq=tq),
        out_shape=jax.ShapeDtypeStruct((n_streams * seq, DA_WIDTH), BF16),
        grid=(n_streams, DA_HEADS, nq),
        in_specs=[
            pl.BlockSpec((None, tq, dq), lambda b, h, i: (3, b * nq + i, h)),
            pl.BlockSpec((None, seq, dq), lambda b, h, i: (4, b, h)),
            pl.BlockSpec((None, seq, DA_V_DIM), lambda b, h, i: (5, b, h)),
            const((1, DA_HEAD_DIM)), const((1, DA_HEAD_DIM)), const((1, DA_HEAD_DIM)), const((1, DA_HEAD_DIM)),
            const((1, DA_V_DIM)),
        ],
        out_specs=pl.BlockSpec((tq, DA_V_DIM), lambda b, h, i: (b * nq + i, h)),
        scratch_shapes=[
            pltpu.VMEM((seq, dq), BF16),
            pltpu.VMEM((seq, DA_V_DIM), BF16),
            pltpu.VMEM((2, nq, tq, tq), F32),
            pltpu.VMEM((2, tq, LANES), F32),
            pltpu.VMEM((2, tq, LANES), F32),
            pltpu.VMEM((2, tq, DA_V_DIM), F32),
        ],
        compiler_params=_params("parallel", "parallel", "arbitrary"),
        name="attn_prompt",
    )(proj, proj, proj, lq1, lk1, lq2, lk2, da_g)


def _attn_sample_kernel(q_ref, kn_ref, vn_ref, ck_ref, cv_ref, lq1_ref, lk1_ref, lq2_ref, lk2_ref, g_ref,
                        o_ref, kbuf, vbuf, sem, *, lam_init, li):
    hd = DA_HEAD_DIM
    s = pl.program_id(0)
    slot = s % 2

    def cache_copies(stream, slot_, h):
        return (pltpu.make_async_copy(ck_ref.at[li, stream, :, h, :], kbuf.at[slot_, h], sem.at[0, slot_, h]),
                pltpu.make_async_copy(cv_ref.at[li, stream, :, h, :], vbuf.at[slot_, h], sem.at[1, slot_, h]))

    def start_stream(stream, slot_):
        for h in range(DA_HEADS):
            for cp in cache_copies(stream, slot_, h):
                cp.start()

    @pl.when(s == 0)
    def _():
        start_stream(0, 0)

    @pl.when(s + 1 < pl.num_programs(0))
    def _():
        start_stream(s + 1, 1 - slot)

    lam = _lambda(lq1_ref, lk1_ref, lq2_ref, lk2_ref, lam_init)
    for h in range(DA_HEADS):
        cols = slice(h * DA_V_DIM, (h + 1) * DA_V_DIM)
        qs = (q_ref[:, cols] * Q_SCALE).astype(BF16)
        for cp in cache_copies(s, slot, h):
            cp.wait()
        kc, vc = kbuf[slot, h].astype(BF16), vbuf[slot, h].astype(BF16)
        kn, vn = kn_ref[:, cols].astype(BF16), vn_ref[:, cols].astype(BF16)
        outs = []
        for t in range(2):
            sl = slice(t * hd, (t + 1) * hd)
            sc = lax.dot_general(qs[:, sl], kc[:, sl], NT, preferred_element_type=F32)
            sn = lax.dot_general(qs[:, sl], kn[:, sl], NT, preferred_element_type=F32)
            m = jnp.maximum(jnp.max(_fold(jnp.maximum, sc), axis=-1, keepdims=True),
                            jnp.max(sn, axis=-1, keepdims=True))
            pc = _exp2_rows(sc, _replicate(m))
            pn = jnp.exp2(sn - m)
            l = jnp.sum(_fold(jnp.add, pc), axis=-1, keepdims=True) + jnp.sum(pn, axis=-1, keepdims=True)
            acc = (jnp.dot(pc.astype(BF16), vc, preferred_element_type=F32)
                   + jnp.dot(pn.astype(BF16), vn, preferred_element_type=F32))
            outs.append(acc / l)
        o_ref[:, cols] = _da_finish(outs[0], outs[1], lam, g_ref[...], lam_init)


def _attn_sample(proj, row0, n_streams, seq, cache_k, cache_v, li, lq1, lk1, lq2, lk2, da_g, lam_init):
    past = cache_k.shape[2]
    base = row0 // seq
    dq = 2 * DA_HEAD_DIM

    def const(shape):
        return pl.BlockSpec(shape, lambda s: (0,) * len(shape))

    def new(piece):
        return pl.BlockSpec((None, seq, PIECE), lambda s: (piece, base + s, 0))

    cache_spec = pl.BlockSpec(memory_space=pl.ANY)
    return pl.pallas_call(
        functools.partial(_attn_sample_kernel, lam_init=lam_init, li=li),
        out_shape=jax.ShapeDtypeStruct((n_streams * seq, DA_WIDTH), BF16),
        grid=(n_streams,),
        in_specs=[
            new(3), new(4), new(5), cache_spec, cache_spec,
            const((1, DA_HEAD_DIM)), const((1, DA_HEAD_DIM)), const((1, DA_HEAD_DIM)), const((1, DA_HEAD_DIM)),
            const((1, DA_V_DIM)),
        ],
        out_specs=pl.BlockSpec((seq, DA_WIDTH), lambda s: (s, 0)),
        scratch_shapes=[
            pltpu.VMEM((2, DA_HEADS, past, dq), F32),
            pltpu.VMEM((2, DA_HEADS, past, DA_V_DIM), F32),
            pltpu.SemaphoreType.DMA((2, 2, DA_HEADS)),
        ],
        compiler_params=_params("arbitrary"),
        name="attn_sample",
    )(proj, proj, proj, cache_k, cache_v, lq1, lk1, lq2, lk2, da_g)


def _outproj_kernel(x_ref, ml_ref, da_ref, w1_ref, w2_ref, post_ref, o_ref):
    y = (jnp.dot(ml_ref[...], w1_ref[...], preferred_element_type=F32)
         + jnp.dot(da_ref[...], w2_ref[...], preferred_element_type=F32))
    o_ref[...] = x_ref[...] + _rms(y) * post_ref[...]


def _outproj(x, ml_out, da_out, w_out, post_g, li):
    T, D = x.shape
    tm = _tile(T, 512)
    return pl.pallas_call(
        _outproj_kernel,
        out_shape=jax.ShapeDtypeStruct((T, D), F32),
        grid=(T // tm,),
        in_specs=[
            pl.BlockSpec((tm, D), lambda i: (i, 0)),
            pl.BlockSpec((tm, ML_WIDTH), lambda i: (i, 0)),
            pl.BlockSpec((tm, DA_WIDTH), lambda i: (i, 0)),
            pl.BlockSpec((None, ML_WIDTH, D), lambda i: (li, 0, 0)),
            pl.BlockSpec((None, DA_WIDTH, D), lambda i: (li, 1, 0)),
            pl.BlockSpec((1, D), lambda i: (0, 0)),
        ],
        out_specs=pl.BlockSpec((tm, D), lambda i: (i, 0)),
        compiler_params=_params("parallel"),
        name="outproj",
    )(x, ml_out, da_out, w_out, w_out, post_g)


def kernel(x_prompt, x_sample, cache_k, cache_v, state_C, state_n, state_m, state_conv,
           ffn1_pre_g, ffn1_wg, ffn1_wu, ffn1_wd, ffn1_post_g,
           mix_pre_g, w_in, b_i, b_f, conv_w, conv_b, ml_norm_g,
           lam_q1, lam_k1, lam_q2, lam_k2, da_norm_g, w_out, mix_post_g,
           ffn2_pre_g, ffn2_wg, ffn2_wu, ffn2_wd, ffn2_post_g):
    Bp, S, D = x_prompt.shape
    Bs, Ls, _ = x_sample.shape
    depth = w_in.shape[0]
    past = cache_k.shape[2]
    Tp, Ts = Bp * S, Bs * Ls
    assert Ls == CHUNK and past % LANES == 0 and S % CHUNK == 0 and Tp % Ls == 0
    Lp = 256 if S % 256 == 0 else CHUNK

    xp, xs = x_prompt.reshape(Tp, D), x_sample.reshape(Ts, D)
    zero_state = (jnp.zeros((1, Bp, ML_HEADS, ML_V_DIM, ML_QK_DIM), F32), jnp.zeros((1, Bp, ML_HEADS, ML_QK_DIM), F32),
                  jnp.zeros((1, Bp, 1, ML_HEADS), F32), jnp.zeros((1, Bp, CONV_W - 1, ML_CONV_CH), F32))
    stream_state = (state_C, state_n, state_m.reshape(depth, Bs, 1, ML_HEADS), state_conv)

    def row(v):
        return v.reshape(1, -1)

    w_main, w_gate = _split_w_in(w_in)
    w_o = w_out.astype(BF16)

    kv_p, kv_s, st_p, st_s = (), (), [], []
    for li in range(depth):
        lam_init = 0.8 - 0.6 * math.exp(-0.3 * li)
        ffn1 = (row(ffn1_pre_g[li]), ffn1_wg, ffn1_wu, ffn1_wd, row(ffn1_post_g[li]), li)
        ffn2 = (row(ffn2_pre_g[li]), ffn2_wg, ffn2_wu, ffn2_wd, row(ffn2_post_g[li]), li)
        bias = jnp.concatenate([b_i[li], b_f[li]])
        bias_c = jnp.pad(bias, (0, LANES - N_GATES)).reshape(1, LANES)
        bias_r = bias.reshape(N_GATES, 1)
        ml_args = (bias_c, bias_r, conv_w[li], row(conv_b[li]), row(ml_norm_g[li]))
        lam_args = (row(lam_q1[li]), row(lam_k1[li]), row(lam_q2[li]), row(lam_k2[li]), row(da_norm_g[li]), lam_init)

        xp = _ffn(xp, *ffn1)
        xs = _ffn(xs, *ffn1)
        proj_p, gates_p, *kv_p = _inproj(xp, row(mix_pre_g[li]), w_main, w_gate, li, kv_p)
        proj_s, gates_s, *kv_s = _inproj(xs, row(mix_pre_g[li]), w_main, w_gate, li, kv_s)
        ml_p, *st = _mlstm(proj_p, gates_p, Bp, S, Lp, *ml_args, zero_state, 0)
        st_p.append(st)
        ml_s, *st = _mlstm(proj_s, gates_s, Bs, Ls, Ls, *ml_args, stream_state, li)
        st_s.append(st)
        da_p = _attn_prompt(proj_p, Bp, S, *lam_args)
        da_s = _attn_sample(proj_s, 0, Bs, Ls, cache_k, cache_v, li, *lam_args)
        xp = _ffn(_outproj(xp, ml_p, da_p, w_o, row(mix_post_g[li]), li), *ffn2)
        xs = _ffn(_outproj(xs, ml_s, da_s, w_o, row(mix_post_g[li]), li), *ffn2)

    def states(kv, st, n_streams, seq):
        c, n, m, conv = (jnp.stack(e) for e in zip(*st))
        return (kv[0].reshape(depth, n_streams, seq, DA_HEADS, 2 * DA_HEAD_DIM),
                kv[1].reshape(depth, n_streams, seq, DA_HEADS, DA_V_DIM),
                c, n, m.reshape(depth, n_streams, ML_HEADS), conv)

    return (xp.reshape(Bp, S, D), xs.reshape(Bs, Ls, D), *states(kv_p, st_p, Bp, S), *states(kv_s, st_s, Bs, Ls))
```

```python
import functools
import math

import jax
import jax.numpy as jnp
from jax import lax
from jax.experimental import pallas as pl
from jax.experimental.pallas import tpu as pltpu

F32 = jnp.float32
BF16 = jnp.bfloat16

CHUNK = 64
ML_HEADS = 4
ML_QK_DIM = 128
ML_V_DIM = 256
DA_HEADS = 4
DA_HEAD_DIM = 128
DA_V_DIM = 2 * DA_HEAD_DIM
CONV_W = 4
EPS = 1e-6
ML_WIDTH = ML_HEADS * ML_V_DIM
DA_WIDTH = DA_HEADS * DA_V_DIM
ML_CONV_CH = 2 * ML_HEADS * ML_QK_DIM
DA_QK_WIDTH = DA_HEADS * 2 * DA_HEAD_DIM
PIECE = 1024
N_PIECES = 6
N_GATES = 2 * ML_HEADS
GATE_OFF = 3 * PIECE
assert ML_CONV_CH == ML_WIDTH == DA_QK_WIDTH == DA_WIDTH == PIECE

LANES = 128
CONV_PAD = 8
VMEM_LIMIT = 56 * 1024 * 1024

NEG_INF = float("-inf")
NT = (((1,), (1,)), ((), ()))
TN = (((0,), (0,)), ((), ()))


def _params(*sem):
    return pltpu.CompilerParams(dimension_semantics=sem, vmem_limit_bytes=VMEM_LIMIT)


def _rms(xf):
    return xf * lax.rsqrt(jnp.mean(xf * xf, axis=-1, keepdims=True) + EPS)


def _silu(x):
    return x * jax.nn.sigmoid(x)


def _log_sigmoid(x):
    return jnp.minimum(x, 0.0) - jnp.log1p(jnp.exp(-jnp.abs(x)))


def _tile(n, pref):
    return pref if n % pref == 0 else n


def _ffn_kernel(x_ref, pre_ref, wg_ref, wu_ref, wd_ref, post_ref, o_ref, xn_ref, *, tail):
    j = pl.program_id(1)
    last = pl.num_programs(1) - 1

    @pl.when(j == 0)
    def _():
        xn_ref[...] = (_rms(x_ref[...]) * pre_ref[...]).astype(BF16)
        o_ref[...] = jnp.zeros_like(o_ref)

    def slab(valid):
        xn = xn_ref[...]
        g = jnp.dot(xn, wg_ref[...].astype(BF16), preferred_element_type=F32)
        u = jnp.dot(xn, wu_ref[...].astype(BF16), preferred_element_type=F32)
        h = _silu(g) * u
        wd = wd_ref[...]
        if valid:
            h = jnp.where(lax.broadcasted_iota(jnp.int32, h.shape, 1) < valid, h, 0.0)
            wd = jnp.where(lax.broadcasted_iota(jnp.int32, wd.shape, 0) < valid, wd, 0.0)
        o_ref[...] += jnp.dot(h.astype(BF16), wd.astype(BF16), preferred_element_type=F32)

    if tail:
        pl.when(j < last)(lambda: slab(0))
        pl.when(j == last)(lambda: slab(tail))
    else:
        slab(0)

    @pl.when(j == last)
    def _():
        o_ref[...] = x_ref[...] + 0.5 * (_rms(o_ref[...]) * post_ref[...])


def _ffn(x, pre_g, wg, wu, wd, post_g, li):
    T, D = x.shape
    F = wg.shape[2]
    tm = _tile(T, 1024)
    tf = 256 if F > 256 else F
    nf = pl.cdiv(F, tf)
    return pl.pallas_call(
        functools.partial(_ffn_kernel, tail=F % tf),
        out_shape=jax.ShapeDtypeStruct((T, D), F32),
        grid=(T // tm, nf),
        in_specs=[
            pl.BlockSpec((tm, D), lambda i, j: (i, 0)),
            pl.BlockSpec((1, D), lambda i, j: (0, 0)),
            pl.BlockSpec((None, D, tf), lambda i, j: (li, 0, j)),
            pl.BlockSpec((None, D, tf), lambda i, j: (li, 0, j)),
            pl.BlockSpec((None, tf, D), lambda i, j: (li, j, 0)),
            pl.BlockSpec((1, D), lambda i, j: (0, 0)),
        ],
        out_specs=pl.BlockSpec((tm, D), lambda i, j: (i, 0)),
        scratch_shapes=[pltpu.VMEM((tm, D), BF16)],
        compiler_params=_params("parallel", "arbitrary"),
        name="ffn",
    )(x, pre_g, wg, wu, wd, post_g)


K_PIECE, V_PIECE = 4, 5


def _inproj_kernel(x_ref, pre_ref, w_ref, wgate_ref, k5_in, v5_in, o_ref, gate_ref, k5_ref, v5_ref,
                   xn_ref, kv_ref, sem, *, li):
    del k5_in, v5_in
    i, j = pl.program_id(0), pl.program_id(1)
    final = (i == pl.num_programs(0) - 1) & (j == pl.num_programs(1) - 1)

    def head_copies(slot, dst_ref, tile):
        return [pltpu.make_async_copy(kv_ref.at[slot, :, pl.ds(h * DA_V_DIM, DA_V_DIM)],
                                      dst_ref.at[li, tile, :, h, :], sem.at[slot, h]) for h in range(DA_HEADS)]

    @pl.when(j == 0)
    def _():
        xn = (_rms(x_ref[...]) * pre_ref[...]).astype(BF16)
        xn_ref[...] = xn
        gate_ref[...] = jnp.dot(xn, wgate_ref[...], preferred_element_type=F32)

    @pl.when((j == 0) & (i > 0))
    def _():
        for cp in head_copies(1, v5_ref, i - 1):
            cp.wait()

    res = jnp.dot(xn_ref[...], w_ref[...], preferred_element_type=F32)
    o_ref[...] = res

    @pl.when(j == K_PIECE)
    def _():
        kv_ref[0] = res
        for cp in head_copies(0, k5_ref, i):
            cp.start()

    @pl.when(j == V_PIECE)
    def _():
        for cp in head_copies(0, k5_ref, i):
            cp.wait()
        kv_ref[1] = res
        for cp in head_copies(1, v5_ref, i):
            cp.start()

    @pl.when(final)
    def _():
        for cp in head_copies(1, v5_ref, i):
            cp.wait()


def _kv_tile(T):
    return _tile(T, 1024)


def _inproj(x, pre_g, w_main, w_gate, li, kv):
    T, D = x.shape
    tm = _kv_tile(T)
    assert V_PIECE == N_PIECES - 1 and K_PIECE == V_PIECE - 1
    kv_shape = jax.ShapeDtypeStruct(kv[0].shape, F32)
    any_spec = pl.BlockSpec(memory_space=pl.ANY)
    return pl.pallas_call(
        functools.partial(_inproj_kernel, li=li),
        out_shape=(jax.ShapeDtypeStruct((N_PIECES, T, PIECE), F32), jax.ShapeDtypeStruct((T, LANES), F32),
                   kv_shape, kv_shape),
        grid=(T // tm, N_PIECES),
        in_specs=[
            pl.BlockSpec((tm, D), lambda i, j: (i, 0)),
            pl.BlockSpec((1, D), lambda i, j: (0, 0)),
            pl.BlockSpec((None, D, PIECE), lambda i, j: (li, 0, j)),
            pl.BlockSpec((None, D, LANES), lambda i, j: (li, 0, 0)),
            any_spec, any_spec,
        ],
        out_specs=(pl.BlockSpec((None, tm, PIECE), lambda i, j: (j, i, 0)),
                   pl.BlockSpec((tm, LANES), lambda i, j: (i, 0)), any_spec, any_spec),
        scratch_shapes=[pltpu.VMEM((tm, D), BF16), pltpu.VMEM((2, tm, PIECE), F32),
                        pltpu.SemaphoreType.DMA((2, DA_HEADS))],
        input_output_aliases={4: 2, 5: 3},
        compiler_params=_params("arbitrary", "arbitrary"),
        name="inproj",
    )(x, pre_g, w_main, w_gate, *kv)


def _mlstm_kernel(qk_ref, v_ref, og_ref, gc_ref, gr_ref, bc_ref, br_ref, cw_ref, cb_ref, mlg_ref,
                  c0_ref, n0_ref, m0_ref, conv0_ref,
                  out_ref, cn_ref, nn_ref, mn_ref, convn_ref,
                  xp_ref, c_s, n_s, m_s, *, L):
    c = pl.program_id(1)

    @pl.when(c == 0)
    def _():
        c_s[...] = c0_ref[...]
        n_s[...] = n0_ref[...]
        m_s[...] = m0_ref[...]
        xp_ref[0:CONV_PAD, :] = jnp.zeros((CONV_PAD, ML_CONV_CH), F32)
        xp_ref[CONV_PAD - (CONV_W - 1):CONV_PAD, :] = conv0_ref[...]

    qk = qk_ref[...]
    xp_ref[CONV_PAD:CONV_PAD + L, :] = qk
    cw = cw_ref[...]
    conv = cb_ref[...] + qk * cw[CONV_W - 1:CONV_W, :]
    for t in range(CONV_W - 1):
        off = CONV_PAD - (CONV_W - 1) + t
        conv = conv + xp_ref[off:off + L, :] * cw[t:t + 1, :]
    tail = xp_ref[CONV_PAD + L - (CONV_W - 1):CONV_PAD + L, :]
    xp_ref[CONV_PAD - (CONV_W - 1):CONV_PAD, :] = tail
    act = _silu(conv)

    gcb = gc_ref[...] + bc_ref[...]
    grb = gr_ref[...] + br_ref[...]
    row = lax.broadcasted_iota(jnp.int32, (L, L), 0)
    col = lax.broadcasted_iota(jnp.int32, (L, L), 1)
    causal = col <= row
    b_c = jnp.dot(causal.astype(F32), _log_sigmoid(gcb), precision=lax.Precision.HIGHEST,
                  preferred_element_type=F32)
    b_r = jnp.dot(_log_sigmoid(grb), (row <= col).astype(F32), precision=lax.Precision.HIGHEST,
                  preferred_element_type=F32)

    v_all = v_ref[...]
    hs = []
    for h in range(ML_HEADS):
        q = act[:, h * ML_QK_DIM:(h + 1) * ML_QK_DIM]
        k = act[:, ML_HEADS * ML_QK_DIM + h * ML_QK_DIM:ML_HEADS * ML_QK_DIM + (h + 1) * ML_QK_DIM]
        k = k * (ML_QK_DIM ** -0.5)
        qb = q.astype(BF16)
        vb = v_all[:, h * ML_V_DIM:(h + 1) * ML_V_DIM].astype(BF16)
        ig_r = grb[h:h + 1, :]
        ig_c = gcb[:, h:h + 1]
        bh_r = b_r[ML_HEADS + h:ML_HEADS + h + 1, :]
        bh_c = b_c[:, ML_HEADS + h:ML_HEADS + h + 1]
        m_prev = m_s[:, h:h + 1]
        n_prev = n_s[h:h + 1, :]
        c_prev = c_s[h]

        d = jnp.where(causal, bh_c - bh_r + ig_r, NEG_INF)
        inter = bh_c + m_prev
        m_t = jnp.maximum(inter, jnp.max(d, axis=-1, keepdims=True))
        w_intra = jnp.exp(d - m_t)
        w_inter = jnp.exp(inter - m_t)
        s = lax.dot_general(qb, k.astype(BF16), NT, preferred_element_type=F32) * w_intra
        num = (jnp.dot(s.astype(BF16), vb, preferred_element_type=F32)
               + w_inter * lax.dot_general(qb, c_prev.astype(BF16), NT, preferred_element_type=F32))
        den = (jnp.sum(s, axis=-1, keepdims=True)
               + w_inter * jnp.sum(q * n_prev, axis=-1, keepdims=True))
        hs.append(num * (1.0 / jnp.maximum(jnp.abs(den), jnp.exp(-m_t))))

        m_new = m_t[L - 1:L, :]
        b_last = bh_c[L - 1:L, :]
        w_state = jnp.exp(b_last - bh_c + ig_c - m_new)
        decay = jnp.exp(b_last + m_prev - m_new)
        kw = k * w_state
        c_s[h] = decay * c_prev + lax.dot_general(vb, kw.astype(BF16), TN, preferred_element_type=F32)
        n_s[h:h + 1, :] = decay * n_prev + jnp.sum(kw, axis=0, keepdims=True)
        m_s[:, h:h + 1] = m_new

    gated = jax.nn.sigmoid(og_ref[...]) * jnp.concatenate(hs, axis=-1)
    out_ref[...] = (_rms(gated) * mlg_ref[...]).astype(BF16)

    @pl.when(c == pl.num_programs(1) - 1)
    def _():
        cn_ref[...] = c_s[...]
        nn_ref[...] = n_s[...]
        mn_ref[...] = m_s[...]
        convn_ref[...] = tail


def _mlstm(proj, gates, n_streams, seq, L, bias_c, bias_r, conv_w, conv_b, ml_g, state0, li0):
    nc = seq // L
    rows = n_streams * seq
    g_rows = gates[:, :N_GATES].reshape(n_streams * nc, L, N_GATES).transpose(0, 2, 1)

    def tok(piece):
        return pl.BlockSpec((None, L, PIECE), lambda s, c: (piece, s * nc + c, 0))

    def const(shape):
        return pl.BlockSpec(shape, lambda s, c: (0,) * len(shape))

    state_dims = ((ML_HEADS, ML_V_DIM, ML_QK_DIM), (ML_HEADS, ML_QK_DIM), (1, ML_HEADS), (CONV_W - 1, ML_CONV_CH))

    state_in = [pl.BlockSpec((None, None) + d, lambda s, c, n=len(d): (li0, s) + (0,) * n) for d in state_dims]
    state_out = [pl.BlockSpec((None,) + d, lambda s, c, n=len(d): (s,) + (0,) * n) for d in state_dims]
    return pl.pallas_call(
        functools.partial(_mlstm_kernel, L=L),
        out_shape=(jax.ShapeDtypeStruct((rows, ML_WIDTH), BF16),
                   *(jax.ShapeDtypeStruct((n_streams,) + d, F32) for d in state_dims)),
        grid=(n_streams, nc),
        in_specs=[
            tok(0), tok(1), tok(2),
            pl.BlockSpec((L, LANES), lambda s, c: (s * nc + c, 0)),
            pl.BlockSpec((None, N_GATES, L), lambda s, c: (s * nc + c, 0, 0)),
            const((1, LANES)), const((N_GATES, 1)), const((CONV_W, ML_CONV_CH)), const((1, ML_CONV_CH)),
            const((1, ML_WIDTH)),
            *state_in,
        ],
        out_specs=(pl.BlockSpec((L, ML_WIDTH), lambda s, c: (s * nc + c, 0)), *state_out),
        scratch_shapes=[
            pltpu.VMEM((CONV_PAD + L, ML_CONV_CH), F32),
            pltpu.VMEM((ML_HEADS, ML_V_DIM, ML_QK_DIM), F32),
            pltpu.VMEM((ML_HEADS, ML_QK_DIM), F32),
            pltpu.VMEM((1, ML_HEADS), F32),
        ],
        compiler_params=_params("arbitrary", "arbitrary"),
        name="mlstm",
    )(proj, proj, proj, gates, g_rows, bias_c, bias_r, conv_w, conv_b, ml_g, *state0)


def _lambda(lq1_ref, lk1_ref, lq2_ref, lk2_ref, lam_init):
    return (jnp.exp(jnp.sum(lq1_ref[...] * lk1_ref[...], axis=-1, keepdims=True))
            - jnp.exp(jnp.sum(lq2_ref[...] * lk2_ref[...], axis=-1, keepdims=True)) + lam_init)


def _lane_tiles(x):
    return [x[:, c * LANES:(c + 1) * LANES] for c in range(x.shape[1] // LANES)]


def _fold(op, x):
    return functools.reduce(op, _lane_tiles(x))


def _exp2_rows(s, m_b):
    return jnp.concatenate([jnp.exp2(t - m_b) for t in _lane_tiles(s)], axis=-1)


def _replicate(col):
    return jnp.broadcast_to(col, (col.shape[0], LANES))


def _da_finish(out1, out2, lam, g, lam_init):
    att = out1 - lam * out2
    return ((_rms(att) * g) * (1.0 - lam_init)).astype(BF16)


LOG2E = math.log2(math.e)
Q_SCALE = DA_HEAD_DIM ** -0.5 * LOG2E


def _attn_prompt_kernel(q_ref, k_ref, v_ref, lq1_ref, lk1_ref, lq2_ref, lk2_ref, g_ref, o_ref,
                        kb_ref, vb_ref, s_ref, m_ref, l_ref, acc_ref, *, lam_init, tq):
    i = pl.program_id(2)
    hd = DA_HEAD_DIM

    @pl.when(i == 0)
    def _():
        kb_ref[...] = k_ref[...].astype(BF16)
        vb_ref[...] = v_ref[...].astype(BF16)

    qs = (q_ref[...] * Q_SCALE).astype(BF16)

    def rows(ref, j, nb):
        return ref[pl.ds(pl.multiple_of(j * tq, tq), nb * tq), :]

    def scores(kj, t):
        return lax.dot_general(qs[:, t * hd:(t + 1) * hd], kj[:, t * hd:(t + 1) * hd], NT,
                               preferred_element_type=F32)

    def note_scores(j, nb, t, s):
        for b in range(nb):
            s_ref[t, j + b] = s[:, b * tq:(b + 1) * tq]
        m_ref[t] = jnp.maximum(m_ref[t], _fold(jnp.maximum, s))

    def for_blocks(n, fn):
        def pair(k, carry):
            fn(2 * k, 2)
            return carry

        lax.fori_loop(0, n // 2, pair, 0)
        pl.when(n % 2 == 1)(lambda: fn(n - 1, 1))

    m_ref[...] = jnp.full_like(m_ref, NEG_INF)

    def sweep_max(j, nb):
        kj = rows(kb_ref, j, nb)
        for t in range(2):
            note_scores(j, nb, t, scores(kj, t))

    for_blocks(i, sweep_max)
    diag = (lax.broadcasted_iota(jnp.int32, (tq, tq), 0) // CHUNK
            >= lax.broadcasted_iota(jnp.int32, (tq, tq), 1) // CHUNK)
    kd = rows(kb_ref, i, 1)
    for t in range(2):
        note_scores(i, 1, t, jnp.where(diag, scores(kd, t), NEG_INF))
        m_ref[t] = _replicate(jnp.max(m_ref[t], axis=-1, keepdims=True))

    l_ref[...] = jnp.zeros_like(l_ref)
    acc_ref[...] = jnp.zeros_like(acc_ref)

    def sweep_pv(j, nb):
        vj = rows(vb_ref, j, nb)
        for t in range(2):
            p = jnp.concatenate([_exp2_rows(s_ref[t, j + b], m_ref[t]) for b in range(nb)], axis=-1)
            acc_ref[t] += jnp.dot(p.astype(BF16), vj, preferred_element_type=F32)
            l_ref[t] += _fold(jnp.add, p)

    def single(j, carry):
        sweep_pv(j, 1)
        return carry

    lax.fori_loop(0, i + 1, single, 0)
    outs = [acc_ref[t] * (1.0 / jnp.sum(l_ref[t], axis=-1, keepdims=True)) for t in range(2)]
    lam = _lambda(lq1_ref, lk1_ref, lq2_ref, lk2_ref, lam_init)
    o_ref[...] = _da_finish(outs[0], outs[1], lam, g_ref[...], lam_init)


def _attn_prompt(proj, n_streams, seq, lq1, lk1, lq2, lk2, da_g, lam_init):
    tq = _tile(seq, 512)
    nq = seq // tq
    dq = 2 * DA_HEAD_DIM

    def const(shape):
        return pl.BlockSpec(shape, lambda b, h, i: (0,) * len(shape))

    return pl.pallas_call(
        functools.partial(_attn_prompt_kernel, lam_init=lam_init, tq=tq),
        out_shape=jax.ShapeDtypeStruct((n_streams * seq, DA_WIDTH), BF16),
        grid=(n_streams, DA_HEADS, nq),
        in_specs=[
            pl.BlockSpec((None, tq, dq), lambda b, h, i: (3, b * nq + i, h)),
            pl.BlockSpec((None, seq, dq), lambda b, h, i: (4, b, h)),
            pl.BlockSpec((None, seq, DA_V_DIM), lambda b, h, i: (5, b, h)),
            const((1, DA_HEAD_DIM)), const((1, DA_HEAD_DIM)), const((1, DA_HEAD_DIM)), const((1, DA_HEAD_DIM)),
            const((1, DA_V_DIM)),
        ],
        out_specs=pl.BlockSpec((tq, DA_V_DIM), lambda b, h, i: (b * nq + i, h)),
        scratch_shapes=[
            pltpu.VMEM((seq, dq), BF16),
            pltpu.VMEM((seq, DA_V_DIM), BF16),
            pltpu.VMEM((2, nq, tq, tq), F32),
            pltpu.VMEM((2, tq, LANES), F32),
            pltpu.VMEM((2, tq, LANES), F32),
            pltpu.VMEM((2, tq, DA_V_DIM), F32),
        ],
        compiler_params=_params("parallel", "parallel", "arbitrary"),
        name="attn_prompt",
    )(proj, proj, proj, lq1, lk1, lq2, lk2, da_g)


def _attn_sample_kernel(q_ref, kn_ref, vn_ref, ck_ref, cv_ref, lq1_ref, lk1_ref, lq2_ref, lk2_ref, g_ref,
                        o_ref, kbuf, vbuf, sem, *, lam_init, li):
    hd = DA_HEAD_DIM
    s = pl.program_id(0)
    slot = s % 2

    def cache_copies(stream, slot_, h):
        return (pltpu.make_async_copy(ck_ref.at[li, stream, :, h, :], kbuf.at[slot_, h], sem.at[0, slot_, h]),
                pltpu.make_async_copy(cv_ref.at[li, stream, :, h, :], vbuf.at[slot_, h], sem.at[1, slot_, h]))

    def start_stream(stream, slot_):
        for h in range(DA_HEADS):
            for cp in cache_copies(stream, slot_, h):
                cp.start()

    @pl.when(s == 0)
    def _():
        start_stream(0, 0)

    @pl.when(s + 1 < pl.num_programs(0))
    def _():
        start_stream(s + 1, 1 - slot)

    lam = _lambda(lq1_ref, lk1_ref, lq2_ref, lk2_ref, lam_init)
    for h in range(DA_HEADS):
        cols = slice(h * DA_V_DIM, (h + 1) * DA_V_DIM)
        qs = (q_ref[:, cols] * Q_SCALE).astype(BF16)
        for cp in cache_copies(s, slot, h):
            cp.wait()
        kc, vc = kbuf[slot, h].astype(BF16), vbuf[slot, h].astype(BF16)
        kn, vn = kn_ref[:, cols].astype(BF16), vn_ref[:, cols].astype(BF16)
        outs = []
        for t in range(2):
            sl = slice(t * hd, (t + 1) * hd)
            sc = lax.dot_general(qs[:, sl], kc[:, sl], NT, preferred_element_type=F32)
            sn = lax.dot_general(qs[:, sl], kn[:, sl], NT, preferred_element_type=F32)
            m = jnp.maximum(jnp.max(_fold(jnp.maximum, sc), axis=-1, keepdims=True),
                            jnp.max(sn, axis=-1, keepdims=True))
            pc = _exp2_rows(sc, _replicate(m))
            pn = jnp.exp2(sn - m)
            l = jnp.sum(_fold(jnp.add, pc), axis=-1, keepdims=True) + jnp.sum(pn, axis=-1, keepdims=True)
            acc = (jnp.dot(pc.astype(BF16), vc, preferred_element_type=F32)
                   + jnp.dot(pn.astype(BF16), vn, preferred_element_type=F32))
            outs.append(acc * (1.0 / l))
        o_ref[:, cols] = _da_finish(outs[0], outs[1], lam, g_ref[...], lam_init)


def _attn_sample(proj, row0, n_streams, seq, cache_k, cache_v, li, lq1, lk1, lq2, lk2, da_g, lam_init):
    past = cache_k.shape[2]
    base = row0 // seq
    dq = 2 * DA_HEAD_DIM

    def const(shape):
        return pl.BlockSpec(shape, lambda s: (0,) * len(shape))

    def new(piece):
        return pl.BlockSpec((None, seq, PIECE), lambda s: (piece, base + s, 0))

    cache_spec = pl.BlockSpec(memory_space=pl.ANY)
    return pl.pallas_call(
        functools.partial(_attn_sample_kernel, lam_init=lam_init, li=li),
        out_shape=jax.ShapeDtypeStruct((n_streams * seq, DA_WIDTH), BF16),
        grid=(n_streams,),
        in_specs=[
            new(3), new(4), new(5), cache_spec, cache_spec,
            const((1, DA_HEAD_DIM)), const((1, DA_HEAD_DIM)), const((1, DA_HEAD_DIM)), const((1, DA_HEAD_DIM)),
            const((1, DA_V_DIM)),
        ],
        out_specs=pl.BlockSpec((seq, DA_WIDTH), lambda s: (s, 0)),
        scratch_shapes=[
            pltpu.VMEM((2, DA_HEADS, past, dq), F32),
            pltpu.VMEM((2, DA_HEADS, past, DA_V_DIM), F32),
            pltpu.SemaphoreType.DMA((2, 2, DA_HEADS)),
        ],
        compiler_params=_params("arbitrary"),
        name="attn_sample",
    )(proj, proj, proj, cache_k, cache_v, lq1, lk1, lq2, lk2, da_g)


def _outproj_kernel(x_ref, ml_ref, da_ref, w1_ref, w2_ref, post_ref, o_ref):
    y = (jnp.dot(ml_ref[...], w1_ref[...], preferred_element_type=F32)
         + jnp.dot(da_ref[...], w2_ref[...], preferred_element_type=F32))
    o_ref[...] = x_ref[...] + _rms(y) * post_ref[...]


def _outproj(x, ml_out, da_out, w_out, post_g, li):
    T, D = x.shape
    tm = _tile(T, 512)
    return pl.pallas_call(
        _outproj_kernel,
        out_shape=jax.ShapeDtypeStruct((T, D), F32),
        grid=(T // tm,),
        in_specs=[
            pl.BlockSpec((tm, D), lambda i: (i, 0)),
            pl.BlockSpec((tm, ML_WIDTH), lambda i: (i, 0)),
            pl.BlockSpec((tm, DA_WIDTH), lambda i: (i, 0)),
            pl.BlockSpec((None, ML_WIDTH, D), lambda i: (li, 0, 0)),
            pl.BlockSpec((None, DA_WIDTH, D), lambda i: (li, 1, 0)),
            pl.BlockSpec((1, D), lambda i: (0, 0)),
        ],
        out_specs=pl.BlockSpec((tm, D), lambda i: (i, 0)),
        compiler_params=_params("parallel"),
        name="outproj",
    )(x, ml_out, da_out, w_out, w_out, post_g)


def kernel(x_prompt, x_sample, cache_k, cache_v, state_C, state_n, state_m, state_conv,
           ffn1_pre_g, ffn1_wg, ffn1_wu, ffn1_wd, ffn1_post_g,
           mix_pre_g, w_in, b_i, b_f, conv_w, conv_b, ml_norm_g,
           lam_q1, lam_k1, lam_q2, lam_k2, da_norm_g, w_out, mix_post_g,
           ffn2_pre_g, ffn2_wg, ffn2_wu, ffn2_wd, ffn2_post_g):
    Bp, S, D = x_prompt.shape
    Bs, Ls, _ = x_sample.shape
    depth = w_in.shape[0]
    past = cache_k.shape[2]
    Tp, Ts = Bp * S, Bs * Ls
    assert Ls == CHUNK and past % LANES == 0 and S % CHUNK == 0 and Tp % Ls == 0
    Lp = 256 if S % 256 == 0 else CHUNK

    xp, xs = x_prompt.reshape(Tp, D), x_sample.reshape(Ts, D)
    zero_state = (jnp.zeros((1, Bp, ML_HEADS, ML_V_DIM, ML_QK_DIM), F32), jnp.zeros((1, Bp, ML_HEADS, ML_QK_DIM), F32),
                  jnp.zeros((1, Bp, 1, ML_HEADS), F32), jnp.zeros((1, Bp, CONV_W - 1, ML_CONV_CH), F32))
    stream_state = (state_C, state_n, state_m.reshape(depth, Bs, 1, ML_HEADS), state_conv)

    def row(v):
        return v.reshape(1, -1)

    w_main = jnp.concatenate([w_in[:, :, :GATE_OFF], w_in[:, :, GATE_OFF + N_GATES:]], axis=2).astype(BF16)
    w_gate = jnp.pad(w_in[:, :, GATE_OFF:GATE_OFF + N_GATES].astype(BF16), ((0, 0), (0, 0), (0, LANES - N_GATES)))
    w_o = w_out.astype(BF16)

    def kv_buffers(T):
        tm = _kv_tile(T)
        return (jnp.zeros((depth, T // tm, tm, DA_HEADS, DA_V_DIM), F32),) * 2

    kv_p, kv_s, st_p, st_s = kv_buffers(Tp), kv_buffers(Ts), [], []
    for li in range(depth):
        lam_init = 0.8 - 0.6 * math.exp(-0.3 * li)
        ffn1 = (row(ffn1_pre_g[li]), ffn1_wg, ffn1_wu, ffn1_wd, row(ffn1_post_g[li]), li)
        ffn2 = (row(ffn2_pre_g[li]), ffn2_wg, ffn2_wu, ffn2_wd, row(ffn2_post_g[li]), li)
        bias = jnp.concatenate([b_i[li], b_f[li]])
        bias_c = jnp.pad(bias, (0, LANES - N_GATES)).reshape(1, LANES)
        bias_r = bias.reshape(N_GATES, 1)
        ml_args = (bias_c, bias_r, conv_w[li], row(conv_b[li]), row(ml_norm_g[li]))
        lam_args = (row(lam_q1[li]), row(lam_k1[li]), row(lam_q2[li]), row(lam_k2[li]), row(da_norm_g[li]), lam_init)

        xp = _ffn(xp, *ffn1)
        xs = _ffn(xs, *ffn1)
        proj_p, gates_p, *kv_p = _inproj(xp, row(mix_pre_g[li]), w_main, w_gate, li, kv_p)
        proj_s, gates_s, *kv_s = _inproj(xs, row(mix_pre_g[li]), w_main, w_gate, li, kv_s)
        ml_p, *st = _mlstm(proj_p, gates_p, Bp, S, Lp, *ml_args, zero_state, 0)
        st_p.append(st)
        ml_s, *st = _mlstm(proj_s, gates_s, Bs, Ls, Ls, *ml_args, stream_state, li)
        st_s.append(st)
        da_p = _attn_prompt(proj_p, Bp, S, *lam_args)
        da_s = _attn_sample(proj_s, 0, Bs, Ls, cache_k, cache_v, li, *lam_args)
        xp = _ffn(_outproj(xp, ml_p, da_p, w_o, row(mix_post_g[li]), li), *ffn2)
        xs = _ffn(_outproj(xs, ml_s, da_s, w_o, row(mix_post_g[li]), li), *ffn2)

    def states(kv, st, n_streams, seq):
        c, n, m, conv = (jnp.stack(e) for e in zip(*st))
        return (kv[0].reshape(depth, n_streams, seq, DA_HEADS, 2 * DA_HEAD_DIM),
                kv[1].reshape(depth, n_streams, seq, DA_HEADS, DA_V_DIM),
                c, n, m.reshape(depth, n_streams, ML_HEADS), conv)

    return (xp.reshape(Bp, S, D), xs.reshape(Bs, Ls, D), *states(kv_p, st_p, Bp, S), *states(kv_s, st_s, Bs, Ls))
```

```python
import functools
import math

import jax
import jax.numpy as jnp
from jax import lax
from jax.experimental import pallas as pl
from jax.experimental.pallas import tpu as pltpu

F32 = jnp.float32
BF16 = jnp.bfloat16

CHUNK = 64
ML_HEADS = 4
ML_QK_DIM = 128
ML_V_DIM = 256
DA_HEADS = 4
DA_HEAD_DIM = 128
DA_V_DIM = 2 * DA_HEAD_DIM
CONV_W = 4
EPS = 1e-6
ML_WIDTH = ML_HEADS * ML_V_DIM
DA_WIDTH = DA_HEADS * DA_V_DIM
ML_CONV_CH = 2 * ML_HEADS * ML_QK_DIM
DA_QK_WIDTH = DA_HEADS * 2 * DA_HEAD_DIM
PIECE = 1024
N_PIECES = 6
N_GATES = 2 * ML_HEADS
GATE_OFF = 3 * PIECE
assert ML_CONV_CH == ML_WIDTH == DA_QK_WIDTH == DA_WIDTH == PIECE

LANES = 128
CONV_PAD = 8
VMEM_LIMIT = 56 * 1024 * 1024

NEG_INF = float("-inf")
NT = (((1,), (1,)), ((), ()))
TN = (((0,), (0,)), ((), ()))


def _params(*sem):
    return pltpu.CompilerParams(dimension_semantics=sem, vmem_limit_bytes=VMEM_LIMIT)


def _rms(xf):
    return xf * lax.rsqrt(jnp.mean(xf * xf, axis=-1, keepdims=True) + EPS)


def _silu(x):
    return x * jax.nn.sigmoid(x)


def _log_sigmoid(x):
    return jnp.minimum(x, 0.0) - jnp.log1p(jnp.exp(-jnp.abs(x)))


def _tile(n, pref):
    return pref if n % pref == 0 else n


def _ffn_kernel(x_ref, pre_ref, wg_ref, wu_ref, wd_ref, post_ref, o_ref, xn_ref, *, tail):
    j = pl.program_id(1)
    last = pl.num_programs(1) - 1

    def slab(xn, valid):
        g = jnp.dot(xn, wg_ref[...].astype(BF16), preferred_element_type=F32)
        u = jnp.dot(xn, wu_ref[...].astype(BF16), preferred_element_type=F32)
        h = _silu(g) * u
        wd = wd_ref[...]
        if valid:
            h = jnp.where(lax.broadcasted_iota(jnp.int32, h.shape, 1) < valid, h, 0.0)
            wd = jnp.where(lax.broadcasted_iota(jnp.int32, wd.shape, 0) < valid, wd, 0.0)
        return jnp.dot(h.astype(BF16), wd.astype(BF16), preferred_element_type=F32)

    @pl.when(j == 0)
    def _():
        xn = (_rms(x_ref[...]) * pre_ref[...]).astype(BF16)
        xn_ref[...] = xn
        o_ref[...] = slab(xn, 0)

    @pl.when((j > 0) & (j < last))
    def _():
        o_ref[...] += slab(xn_ref[...], 0)

    @pl.when(j == last)
    def _():
        y = o_ref[...] + slab(xn_ref[...], tail)
        o_ref[...] = x_ref[...] + 0.5 * (_rms(y) * post_ref[...])


def _ffn(x, pre_g, wg, wu, wd, post_g, li):
    T, D = x.shape
    F = wg.shape[2]
    tm = _tile(T, 1024)
    tf = 256 if F > 256 else F
    nf = pl.cdiv(F, tf)
    return pl.pallas_call(
        functools.partial(_ffn_kernel, tail=F % tf),
        out_shape=jax.ShapeDtypeStruct((T, D), F32),
        grid=(T // tm, nf),
        in_specs=[
            pl.BlockSpec((tm, D), lambda i, j: (i, 0)),
            pl.BlockSpec((1, D), lambda i, j: (0, 0)),
            pl.BlockSpec((None, D, tf), lambda i, j: (li, 0, j)),
            pl.BlockSpec((None, D, tf), lambda i, j: (li, 0, j)),
            pl.BlockSpec((None, tf, D), lambda i, j: (li, j, 0)),
            pl.BlockSpec((1, D), lambda i, j: (0, 0)),
        ],
        out_specs=pl.BlockSpec((tm, D), lambda i, j: (i, 0)),
        scratch_shapes=[pltpu.VMEM((tm, D), BF16)],
        compiler_params=_params("parallel", "arbitrary"),
        name="ffn",
    )(x, pre_g, wg, wu, wd, post_g)


K_PIECE, V_PIECE = 4, 5


def _inproj_kernel(x_ref, pre_ref, w_ref, wgate_ref, k5_in, v5_in, o_ref, gate_ref, k5_ref, v5_ref,
                   xn_ref, kv_ref, sem, *, li):
    del k5_in, v5_in
    i, j = pl.program_id(0), pl.program_id(1)
    final = (i == pl.num_programs(0) - 1) & (j == pl.num_programs(1) - 1)

    def head_copies(slot, dst_ref, tile):
        return [pltpu.make_async_copy(kv_ref.at[slot, :, pl.ds(h * DA_V_DIM, DA_V_DIM)],
                                      dst_ref.at[li, tile, :, h, :], sem.at[slot, h]) for h in range(DA_HEADS)]

    @pl.when(j == 0)
    def _():
        xn = (_rms(x_ref[...]) * pre_ref[...]).astype(BF16)
        xn_ref[...] = xn
        gate_ref[...] = jnp.dot(xn, wgate_ref[...], preferred_element_type=F32)

    @pl.when((j == 0) & (i > 0))
    def _():
        for cp in head_copies(1, v5_ref, i - 1):
            cp.wait()

    res = jnp.dot(xn_ref[...], w_ref[...], preferred_element_type=F32)
    o_ref[...] = res

    @pl.when(j == K_PIECE)
    def _():
        kv_ref[0] = res
        for cp in head_copies(0, k5_ref, i):
            cp.start()

    @pl.when(j == V_PIECE)
    def _():
        for cp in head_copies(0, k5_ref, i):
            cp.wait()
        kv_ref[1] = res
        for cp in head_copies(1, v5_ref, i):
            cp.start()

    @pl.when(final)
    def _():
        for cp in head_copies(1, v5_ref, i):
            cp.wait()


def _kv_tile(T):
    return _tile(T, 1024)


def _inproj(x, pre_g, w_main, w_gate, li, kv):
    T, D = x.shape
    tm = _kv_tile(T)
    assert V_PIECE == N_PIECES - 1 and K_PIECE == V_PIECE - 1
    kv_shape = jax.ShapeDtypeStruct(kv[0].shape, F32)
    any_spec = pl.BlockSpec(memory_space=pl.ANY)
    return pl.pallas_call(
        functools.partial(_inproj_kernel, li=li),
        out_shape=(jax.ShapeDtypeStruct((N_PIECES, T, PIECE), F32), jax.ShapeDtypeStruct((T, LANES), F32),
                   kv_shape, kv_shape),
        grid=(T // tm, N_PIECES),
        in_specs=[
            pl.BlockSpec((tm, D), lambda i, j: (i, 0)),
            pl.BlockSpec((1, D), lambda i, j: (0, 0)),
            pl.BlockSpec((None, D, PIECE), lambda i, j: (li, 0, j)),
            pl.BlockSpec((None, D, LANES), lambda i, j: (li, 0, 0)),
            any_spec, any_spec,
        ],
        out_specs=(pl.BlockSpec((None, tm, PIECE), lambda i, j: (j, i, 0)),
                   pl.BlockSpec((tm, LANES), lambda i, j: (i, 0)), any_spec, any_spec),
        scratch_shapes=[pltpu.VMEM((tm, D), BF16), pltpu.VMEM((2, tm, PIECE), F32),
                        pltpu.SemaphoreType.DMA((2, DA_HEADS))],
        input_output_aliases={4: 2, 5: 3},
        compiler_params=_params("arbitrary", "arbitrary"),
        name="inproj",
    )(x, pre_g, w_main, w_gate, *kv)


def _mlstm_kernel(qk_ref, v_ref, og_ref, gc_ref, gr_ref, bc_ref, br_ref, cw_ref, cb_ref, mlg_ref,
                  c0_ref, n0_ref, m0_ref, conv0_ref,
                  out_ref, cn_ref, nn_ref, mn_ref, convn_ref,
                  xp_ref, c_s, n_s, m_s, *, L):
    c = pl.program_id(1)

    @pl.when(c == 0)
    def _():
        c_s[...] = c0_ref[...]
        n_s[...] = n0_ref[...]
        m_s[...] = m0_ref[...]
        xp_ref[0:CONV_PAD, :] = jnp.zeros((CONV_PAD, ML_CONV_CH), F32)
        xp_ref[CONV_PAD - (CONV_W - 1):CONV_PAD, :] = conv0_ref[...]

    qk = qk_ref[...]
    xp_ref[CONV_PAD:CONV_PAD + L, :] = qk
    cw = cw_ref[...]
    conv = cb_ref[...] + qk * cw[CONV_W - 1:CONV_W, :]
    for t in range(CONV_W - 1):
        off = CONV_PAD - (CONV_W - 1) + t
        conv = conv + xp_ref[off:off + L, :] * cw[t:t + 1, :]
    tail = xp_ref[CONV_PAD + L - (CONV_W - 1):CONV_PAD + L, :]
    xp_ref[CONV_PAD - (CONV_W - 1):CONV_PAD, :] = tail
    act = _silu(conv)

    gcb = gc_ref[...] + bc_ref[...]
    grb = gr_ref[...] + br_ref[...]
    row = lax.broadcasted_iota(jnp.int32, (L, L), 0)
    col = lax.broadcasted_iota(jnp.int32, (L, L), 1)
    causal = col <= row
    b_c = jnp.dot(causal.astype(F32), _log_sigmoid(gcb), precision=lax.Precision.HIGHEST,
                  preferred_element_type=F32)
    b_r = jnp.dot(_log_sigmoid(grb), (row <= col).astype(F32), precision=lax.Precision.HIGHEST,
                  preferred_element_type=F32)

    v_all = v_ref[...]
    hs = []
    for h in range(ML_HEADS):
        q = act[:, h * ML_QK_DIM:(h + 1) * ML_QK_DIM]
        k = act[:, ML_HEADS * ML_QK_DIM + h * ML_QK_DIM:ML_HEADS * ML_QK_DIM + (h + 1) * ML_QK_DIM]
        k = k * (ML_QK_DIM ** -0.5)
        qb = q.astype(BF16)
        vb = v_all[:, h * ML_V_DIM:(h + 1) * ML_V_DIM].astype(BF16)
        ig_r = grb[h:h + 1, :]
        ig_c = gcb[:, h:h + 1]
        bh_r = b_r[ML_HEADS + h:ML_HEADS + h + 1, :]
        bh_c = b_c[:, ML_HEADS + h:ML_HEADS + h + 1]
        m_prev = m_s[:, h:h + 1]
        n_prev = n_s[h:h + 1, :]
        c_prev = c_s[h]

        d = jnp.where(causal, bh_c - bh_r + ig_r, NEG_INF)
        inter = bh_c + m_prev
        m_t = jnp.maximum(inter, jnp.max(d, axis=-1, keepdims=True))
        w_intra = jnp.exp(d - m_t)
        w_inter = jnp.exp(inter - m_t)
        s = lax.dot_general(qb, k.astype(BF16), NT, preferred_element_type=F32) * w_intra
        num = (jnp.dot(s.astype(BF16), vb, preferred_element_type=F32)
               + w_inter * lax.dot_general(qb, c_prev.astype(BF16), NT, preferred_element_type=F32))
        den = (jnp.sum(s, axis=-1, keepdims=True)
               + w_inter * jnp.sum(q * n_prev, axis=-1, keepdims=True))
        hs.append(num * (1.0 / jnp.maximum(jnp.abs(den), jnp.exp(-m_t))))

        m_new = m_t[L - 1:L, :]
        b_last = bh_c[L - 1:L, :]
        w_state = jnp.exp(b_last - bh_c + ig_c - m_new)
        decay = jnp.exp(b_last + m_prev - m_new)
        kw = k * w_state
        c_s[h] = decay * c_prev + lax.dot_general(vb, kw.astype(BF16), TN, preferred_element_type=F32)
        n_s[h:h + 1, :] = decay * n_prev + jnp.sum(kw, axis=0, keepdims=True)
        m_s[:, h:h + 1] = m_new

    gated = jax.nn.sigmoid(og_ref[...]) * jnp.concatenate(hs, axis=-1)
    out_ref[...] = (_rms(gated) * mlg_ref[...]).astype(BF16)

    @pl.when(c == pl.num_programs(1) - 1)
    def _():
        cn_ref[...] = c_s[...]
        nn_ref[...] = n_s[...]
        mn_ref[...] = m_s[...]
        convn_ref[...] = tail


def _mlstm(proj, gates, n_streams, seq, L, bias_c, bias_r, conv_w, conv_b, ml_g, state0, li0):
    nc = seq // L
    rows = n_streams * seq
    g_rows = gates[:, :N_GATES].reshape(n_streams * nc, L, N_GATES).transpose(0, 2, 1)

    def tok(piece):
        return pl.BlockSpec((None, L, PIECE), lambda s, c: (piece, s * nc + c, 0))

    def const(shape):
        return pl.BlockSpec(shape, lambda s, c: (0,) * len(shape))

    state_dims = ((ML_HEADS, ML_V_DIM, ML_QK_DIM), (ML_HEADS, ML_QK_DIM), (1, ML_HEADS), (CONV_W - 1, ML_CONV_CH))

    state_in = [pl.BlockSpec((None, None) + d, lambda s, c, n=len(d): (li0, s) + (0,) * n) for d in state_dims]
    state_out = [pl.BlockSpec((None,) + d, lambda s, c, n=len(d): (s,) + (0,) * n) for d in state_dims]
    return pl.pallas_call(
        functools.partial(_mlstm_kernel, L=L),
        out_shape=(jax.ShapeDtypeStruct((rows, ML_WIDTH), BF16),
                   *(jax.ShapeDtypeStruct((n_streams,) + d, F32) for d in state_dims)),
        grid=(n_streams, nc),
        in_specs=[
            tok(0), tok(1), tok(2),
            pl.BlockSpec((L, LANES), lambda s, c: (s * nc + c, 0)),
            pl.BlockSpec((None, N_GATES, L), lambda s, c: (s * nc + c, 0, 0)),
            const((1, LANES)), const((N_GATES, 1)), const((CONV_W, ML_CONV_CH)), const((1, ML_CONV_CH)),
            const((1, ML_WIDTH)),
            *state_in,
        ],
        out_specs=(pl.BlockSpec((L, ML_WIDTH), lambda s, c: (s * nc + c, 0)), *state_out),
        scratch_shapes=[
            pltpu.VMEM((CONV_PAD + L, ML_CONV_CH), F32),
            pltpu.VMEM((ML_HEADS, ML_V_DIM, ML_QK_DIM), F32),
            pltpu.VMEM((ML_HEADS, ML_QK_DIM), F32),
            pltpu.VMEM((1, ML_HEADS), F32),
        ],
        compiler_params=_params("arbitrary", "arbitrary"),
        name="mlstm",
    )(proj, proj, proj, gates, g_rows, bias_c, bias_r, conv_w, conv_b, ml_g, *state0)


def _lambda(lq1_ref, lk1_ref, lq2_ref, lk2_ref, lam_init):
    return (jnp.exp(jnp.sum(lq1_ref[...] * lk1_ref[...], axis=-1, keepdims=True))
            - jnp.exp(jnp.sum(lq2_ref[...] * lk2_ref[...], axis=-1, keepdims=True)) + lam_init)


def _lane_tiles(x):
    return [x[:, c * LANES:(c + 1) * LANES] for c in range(x.shape[1] // LANES)]


def _fold(op, x):
    return functools.reduce(op, _lane_tiles(x))


def _exp2_rows(s, m_b):
    return jnp.concatenate([jnp.exp2(t - m_b) for t in _lane_tiles(s)], axis=-1)


def _replicate(col):
    return jnp.broadcast_to(col, (col.shape[0], LANES))


def _da_finish(out1, out2, lam, g, lam_init):
    att = out1 - lam * out2
    return ((_rms(att) * g) * (1.0 - lam_init)).astype(BF16)


LOG2E = math.log2(math.e)
Q_SCALE = DA_HEAD_DIM ** -0.5 * LOG2E


def _attn_prompt_kernel(q_ref, k_ref, v_ref, lq1_ref, lk1_ref, lq2_ref, lk2_ref, g_ref, o_ref,
                        kb_ref, vb_ref, s_ref, m_ref, l_ref, acc_ref, *, lam_init, tq):
    i = pl.program_id(2)
    hd = DA_HEAD_DIM

    @pl.when(i == 0)
    def _():
        kb_ref[...] = k_ref[...].astype(BF16)
        vb_ref[...] = v_ref[...].astype(BF16)

    qs = (q_ref[...] * Q_SCALE).astype(BF16)

    def rows(ref, j, nb):
        return ref[pl.ds(pl.multiple_of(j * tq, tq), nb * tq), :]

    def scores(kj, t):
        return lax.dot_general(qs[:, t * hd:(t + 1) * hd], kj[:, t * hd:(t + 1) * hd], NT,
                               preferred_element_type=F32)

    def note_scores(j, nb, t, s):
        for b in range(nb):
            s_ref[t, j + b] = s[:, b * tq:(b + 1) * tq]
        m_ref[t] = jnp.maximum(m_ref[t], _fold(jnp.maximum, s))

    def for_blocks(n, fn):
        def pair(k, carry):
            fn(2 * k, 2)
            return carry

        lax.fori_loop(0, n // 2, pair, 0)
        pl.when(n % 2 == 1)(lambda: fn(n - 1, 1))

    m_ref[...] = jnp.full_like(m_ref, NEG_INF)

    def sweep_max(j, nb):
        kj = rows(kb_ref, j, nb)
        for t in range(2):
            note_scores(j, nb, t, scores(kj, t))

    for_blocks(i, sweep_max)
    diag = (lax.broadcasted_iota(jnp.int32, (tq, tq), 0) // CHUNK
            >= lax.broadcasted_iota(jnp.int32, (tq, tq), 1) // CHUNK)
    kd = rows(kb_ref, i, 1)
    for t in range(2):
        note_scores(i, 1, t, jnp.where(diag, scores(kd, t), NEG_INF))
        m_ref[t] = _replicate(jnp.max(m_ref[t], axis=-1, keepdims=True))

    l_ref[...] = jnp.zeros_like(l_ref)
    acc_ref[...] = jnp.zeros_like(acc_ref)

    def sweep_pv(j, nb):
        vj = rows(vb_ref, j, nb)
        for t in range(2):
            p = jnp.concatenate([_exp2_rows(s_ref[t, j + b], m_ref[t]) for b in range(nb)], axis=-1)
            acc_ref[t] += jnp.dot(p.astype(BF16), vj, preferred_element_type=F32)
            l_ref[t] += _fold(jnp.add, p)

    def single(j, carry):
        sweep_pv(j, 1)
        return carry

    lax.fori_loop(0, i + 1, single, 0)
    outs = [acc_ref[t] * (1.0 / jnp.sum(l_ref[t], axis=-1, keepdims=True)) for t in range(2)]
    lam = _lambda(lq1_ref, lk1_ref, lq2_ref, lk2_ref, lam_init)
    o_ref[...] = _da_finish(outs[0], outs[1], lam, g_ref[...], lam_init)


def _attn_prompt(proj, n_streams, seq, lq1, lk1, lq2, lk2, da_g, lam_init):
    tq = _tile(seq, 512)
    nq = seq // tq
    dq = 2 * DA_HEAD_DIM

    def const(shape):
        return pl.BlockSpec(shape, lambda b, h, i: (0,) * len(shape))

    return pl.pallas_call(
        functools.partial(_attn_prompt_kernel, lam_init=lam_init, tq=tq),
        out_shape=jax.ShapeDtypeStruct((n_streams * seq, DA_WIDTH), BF16),
        grid=(n_streams, DA_HEADS, nq),
        in_specs=[
            pl.BlockSpec((None, tq, dq), lambda b, h, i: (3, b * nq + i, h)),
            pl.BlockSpec((None, seq, dq), lambda b, h, i: (4, b, h)),
            pl.BlockSpec((None, seq, DA_V_DIM), lambda b, h, i: (5, b, h)),
            const((1, DA_HEAD_DIM)), const((1, DA_HEAD_DIM)), const((1, DA_HEAD_DIM)), const((1, DA_HEAD_DIM)),
            const((1, DA_V_DIM)),
        ],
        out_specs=pl.BlockSpec((tq, DA_V_DIM), lambda b, h, i: (b * nq + i, h)),
        scratch_shapes=[
            pltpu.VMEM((seq, dq), BF16),
            pltpu.VMEM((seq, DA_V_DIM), BF16),
            pltpu.VMEM((2, nq, tq, tq), F32),
            pltpu.VMEM((2, tq, LANES), F32),
            pltpu.VMEM((2, tq, LANES), F32),
            pltpu.VMEM((2, tq, DA_V_DIM), F32),
        ],
        compiler_params=_params("parallel", "parallel", "arbitrary"),
        name="attn_prompt",
    )(proj, proj, proj, lq1, lk1, lq2, lk2, da_g)


def _attn_sample_kernel(q_ref, kn_ref, vn_ref, ck_ref, cv_ref, lq1_ref, lk1_ref, lq2_ref, lk2_ref, g_ref,
                        o_ref, kbuf, vbuf, sem, *, lam_init, li):
    hd = DA_HEAD_DIM
    s = pl.program_id(0)
    slot = s % 2

    def cache_copies(stream, slot_, h):
        return (pltpu.make_async_copy(ck_ref.at[li, stream, :, h, :], kbuf.at[slot_, h], sem.at[0, slot_, h]),
                pltpu.make_async_copy(cv_ref.at[li, stream, :, h, :], vbuf.at[slot_, h], sem.at[1, slot_, h]))

    def start_stream(stream, slot_):
        for h in range(DA_HEADS):
            for cp in cache_copies(stream, slot_, h):
                cp.start()

    @pl.when(s == 0)
    def _():
        start_stream(0, 0)

    @pl.when(s + 1 < pl.num_programs(0))
    def _():
        start_stream(s + 1, 1 - slot)

    for h in range(DA_HEADS):
        for cp in cache_copies(s, slot, h):
            cp.wait()

    rows = q_ref.shape[0]
    lam = _lambda(lq1_ref, lk1_ref, lq2_ref, lk2_ref, lam_init)
    for h in range(DA_HEADS):
        cols = slice(h * DA_V_DIM, (h + 1) * DA_V_DIM)
        qs = (q_ref[:, cols] * Q_SCALE).astype(BF16)
        kc, vc = kbuf[slot, h].astype(BF16), vbuf[slot, h].astype(BF16)
        kn, vn = kn_ref[:, cols].astype(BF16), vn_ref[:, cols].astype(BF16)
        pcs, pns, ls = [], [], []
        for t in range(2):
            sl = slice(t * hd, (t + 1) * hd)
            sc = lax.dot_general(qs[:, sl], kc[:, sl], NT, preferred_element_type=F32)
            sn = lax.dot_general(qs[:, sl], kn[:, sl], NT, preferred_element_type=F32)
            m = jnp.maximum(jnp.max(_fold(jnp.maximum, sc), axis=-1, keepdims=True),
                            jnp.max(sn, axis=-1, keepdims=True))
            pc = _exp2_rows(sc, _replicate(m))
            pn = jnp.exp2(sn - m)
            ls.append(jnp.sum(_fold(jnp.add, pc), axis=-1, keepdims=True) + jnp.sum(pn, axis=-1, keepdims=True))
            pcs.append(pc.astype(BF16))
            pns.append(pn.astype(BF16))
        acc = (jnp.dot(jnp.concatenate(pcs, axis=0), vc, preferred_element_type=F32)
               + jnp.dot(jnp.concatenate(pns, axis=0), vn, preferred_element_type=F32))
        outs = [acc[t * rows:(t + 1) * rows] * (1.0 / ls[t]) for t in range(2)]
        o_ref[:, cols] = _da_finish(outs[0], outs[1], lam, g_ref[...], lam_init)


def _attn_sample(proj, row0, n_streams, seq, cache_k, cache_v, li, lq1, lk1, lq2, lk2, da_g, lam_init):
    past = cache_k.shape[2]
    base = row0 // seq
    dq = 2 * DA_HEAD_DIM

    def const(shape):
        return pl.BlockSpec(shape, lambda s: (0,) * len(shape))

    def new(piece):
        return pl.BlockSpec((None, seq, PIECE), lambda s: (piece, base + s, 0))

    cache_spec = pl.BlockSpec(memory_space=pl.ANY)
    return pl.pallas_call(
        functools.partial(_attn_sample_kernel, lam_init=lam_init, li=li),
        out_shape=jax.ShapeDtypeStruct((n_streams * seq, DA_WIDTH), BF16),
        grid=(n_streams,),
        in_specs=[
            new(3), new(4), new(5), cache_spec, cache_spec,
            const((1, DA_HEAD_DIM)), const((1, DA_HEAD_DIM)), const((1, DA_HEAD_DIM)), const((1, DA_HEAD_DIM)),
            const((1, DA_V_DIM)),
        ],
        out_specs=pl.BlockSpec((seq, DA_WIDTH), lambda s: (s, 0)),
        scratch_shapes=[
            pltpu.VMEM((2, DA_HEADS, past, dq), F32),
            pltpu.VMEM((2, DA_HEADS, past, DA_V_DIM), F32),
            pltpu.SemaphoreType.DMA((2, 2, DA_HEADS)),
        ],
        compiler_params=_params("arbitrary"),
        name="attn_sample",
    )(proj, proj, proj, cache_k, cache_v, lq1, lk1, lq2, lk2, da_g)


def _outproj_kernel(x_ref, ml_ref, da_ref, w1_ref, w2_ref, post_ref, o_ref):
    y = (jnp.dot(ml_ref[...], w1_ref[...], preferred_element_type=F32)
         + jnp.dot(da_ref[...], w2_ref[...], preferred_element_type=F32))
    o_ref[...] = x_ref[...] + _rms(y) * post_ref[...]


def _outproj(x, ml_out, da_out, w_out, post_g, li):
    T, D = x.shape
    tm = _tile(T, 512)
    return pl.pallas_call(
        _outproj_kernel,
        out_shape=jax.ShapeDtypeStruct((T, D), F32),
        grid=(T // tm,),
        in_specs=[
            pl.BlockSpec((tm, D), lambda i: (i, 0)),
            pl.BlockSpec((tm, ML_WIDTH), lambda i: (i, 0)),
            pl.BlockSpec((tm, DA_WIDTH), lambda i: (i, 0)),
            pl.BlockSpec((None, ML_WIDTH, D), lambda i: (li, 0, 0)),
            pl.BlockSpec((None, DA_WIDTH, D), lambda i: (li, 1, 0)),
            pl.BlockSpec((1, D), lambda i: (0, 0)),
        ],
        out_specs=pl.BlockSpec((tm, D), lambda i: (i, 0)),
        compiler_params=_params("parallel"),
        name="outproj",
    )(x, ml_out, da_out, w_out, w_out, post_g)


def kernel(x_prompt, x_sample, cache_k, cache_v, state_C, state_n, state_m, state_conv,
           ffn1_pre_g, ffn1_wg, ffn1_wu, ffn1_wd, ffn1_post_g,
           mix_pre_g, w_in, b_i, b_f, conv_w, conv_b, ml_norm_g,
           lam_q1, lam_k1, lam_q2, lam_k2, da_norm_g, w_out, mix_post_g,
           ffn2_pre_g, ffn2_wg, ffn2_wu, ffn2_wd, ffn2_post_g):
    Bp, S, D = x_prompt.shape
    Bs, Ls, _ = x_sample.shape
    depth = w_in.shape[0]
    past = cache_k.shape[2]
    Tp, Ts = Bp * S, Bs * Ls
    assert Ls == CHUNK and past % LANES == 0 and S % CHUNK == 0 and Tp % Ls == 0
    Lp = 256 if S % 256 == 0 else CHUNK

    xp, xs = x_prompt.reshape(Tp, D), x_sample.reshape(Ts, D)
    zero_state = (jnp.zeros((1, Bp, ML_HEADS, ML_V_DIM, ML_QK_DIM), F32), jnp.zeros((1, Bp, ML_HEADS, ML_QK_DIM), F32),
                  jnp.zeros((1, Bp, 1, ML_HEADS), F32), jnp.zeros((1, Bp, CONV_W - 1, ML_CONV_CH), F32))
    stream_state = (state_C, state_n, state_m.reshape(depth, Bs, 1, ML_HEADS), state_conv)

    def row(v):
        return v.reshape(1, -1)

    w_main = jnp.concatenate([w_in[:, :, :GATE_OFF], w_in[:, :, GATE_OFF + N_GATES:]], axis=2).astype(BF16)
    w_gate = jnp.pad(w_in[:, :, GATE_OFF:GATE_OFF + N_GATES].astype(BF16), ((0, 0), (0, 0), (0, LANES - N_GATES)))
    w_o = w_out.astype(BF16)

    def kv_buffers(T):
        tm = _kv_tile(T)
        return (jnp.zeros((depth, T // tm, tm, DA_HEADS, DA_V_DIM), F32),) * 2

    kv_p, kv_s, st_p, st_s = kv_buffers(Tp), kv_buffers(Ts), [], []
    for li in range(depth):
        lam_init = 0.8 - 0.6 * math.exp(-0.3 * li)
        ffn1 = (row(ffn1_pre_g[li]), ffn1_wg, ffn1_wu, ffn1_wd, row(ffn1_post_g[li]), li)
        ffn2 = (row(ffn2_pre_g[li]), ffn2_wg, ffn2_wu, ffn2_wd, row(ffn2_post_g[li]), li)
        bias = jnp.concatenate([b_i[li], b_f[li]])
        bias_c = jnp.pad(bias, (0, LANES - N_GATES)).reshape(1, LANES)
        bias_r = bias.reshape(N_GATES, 1)
        ml_args = (bias_c, bias_r, conv_w[li], row(conv_b[li]), row(ml_norm_g[li]))
        lam_args = (row(lam_q1[li]), row(lam_k1[li]), row(lam_q2[li]), row(lam_k2[li]), row(da_norm_g[li]), lam_init)

        xp = _ffn(xp, *ffn1)
        xs = _ffn(xs, *ffn1)
        proj_p, gates_p, *kv_p = _inproj(xp, row(mix_pre_g[li]), w_main, w_gate, li, kv_p)
        proj_s, gates_s, *kv_s = _inproj(xs, row(mix_pre_g[li]), w_main, w_gate, li, kv_s)
        ml_p, *st = _mlstm(proj_p, gates_p, Bp, S, Lp, *ml_args, zero_state, 0)
        st_p.append(st)
        ml_s, *st = _mlstm(proj_s, gates_s, Bs, Ls, Ls, *ml_args, stream_state, li)
        st_s.append(st)
        da_p = _attn_prompt(proj_p, Bp, S, *lam_args)
        da_s = _attn_sample(proj_s, 0, Bs, Ls, cache_k, cache_v, li, *lam_args)
        xp = _ffn(_outproj(xp, ml_p, da_p, w_o, row(mix_post_g[li]), li), *ffn2)
        xs = _ffn(_outproj(xs, ml_s, da_s, w_o, row(mix_post_g[li]), li), *ffn2)

    def states(kv, st, n_streams, seq):
        c, n, m, conv = (jnp.stack(e) for e in zip(*st))
        return (kv[0].reshape(depth, n_streams, seq, DA_HEADS, 2 * DA_HEAD_DIM),
                kv[1].reshape(depth, n_streams, seq, DA_HEADS, DA_V_DIM),
                c, n, m.reshape(depth, n_streams, ML_HEADS), conv)

    return (xp.reshape(Bp, S, D), xs.reshape(Bs, Ls, D), *states(kv_p, st_p, Bp, S), *states(kv_s, st_s, Bs, Ls))
```

```python
import functools
import math

import jax
import jax.numpy as jnp
from jax import lax
from jax.experimental import pallas as pl
from jax.experimental.pallas import tpu as pltpu

F32 = jnp.float32
BF16 = jnp.bfloat16

CHUNK = 64
ML_HEADS = 4
ML_QK_DIM = 128
ML_V_DIM = 256
DA_HEADS = 4
DA_HEAD_DIM = 128
DA_V_DIM = 2 * DA_HEAD_DIM
CONV_W = 4
EPS = 1e-6
ML_WIDTH = ML_HEADS * ML_V_DIM
DA_WIDTH = DA_HEADS * DA_V_DIM
ML_CONV_CH = 2 * ML_HEADS * ML_QK_DIM
DA_QK_WIDTH = DA_HEADS * 2 * DA_HEAD_DIM
PIECE = 1024
N_PIECES = 6
N_GATES = 2 * ML_HEADS
GATE_OFF = 3 * PIECE
assert ML_CONV_CH == ML_WIDTH == DA_QK_WIDTH == DA_WIDTH == PIECE

LANES = 128
CONV_PAD = 8
VMEM_LIMIT = 56 * 1024 * 1024

NEG_INF = float("-inf")
NT = (((1,), (1,)), ((), ()))
TN = (((0,), (0,)), ((), ()))


def _params(*sem):
    return pltpu.CompilerParams(dimension_semantics=sem, vmem_limit_bytes=VMEM_LIMIT)


def _rms(xf):
    return xf * lax.rsqrt(jnp.mean(xf * xf, axis=-1, keepdims=True) + EPS)


def _silu(x):
    return x * jax.nn.sigmoid(x)


def _log_sigmoid(x):
    return jnp.minimum(x, 0.0) - jnp.log1p(jnp.exp(-jnp.abs(x)))


def _tile(n, pref):
    return pref if n % pref == 0 else n


def _ffn_kernel(x_ref, pre_ref, wg_ref, wu_ref, wd_ref, post_ref, o_ref, xn_ref, *, tail):
    j = pl.program_id(1)
    last = pl.num_programs(1) - 1

    def slab(xn, valid):
        g = jnp.dot(xn, wg_ref[...].astype(BF16), preferred_element_type=F32)
        u = jnp.dot(xn, wu_ref[...].astype(BF16), preferred_element_type=F32)
        h = _silu(g) * u
        wd = wd_ref[...]
        if valid:
            h = jnp.where(lax.broadcasted_iota(jnp.int32, h.shape, 1) < valid, h, 0.0)
            wd = jnp.where(lax.broadcasted_iota(jnp.int32, wd.shape, 0) < valid, wd, 0.0)
        return jnp.dot(h.astype(BF16), wd.astype(BF16), preferred_element_type=F32)

    @pl.when(j == 0)
    def _():
        xn = (_rms(x_ref[...]) * pre_ref[...]).astype(BF16)
        xn_ref[...] = xn
        o_ref[...] = slab(xn, 0)

    @pl.when((j > 0) & (j < last))
    def _():
        o_ref[...] += slab(xn_ref[...], 0)

    @pl.when(j == last)
    def _():
        y = o_ref[...] + slab(xn_ref[...], tail)
        o_ref[...] = x_ref[...] + 0.5 * (_rms(y) * post_ref[...])


def _ffn(x, pre_g, wg, wu, wd, post_g, li):
    T, D = x.shape
    F = wg.shape[2]
    tm = _tile(T, 1024)
    tf = 256 if F > 256 else F
    nf = pl.cdiv(F, tf)
    return pl.pallas_call(
        functools.partial(_ffn_kernel, tail=F % tf),
        out_shape=jax.ShapeDtypeStruct((T, D), F32),
        grid=(T // tm, nf),
        in_specs=[
            pl.BlockSpec((tm, D), lambda i, j: (i, 0)),
            pl.BlockSpec((1, D), lambda i, j: (0, 0)),
            pl.BlockSpec((None, D, tf), lambda i, j: (li, 0, j)),
            pl.BlockSpec((None, D, tf), lambda i, j: (li, 0, j)),
            pl.BlockSpec((None, tf, D), lambda i, j: (li, j, 0)),
            pl.BlockSpec((1, D), lambda i, j: (0, 0)),
        ],
        out_specs=pl.BlockSpec((tm, D), lambda i, j: (i, 0)),
        scratch_shapes=[pltpu.VMEM((tm, D), BF16)],
        compiler_params=_params("parallel", "arbitrary"),
        name="ffn",
    )(x, pre_g, wg, wu, wd, post_g)


K_PIECE, V_PIECE = 4, 5


def _inproj_kernel(*refs, li, depth):
    x_ref, pre_ref, w_ref, wgate_ref = refs[:4]
    o_ref, gate_ref, k5_ref, v5_ref, xn_ref, kv_ref, sem, *zero_scratch = refs[4 + (2 if li else 0):]
    i, j = pl.program_id(0), pl.program_id(1)
    final = (i == pl.num_programs(0) - 1) & (j == pl.num_programs(1) - 1)

    def head_copies(slot, dst_ref, tile):
        return [pltpu.make_async_copy(kv_ref.at[slot, :, pl.ds(h * DA_V_DIM, DA_V_DIM)],
                                      dst_ref.at[li, tile, :, h, :], sem.at[slot, h]) for h in range(DA_HEADS)]

    @pl.when(j == 0)
    def _():
        xn = (_rms(x_ref[...]) * pre_ref[...]).astype(BF16)
        xn_ref[...] = xn
        gate_ref[...] = jnp.dot(xn, wgate_ref[...], preferred_element_type=F32)
        o_ref[...] = jnp.dot(xn, w_ref[...], preferred_element_type=F32)

    @pl.when(j > 0)
    def _():
        o_ref[...] = jnp.dot(xn_ref[...], w_ref[...], preferred_element_type=F32)

    @pl.when((j == 0) & (i > 0))
    def _():
        for cp in head_copies(1, v5_ref, i - 1):
            cp.wait()

    @pl.when(j == K_PIECE)
    def _():
        kv_ref[0] = o_ref[...]
        for cp in head_copies(0, k5_ref, i):
            cp.start()

    @pl.when(j == V_PIECE)
    def _():
        for cp in head_copies(0, k5_ref, i):
            cp.wait()
        kv_ref[1] = o_ref[...]
        for cp in head_copies(1, v5_ref, i):
            cp.start()

    @pl.when(final)
    def _():
        for cp in head_copies(1, v5_ref, i):
            cp.wait()

    if zero_scratch:
        zero_ref, zero_sem = zero_scratch

        def zero_copies(tile):
            return [pltpu.make_async_copy(zero_ref, dst.at[l, tile, :, h, :], zero_sem.at[n, l - 1, h])
                    for n, dst in enumerate((k5_ref, v5_ref)) for l in range(1, depth) for h in range(DA_HEADS)]

        @pl.when((i == 0) & (j == 0))
        def _():
            zero_ref[...] = jnp.zeros_like(zero_ref)

        @pl.when(j == 1)
        def _():
            for cp in zero_copies(i):
                cp.start()

        @pl.when(j == 2)
        def _():
            for cp in zero_copies(i):
                cp.wait()


def _kv_tile(T):
    return _tile(T, 1024)


def _inproj(x, pre_g, w_main, w_gate, li, depth, kv):
    T, D = x.shape
    tm = _kv_tile(T)
    assert V_PIECE == N_PIECES - 1 and K_PIECE == V_PIECE - 1 and K_PIECE > 2
    kv_shape = jax.ShapeDtypeStruct((depth, T // tm, tm, DA_HEADS, DA_V_DIM), F32)
    any_spec = pl.BlockSpec(memory_space=pl.ANY)
    zero_fill = li == 0 and depth > 1
    assert len(kv) == (2 if li else 0)
    return pl.pallas_call(
        functools.partial(_inproj_kernel, li=li, depth=depth),
        out_shape=(jax.ShapeDtypeStruct((N_PIECES, T, PIECE), F32), jax.ShapeDtypeStruct((T, LANES), F32),
                   kv_shape, kv_shape),
        grid=(T // tm, N_PIECES),
        in_specs=[
            pl.BlockSpec((tm, D), lambda i, j: (i, 0)),
            pl.BlockSpec((1, D), lambda i, j: (0, 0)),
            pl.BlockSpec((None, D, PIECE), lambda i, j: (li, 0, j)),
            pl.BlockSpec((None, D, LANES), lambda i, j: (li, 0, 0)),
        ] + [any_spec] * len(kv),
        out_specs=(pl.BlockSpec((None, tm, PIECE), lambda i, j: (j, i, 0)),
                   pl.BlockSpec((tm, LANES), lambda i, j: (i, 0)), any_spec, any_spec),
        scratch_shapes=[pltpu.VMEM((tm, D), BF16), pltpu.VMEM((2, tm, PIECE), F32),
                        pltpu.SemaphoreType.DMA((2, DA_HEADS))]
        + ([pltpu.VMEM((tm, DA_V_DIM), F32), pltpu.SemaphoreType.DMA((2, depth - 1, DA_HEADS))] if zero_fill else []),
        input_output_aliases={4: 2, 5: 3} if li else {},
        compiler_params=_params("arbitrary", "arbitrary"),
        name="inproj",
    )(x, pre_g, w_main, w_gate, *kv)


def _mlstm_kernel(qk_ref, v_ref, og_ref, gc_ref, gr_ref, bc_ref, br_ref, cw_ref, cb_ref, mlg_ref,
                  c0_ref, n0_ref, m0_ref, conv0_ref,
                  out_ref, cn_ref, nn_ref, mn_ref, convn_ref,
                  xp_ref, c_s, n_s, m_s, *, L):
    c = pl.program_id(1)

    @pl.when(c == 0)
    def _():
        c_s[...] = c0_ref[...]
        n_s[...] = n0_ref[...]
        m_s[...] = m0_ref[...]
        xp_ref[0:CONV_PAD, :] = jnp.zeros((CONV_PAD, ML_CONV_CH), F32)
        xp_ref[CONV_PAD - (CONV_W - 1):CONV_PAD, :] = conv0_ref[...]

    qk = qk_ref[...]
    xp_ref[CONV_PAD:CONV_PAD + L, :] = qk
    cw = cw_ref[...]
    conv = cb_ref[...] + qk * cw[CONV_W - 1:CONV_W, :]
    for t in range(CONV_W - 1):
        off = CONV_PAD - (CONV_W - 1) + t
        conv = conv + xp_ref[off:off + L, :] * cw[t:t + 1, :]
    tail = xp_ref[CONV_PAD + L - (CONV_W - 1):CONV_PAD + L, :]
    xp_ref[CONV_PAD - (CONV_W - 1):CONV_PAD, :] = tail
    act = _silu(conv)

    gcb = gc_ref[...] + bc_ref[...]
    grb = gr_ref[...] + br_ref[...]
    row = lax.broadcasted_iota(jnp.int32, (L, L), 0)
    col = lax.broadcasted_iota(jnp.int32, (L, L), 1)
    causal = col <= row
    b_c = jnp.dot(causal.astype(F32), _log_sigmoid(gcb), precision=lax.Precision.HIGHEST,
                  preferred_element_type=F32)
    b_r = jnp.dot(_log_sigmoid(grb), (row <= col).astype(F32), precision=lax.Precision.HIGHEST,
                  preferred_element_type=F32)

    v_all = v_ref[...]
    hs = []
    for h in range(ML_HEADS):
        q = act[:, h * ML_QK_DIM:(h + 1) * ML_QK_DIM]
        k = act[:, ML_HEADS * ML_QK_DIM + h * ML_QK_DIM:ML_HEADS * ML_QK_DIM + (h + 1) * ML_QK_DIM]
        k = k * (ML_QK_DIM ** -0.5)
        qb = q.astype(BF16)
        vb = v_all[:, h * ML_V_DIM:(h + 1) * ML_V_DIM].astype(BF16)
        ig_r = grb[h:h + 1, :]
        ig_c = gcb[:, h:h + 1]
        bh_r = b_r[ML_HEADS + h:ML_HEADS + h + 1, :]
        bh_c = b_c[:, ML_HEADS + h:ML_HEADS + h + 1]
        m_prev = m_s[:, h:h + 1]
        n_prev = n_s[h:h + 1, :]
        c_prev = c_s[h]

        d = jnp.where(causal, bh_c - bh_r + ig_r, NEG_INF)
        inter = bh_c + m_prev
        m_t = jnp.maximum(inter, jnp.max(d, axis=-1, keepdims=True))
        w_intra = jnp.exp(d - m_t)
        w_inter = jnp.exp(inter - m_t)
        s = lax.dot_general(qb, k.astype(BF16), NT, preferred_element_type=F32) * w_intra
        num = (jnp.dot(s.astype(BF16), vb, preferred_element_type=F32)
               + w_inter * lax.dot_general(qb, c_prev.astype(BF16), NT, preferred_element_type=F32))
        den = (jnp.sum(s, axis=-1, keepdims=True)
               + w_inter * jnp.sum(q * n_prev, axis=-1, keepdims=True))
        hs.append(num * (1.0 / jnp.maximum(jnp.abs(den), jnp.exp(-m_t))))

        m_new = m_t[L - 1:L, :]
        b_last = bh_c[L - 1:L, :]
        w_state = jnp.exp(b_last - bh_c + ig_c - m_new)
        decay = jnp.exp(b_last + m_prev - m_new)
        kw = k * w_state
        c_s[h] = decay * c_prev + lax.dot_general(vb, kw.astype(BF16), TN, preferred_element_type=F32)
        n_s[h:h + 1, :] = decay * n_prev + jnp.sum(kw, axis=0, keepdims=True)
        m_s[:, h:h + 1] = m_new

    gated = jax.nn.sigmoid(og_ref[...]) * jnp.concatenate(hs, axis=-1)
    out_ref[...] = (_rms(gated) * mlg_ref[...]).astype(BF16)

    @pl.when(c == pl.num_programs(1) - 1)
    def _():
        cn_ref[...] = c_s[...]
        nn_ref[...] = n_s[...]
        mn_ref[...] = m_s[...]
        convn_ref[...] = tail


def _mlstm(proj, gates, n_streams, seq, L, bias_c, bias_r, conv_w, conv_b, ml_g, state0, li0):
    nc = seq // L
    rows = n_streams * seq
    g_rows = gates[:, :N_GATES].reshape(n_streams * nc, L, N_GATES).transpose(0, 2, 1)

    def tok(piece):
        return pl.BlockSpec((None, L, PIECE), lambda s, c: (piece, s * nc + c, 0))

    def const(shape):
        return pl.BlockSpec(shape, lambda s, c: (0,) * len(shape))

    state_dims = ((ML_HEADS, ML_V_DIM, ML_QK_DIM), (ML_HEADS, ML_QK_DIM), (1, ML_HEADS), (CONV_W - 1, ML_CONV_CH))

    state_in = [pl.BlockSpec((None, None) + d, lambda s, c, n=len(d): (li0, s) + (0,) * n) for d in state_dims]
    state_out = [pl.BlockSpec((None,) + d, lambda s, c, n=len(d): (s,) + (0,) * n) for d in state_dims]
    return pl.pallas_call(
        functools.partial(_mlstm_kernel, L=L),
        out_shape=(jax.ShapeDtypeStruct((rows, ML_WIDTH), BF16),
                   *(jax.ShapeDtypeStruct((n_streams,) + d, F32) for d in state_dims)),
        grid=(n_streams, nc),
        in_specs=[
            tok(0), tok(1), tok(2),
            pl.BlockSpec((L, LANES), lambda s, c: (s * nc + c, 0)),
            pl.BlockSpec((None, N_GATES, L), lambda s, c: (s * nc + c, 0, 0)),
            const((1, LANES)), const((N_GATES, 1)), const((CONV_W, ML_CONV_CH)), const((1, ML_CONV_CH)),
            const((1, ML_WIDTH)),
            *state_in,
        ],
        out_specs=(pl.BlockSpec((L, ML_WIDTH), lambda s, c: (s * nc + c, 0)), *state_out),
        scratch_shapes=[
            pltpu.VMEM((CONV_PAD + L, ML_CONV_CH), F32),
            pltpu.VMEM((ML_HEADS, ML_V_DIM, ML_QK_DIM), F32),
            pltpu.VMEM((ML_HEADS, ML_QK_DIM), F32),
            pltpu.VMEM((1, ML_HEADS), F32),
        ],
        compiler_params=_params("arbitrary", "arbitrary"),
        name="mlstm",
    )(proj, proj, proj, gates, g_rows, bias_c, bias_r, conv_w, conv_b, ml_g, *state0)


def _lambda(lq1_ref, lk1_ref, lq2_ref, lk2_ref, lam_init):
    return (jnp.exp(jnp.sum(lq1_ref[...] * lk1_ref[...], axis=-1, keepdims=True))
            - jnp.exp(jnp.sum(lq2_ref[...] * lk2_ref[...], axis=-1, keepdims=True)) + lam_init)


def _lane_tiles(x):
    return [x[:, c * LANES:(c + 1) * LANES] for c in range(x.shape[1] // LANES)]


def _fold(op, x):
    return functools.reduce(op, _lane_tiles(x))


def _exp2_rows(s, m_b):
    return jnp.concatenate([jnp.exp2(t - m_b) for t in _lane_tiles(s)], axis=-1)


def _replicate(col):
    return jnp.broadcast_to(col, (col.shape[0], LANES))


def _da_finish(out1, out2, lam, g, lam_init):
    att = out1 - lam * out2
    return ((_rms(att) * g) * (1.0 - lam_init)).astype(BF16)


LOG2E = math.log2(math.e)
Q_SCALE = DA_HEAD_DIM ** -0.5 * LOG2E


def _attn_prompt_kernel(q_ref, k_ref, v_ref, lq1_ref, lk1_ref, lq2_ref, lk2_ref, g_ref, o_ref,
                        kb_ref, vb_ref, s_ref, m_ref, l_ref, acc_ref, *, lam_init, tq):
    i = pl.program_id(2)
    hd = DA_HEAD_DIM

    @pl.when(i == 0)
    def _():
        kb_ref[...] = k_ref[...].astype(BF16)
        vb_ref[...] = v_ref[...].astype(BF16)

    qs = (q_ref[...] * Q_SCALE).astype(BF16)

    def rows(ref, j, nb):
        return ref[pl.ds(pl.multiple_of(j * tq, tq), nb * tq), :]

    def scores(kj, t):
        return lax.dot_general(qs[:, t * hd:(t + 1) * hd], kj[:, t * hd:(t + 1) * hd], NT,
                               preferred_element_type=F32)

    def note_scores(j, nb, t, s):
        for b in range(nb):
            s_ref[t, j + b] = s[:, b * tq:(b + 1) * tq]
        m_ref[t] = jnp.maximum(m_ref[t], _fold(jnp.maximum, s))

    def for_blocks(n, fn):
        def pair(k, carry):
            fn(2 * k, 2)
            return carry

        lax.fori_loop(0, n // 2, pair, 0)
        pl.when(n % 2 == 1)(lambda: fn(n - 1, 1))

    m_ref[...] = jnp.full_like(m_ref, NEG_INF)

    def sweep_max(j, nb):
        kj = rows(kb_ref, j, nb)
        for t in range(2):
            note_scores(j, nb, t, scores(kj, t))

    for_blocks(i, sweep_max)
    diag = (lax.broadcasted_iota(jnp.int32, (tq, 1), 0) // CHUNK
            >= lax.broadcasted_iota(jnp.int32, (1, tq), 1) // CHUNK)
    kd = rows(kb_ref, i, 1)
    for t in range(2):
        note_scores(i, 1, t, jnp.where(diag, scores(kd, t), NEG_INF))
        m_ref[t] = _replicate(jnp.max(m_ref[t], axis=-1, keepdims=True))

    l_ref[...] = jnp.zeros_like(l_ref)
    acc_ref[...] = jnp.zeros_like(acc_ref)

    def sweep_pv(j, nb):
        vj = rows(vb_ref, j, nb)
        for t in range(2):
            p = jnp.concatenate([_exp2_rows(s_ref[t, j + b], m_ref[t]) for b in range(nb)], axis=-1)
            acc_ref[t] += jnp.dot(p.astype(BF16), vj, preferred_element_type=F32)
            l_ref[t] += _fold(jnp.add, p)

    def single(j, carry):
        sweep_pv(j, 1)
        return carry

    lax.fori_loop(0, i + 1, single, 0)
    outs = [acc_ref[t] * (1.0 / jnp.sum(l_ref[t], axis=-1, keepdims=True)) for t in range(2)]
    lam = _lambda(lq1_ref, lk1_ref, lq2_ref, lk2_ref, lam_init)
    o_ref[...] = _da_finish(outs[0], outs[1], lam, g_ref[...], lam_init)


def _attn_prompt(proj, n_streams, seq, lq1, lk1, lq2, lk2, da_g, lam_init):
    tq = _tile(seq, 512)
    nq = seq // tq
    dq = 2 * DA_HEAD_DIM

    def const(shape):
        return pl.BlockSpec(shape, lambda b, h, i: (0,) * len(shape))

    return pl.pallas_call(
        functools.partial(_attn_prompt_kernel, lam_init=lam_init, tq=tq),
        out_shape=jax.ShapeDtypeStruct((n_streams * seq, DA_WIDTH), BF16),
        grid=(n_streams, DA_HEADS, nq),
        in_specs=[
            pl.BlockSpec((None, tq, dq), lambda b, h, i: (3, b * nq + i, h)),
            pl.BlockSpec((None, seq, dq), lambda b, h, i: (4, b, h)),
            pl.BlockSpec((None, seq, DA_V_DIM), lambda b, h, i: (5, b, h)),
            const((1, DA_HEAD_DIM)), const((1, DA_HEAD_DIM)), const((1, DA_HEAD_DIM)), const((1, DA_HEAD_DIM)),
            const((1, DA_V_DIM)),
        ],
        out_specs=pl.BlockSpec((tq, DA_V_DIM), lambda b, h, i: (b * nq + i, h)),
        scratch_shapes=[
            pltpu.VMEM((seq, dq), BF16),
            pltpu.VMEM((seq, DA_V_DIM), BF16),
            pltpu.VMEM((2, nq, tq, tq), F32),
            pltpu.VMEM((2, tq, LANES), F32),
            pltpu.VMEM((2, tq, LANES), F32),
            pltpu.VMEM((2, tq, DA_V_DIM), F32),
        ],
        compiler_params=_params("parallel", "parallel", "arbitrary"),
        name="attn_prompt",
    )(proj, proj, proj, lq1, lk1, lq2, lk2, da_g)


def _attn_sample_kernel(q_ref, kn_ref, vn_ref, ck_ref, cv_ref, lq1_ref, lk1_ref, lq2_ref, lk2_ref, g_ref,
                        o_ref, kbuf, vbuf, sem, *, lam_init, li):
    hd = DA_HEAD_DIM
    s = pl.program_id(0)
    slot = s % 2

    def cache_copies(stream, slot_, h):
        return (pltpu.make_async_copy(ck_ref.at[li, stream, :, h, :], kbuf.at[slot_, h], sem.at[0, slot_, h]),
                pltpu.make_async_copy(cv_ref.at[li, stream, :, h, :], vbuf.at[slot_, h], sem.at[1, slot_, h]))

    def start_stream(stream, slot_):
        for h in range(DA_HEADS):
            for cp in cache_copies(stream, slot_, h):
                cp.start()

    @pl.when(s == 0)
    def _():
        start_stream(0, 0)

    @pl.when(s + 1 < pl.num_programs(0))
    def _():
        start_stream(s + 1, 1 - slot)

    for h in range(DA_HEADS):
        for cp in cache_copies(s, slot, h):
            cp.wait()

    rows = q_ref.shape[0]
    lam = _lambda(lq1_ref, lk1_ref, lq2_ref, lk2_ref, lam_init)
    for h in range(DA_HEADS):
        cols = slice(h * DA_V_DIM, (h + 1) * DA_V_DIM)
        qs = (q_ref[:, cols] * Q_SCALE).astype(BF16)
        kc, vc = kbuf[slot, h].astype(BF16), vbuf[slot, h].astype(BF16)
        kn, vn = kn_ref[:, cols].astype(BF16), vn_ref[:, cols].astype(BF16)
        pcs, pns, ls = [], [], []
        for t in range(2):
            sl = slice(t * hd, (t + 1) * hd)
            sc = lax.dot_general(qs[:, sl], kc[:, sl], NT, preferred_element_type=F32)
            sn = lax.dot_general(qs[:, sl], kn[:, sl], NT, preferred_element_type=F32)
            m = jnp.maximum(jnp.max(_fold(jnp.maximum, sc), axis=-1, keepdims=True),
                            jnp.max(sn, axis=-1, keepdims=True))
            pc = _exp2_rows(sc, _replicate(m))
            pn = jnp.exp2(sn - m)
            ls.append(jnp.sum(_fold(jnp.add, pc), axis=-1, keepdims=True) + jnp.sum(pn, axis=-1, keepdims=True))
            pcs.append(pc.astype(BF16))
            pns.append(pn.astype(BF16))
        acc = (jnp.dot(jnp.concatenate(pcs, axis=0), vc, preferred_element_type=F32)
               + jnp.dot(jnp.concatenate(pns, axis=0), vn, preferred_element_type=F32))
        outs = [acc[t * rows:(t + 1) * rows] * (1.0 / ls[t]) for t in range(2)]
        o_ref[:, cols] = _da_finish(outs[0], outs[1], lam, g_ref[...], lam_init)


def _attn_sample(proj, row0, n_streams, seq, cache_k, cache_v, li, lq1, lk1, lq2, lk2, da_g, lam_init):
    past = cache_k.shape[2]
    base = row0 // seq
    dq = 2 * DA_HEAD_DIM

    def const(shape):
        return pl.BlockSpec(shape, lambda s: (0,) * len(shape))

    def new(piece):
        return pl.BlockSpec((None, seq, PIECE), lambda s: (piece, base + s, 0))

    cache_spec = pl.BlockSpec(memory_space=pl.ANY)
    return pl.pallas_call(
        functools.partial(_attn_sample_kernel, lam_init=lam_init, li=li),
        out_shape=jax.ShapeDtypeStruct((n_streams * seq, DA_WIDTH), BF16),
        grid=(n_streams,),
        in_specs=[
            new(3), new(4), new(5), cache_spec, cache_spec,
            const((1, DA_HEAD_DIM)), const((1, DA_HEAD_DIM)), const((1, DA_HEAD_DIM)), const((1, DA_HEAD_DIM)),
            const((1, DA_V_DIM)),
        ],
        out_specs=pl.BlockSpec((seq, DA_WIDTH), lambda s: (s, 0)),
        scratch_shapes=[
            pltpu.VMEM((2, DA_HEADS, past, dq), F32),
            pltpu.VMEM((2, DA_HEADS, past, DA_V_DIM), F32),
            pltpu.SemaphoreType.DMA((2, 2, DA_HEADS)),
        ],
        compiler_params=_params("arbitrary"),
        name="attn_sample",
    )(proj, proj, proj, cache_k, cache_v, lq1, lk1, lq2, lk2, da_g)


def _outproj_kernel(x_ref, ml_ref, da_ref, w1_ref, w2_ref, post_ref, o_ref):
    y = (jnp.dot(ml_ref[...], w1_ref[...], preferred_element_type=F32)
         + jnp.dot(da_ref[...], w2_ref[...], preferred_element_type=F32))
    o_ref[...] = x_ref[...] + _rms(y) * post_ref[...]


def _outproj(x, ml_out, da_out, w_out, post_g, li):
    T, D = x.shape
    tm = _tile(T, 512)
    return pl.pallas_call(
        _outproj_kernel,
        out_shape=jax.ShapeDtypeStruct((T, D), F32),
        grid=(T // tm,),
        in_specs=[
            pl.BlockSpec((tm, D), lambda i: (i, 0)),
            pl.BlockSpec((tm, ML_WIDTH), lambda i: (i, 0)),
            pl.BlockSpec((tm, DA_WIDTH), lambda i: (i, 0)),
            pl.BlockSpec((None, ML_WIDTH, D), lambda i: (li, 0, 0)),
            pl.BlockSpec((None, DA_WIDTH, D), lambda i: (li, 1, 0)),
            pl.BlockSpec((1, D), lambda i: (0, 0)),
        ],
        out_specs=pl.BlockSpec((tm, D), lambda i: (i, 0)),
        compiler_params=_params("parallel"),
        name="outproj",
    )(x, ml_out, da_out, w_out, w_out, post_g)


def kernel(x_prompt, x_sample, cache_k, cache_v, state_C, state_n, state_m, state_conv,
           ffn1_pre_g, ffn1_wg, ffn1_wu, ffn1_wd, ffn1_post_g,
           mix_pre_g, w_in, b_i, b_f, conv_w, conv_b, ml_norm_g,
           lam_q1, lam_k1, lam_q2, lam_k2, da_norm_g, w_out, mix_post_g,
           ffn2_pre_g, ffn2_wg, ffn2_wu, ffn2_wd, ffn2_post_g):
    Bp, S, D = x_prompt.shape
    Bs, Ls, _ = x_sample.shape
    depth = w_in.shape[0]
    past = cache_k.shape[2]
    Tp, Ts = Bp * S, Bs * Ls
    assert Ls == CHUNK and past % LANES == 0 and S % CHUNK == 0 and Tp % Ls == 0
    Lp = 256 if S % 256 == 0 else CHUNK

    xp, xs = x_prompt.reshape(Tp, D), x_sample.reshape(Ts, D)
    zero_state = (jnp.zeros((1, Bp, ML_HEADS, ML_V_DIM, ML_QK_DIM), F32), jnp.zeros((1, Bp, ML_HEADS, ML_QK_DIM), F32),
                  jnp.zeros((1, Bp, 1, ML_HEADS), F32), jnp.zeros((1, Bp, CONV_W - 1, ML_CONV_CH), F32))
    stream_state = (state_C, state_n, state_m.reshape(depth, Bs, 1, ML_HEADS), state_conv)

    def row(v):
        return v.reshape(1, -1)

    w_main = jnp.concatenate([w_in[:, :, :GATE_OFF], w_in[:, :, GATE_OFF + N_GATES:]], axis=2).astype(BF16)
    w_gate = jnp.pad(w_in[:, :, GATE_OFF:GATE_OFF + N_GATES].astype(BF16), ((0, 0), (0, 0), (0, LANES - N_GATES)))
    w_o = w_out.astype(BF16)

    kv_p, kv_s, st_p, st_s = (), (), [], []
    for li in range(depth):
        lam_init = 0.8 - 0.6 * math.exp(-0.3 * li)
        ffn1 = (row(ffn1_pre_g[li]), ffn1_wg, ffn1_wu, ffn1_wd, row(ffn1_post_g[li]), li)
        ffn2 = (row(ffn2_pre_g[li]), ffn2_wg, ffn2_wu, ffn2_wd, row(ffn2_post_g[li]), li)
        bias = jnp.concatenate([b_i[li], b_f[li]])
        bias_c = jnp.pad(bias, (0, LANES - N_GATES)).reshape(1, LANES)
        bias_r = bias.reshape(N_GATES, 1)
        ml_args = (bias_c, bias_r, conv_w[li], row(conv_b[li]), row(ml_norm_g[li]))
        lam_args = (row(lam_q1[li]), row(lam_k1[li]), row(lam_q2[li]), row(lam_k2[li]), row(da_norm_g[li]), lam_init)

        xp = _ffn(xp, *ffn1)
        xs = _ffn(xs, *ffn1)
        proj_p, gates_p, *kv_p = _inproj(xp, row(mix_pre_g[li]), w_main, w_gate, li, depth, kv_p)
        proj_s, gates_s, *kv_s = _inproj(xs, row(mix_pre_g[li]), w_main, w_gate, li, depth, kv_s)
        ml_p, *st = _mlstm(proj_p, gates_p, Bp, S, Lp, *ml_args, zero_state, 0)
        st_p.append(st)
        ml_s, *st = _mlstm(proj_s, gates_s, Bs, Ls, Ls, *ml_args, stream_state, li)
        st_s.append(st)
        da_p = _attn_prompt(proj_p, Bp, S, *lam_args)
        da_s = _attn_sample(proj_s, 0, Bs, Ls, cache_k, cache_v, li, *lam_args)
        xp = _ffn(_outproj(xp, ml_p, da_p, w_o, row(mix_post_g[li]), li), *ffn2)
        xs = _ffn(_outproj(xs, ml_s, da_s, w_o, row(mix_post_g[li]), li), *ffn2)

    def states(kv, st, n_streams, seq):
        c, n, m, conv = (jnp.stack(e) for e in zip(*st))
        return (kv[0].reshape(depth, n_streams, seq, DA_HEADS, 2 * DA_HEAD_DIM),
                kv[1].reshape(depth, n_streams, seq, DA_HEADS, DA_V_DIM),
                c, n, m.reshape(depth, n_streams, ML_HEADS), conv)

    return (xp.reshape(Bp, S, D), xs.reshape(Bs, Ls, D), *states(kv_p, st_p, Bp, S), *states(kv_s, st_s, Bs, Ls))
```

```python
import functools
import math

import jax
import jax.numpy as jnp
from jax import lax
from jax.experimental import pallas as pl
from jax.experimental.pallas import tpu as pltpu

F32 = jnp.float32
BF16 = jnp.bfloat16

CHUNK = 64
ML_HEADS = 4
ML_QK_DIM = 128
ML_V_DIM = 256
DA_HEADS = 4
DA_HEAD_DIM = 128
DA_V_DIM = 2 * DA_HEAD_DIM
CONV_W = 4
EPS = 1e-6
ML_WIDTH = ML_HEADS * ML_V_DIM
DA_WIDTH = DA_HEADS * DA_V_DIM
ML_CONV_CH = 2 * ML_HEADS * ML_QK_DIM
DA_QK_WIDTH = DA_HEADS * 2 * DA_HEAD_DIM
PIECE = 1024
N_PIECES = 6
N_GATES = 2 * ML_HEADS
GATE_OFF = 3 * PIECE
assert ML_CONV_CH == ML_WIDTH == DA_QK_WIDTH == DA_WIDTH == PIECE

LANES = 128
CONV_PAD = 8
VMEM_LIMIT = 56 * 1024 * 1024

NEG_INF = float("-inf")
NT = (((1,), (1,)), ((), ()))
TN = (((0,), (0,)), ((), ()))


def _params(*sem):
    return pltpu.CompilerParams(dimension_semantics=sem, vmem_limit_bytes=VMEM_LIMIT)


def _rms(xf):
    return xf * lax.rsqrt(jnp.mean(xf * xf, axis=-1, keepdims=True) + EPS)


def _silu(x):
    return x * jax.nn.sigmoid(x)


def _log_sigmoid(x):
    return jnp.minimum(x, 0.0) - jnp.log1p(jnp.exp(-jnp.abs(x)))


def _tile(n, pref):
    return pref if n % pref == 0 else n


def _ffn_kernel(x_ref, pre_ref, wg_ref, wu_ref, wd_ref, post_ref, o_ref, xn_ref, *, tail):
    j = pl.program_id(1)
    last = pl.num_programs(1) - 1

    def slab(xn, valid):
        g = jnp.dot(xn, wg_ref[...].astype(BF16), preferred_element_type=F32)
        u = jnp.dot(xn, wu_ref[...].astype(BF16), preferred_element_type=F32)
        h = _silu(g) * u
        wd = wd_ref[...]
        if valid:
            h = jnp.where(lax.broadcasted_iota(jnp.int32, h.shape, 1) < valid, h, 0.0)
            wd = jnp.where(lax.broadcasted_iota(jnp.int32, wd.shape, 0) < valid, wd, 0.0)
        return jnp.dot(h.astype(BF16), wd.astype(BF16), preferred_element_type=F32)

    @pl.when(j == 0)
    def _():
        xn = (_rms(x_ref[...]) * pre_ref[...]).astype(BF16)
        xn_ref[...] = xn
        o_ref[...] = slab(xn, 0)

    @pl.when((j > 0) & (j < last))
    def _():
        o_ref[...] += slab(xn_ref[...], 0)

    @pl.when(j == last)
    def _():
        y = o_ref[...] + slab(xn_ref[...], tail)
        o_ref[...] = x_ref[...] + 0.5 * (_rms(y) * post_ref[...])


def _ffn(x, pre_g, wg, wu, wd, post_g, li):
    T, D = x.shape
    F = wg.shape[2]
    tm = _tile(T, 1024)
    tf = 256 if F > 256 else F
    nf = pl.cdiv(F, tf)
    return pl.pallas_call(
        functools.partial(_ffn_kernel, tail=F % tf),
        out_shape=jax.ShapeDtypeStruct((T, D), F32),
        grid=(T // tm, nf),
        in_specs=[
            pl.BlockSpec((tm, D), lambda i, j: (i, 0)),
            pl.BlockSpec((1, D), lambda i, j: (0, 0)),
            pl.BlockSpec((None, D, tf), lambda i, j: (li, 0, j)),
            pl.BlockSpec((None, D, tf), lambda i, j: (li, 0, j)),
            pl.BlockSpec((None, tf, D), lambda i, j: (li, j, 0)),
            pl.BlockSpec((1, D), lambda i, j: (0, 0)),
        ],
        out_specs=pl.BlockSpec((tm, D), lambda i, j: (i, 0)),
        scratch_shapes=[pltpu.VMEM((tm, D), BF16)],
        compiler_params=_params("parallel", "arbitrary"),
        name="ffn",
    )(x, pre_g, wg, wu, wd, post_g)


K_PIECE, V_PIECE = 4, 5
HALF_PIECES = N_PIECES // 2
assert HALF_PIECES * PIECE == GATE_OFF


def _inproj_kernel(*refs, li, depth):
    x_ref, pre_ref, wa_ref, wb_ref, wgate_ref = refs[:5]
    o_ref, gate_ref, k5_ref, v5_ref, xn_ref, kv_ref, sem, *zero_scratch = refs[5 + (2 if li else 0):]
    i, j = pl.program_id(0), pl.program_id(1)
    final = (i == pl.num_programs(0) - 1) & (j == pl.num_programs(1) - 1)

    def head_copies(which, dst_ref, tile):
        return [pltpu.make_async_copy(kv_ref.at[:, pl.ds(h * DA_V_DIM, DA_V_DIM)],
                                      dst_ref.at[li, tile, :, h, :], sem.at[which, h]) for h in range(DA_HEADS)]

    @pl.when(j == 0)
    def _():
        xn = (_rms(x_ref[...]) * pre_ref[...]).astype(BF16)
        xn_ref[...] = xn
        gate_ref[...] = jnp.dot(xn, wgate_ref[...], preferred_element_type=F32)
        o_ref[...] = jnp.dot(xn, wa_ref[...], preferred_element_type=F32)

    @pl.when((j > 0) & (j < HALF_PIECES))
    def _():
        o_ref[...] = jnp.dot(xn_ref[...], wa_ref[...], preferred_element_type=F32)

    @pl.when(j >= HALF_PIECES)
    def _():
        o_ref[...] = jnp.dot(xn_ref[...], wb_ref[...], preferred_element_type=F32)

    @pl.when((j == 0) & (i > 0))
    def _():
        for cp in head_copies(1, v5_ref, i - 1):
            cp.wait()

    @pl.when(j == K_PIECE)
    def _():
        kv_ref[...] = o_ref[...]
        for cp in head_copies(0, k5_ref, i):
            cp.start()

    @pl.when(j == V_PIECE)
    def _():
        for cp in head_copies(0, k5_ref, i):
            cp.wait()
        kv_ref[...] = o_ref[...]
        for cp in head_copies(1, v5_ref, i):
            cp.start()

    @pl.when(final)
    def _():
        for cp in head_copies(1, v5_ref, i):
            cp.wait()

    if zero_scratch:
        zero_ref, zero_sem = zero_scratch

        def zero_copies(tile):
            return [pltpu.make_async_copy(zero_ref, dst.at[l, tile, :, h, :], zero_sem.at[n, l - 1, h])
                    for n, dst in enumerate((k5_ref, v5_ref)) for l in range(1, depth) for h in range(DA_HEADS)]

        @pl.when((i == 0) & (j == 0))
        def _():
            zero_ref[...] = jnp.zeros_like(zero_ref)

        @pl.when(j == 1)
        def _():
            for cp in zero_copies(i):
                cp.start()

        @pl.when(j == 2)
        def _():
            for cp in zero_copies(i):
                cp.wait()


def _kv_tile(T):
    return _tile(T, 1024)


def _inproj(x, pre_g, w_a, w_b, w_gate, li, depth, kv):
    T, D = x.shape
    tm = _kv_tile(T)
    assert V_PIECE == N_PIECES - 1 and K_PIECE == V_PIECE - 1 and K_PIECE > 2
    kv_shape = jax.ShapeDtypeStruct((depth, T // tm, tm, DA_HEADS, DA_V_DIM), F32)
    any_spec = pl.BlockSpec(memory_space=pl.ANY)
    zero_fill = li == 0 and depth > 1
    assert len(kv) == (2 if li else 0)
    return pl.pallas_call(
        functools.partial(_inproj_kernel, li=li, depth=depth),
        out_shape=(jax.ShapeDtypeStruct((N_PIECES, T, PIECE), F32), jax.ShapeDtypeStruct((T, LANES), F32),
                   kv_shape, kv_shape),
        grid=(T // tm, N_PIECES),
        in_specs=[
            pl.BlockSpec((tm, D), lambda i, j: (i, 0)),
            pl.BlockSpec((1, D), lambda i, j: (0, 0)),
            pl.BlockSpec((None, D, PIECE), lambda i, j: (li, 0, jnp.minimum(j, HALF_PIECES - 1))),
            pl.BlockSpec((None, D, PIECE), lambda i, j: (li, 0, jnp.maximum(j - HALF_PIECES, 0))),
            pl.BlockSpec((None, D, LANES), lambda i, j: (li, 0, 0)),
        ] + [any_spec] * len(kv),
        out_specs=(pl.BlockSpec((None, tm, PIECE), lambda i, j: (j, i, 0)),
                   pl.BlockSpec((tm, LANES), lambda i, j: (i, 0)), any_spec, any_spec),
        scratch_shapes=[pltpu.VMEM((tm, D), BF16), pltpu.VMEM((tm, PIECE), F32),
                        pltpu.SemaphoreType.DMA((2, DA_HEADS))]
        + ([pltpu.VMEM((tm, DA_V_DIM), F32), pltpu.SemaphoreType.DMA((2, depth - 1, DA_HEADS))] if zero_fill else []),
        input_output_aliases={5: 2, 6: 3} if li else {},
        compiler_params=_params("arbitrary", "arbitrary"),
        name="inproj",
    )(x, pre_g, w_a, w_b, w_gate, *kv)


def _mlstm_kernel(qk_ref, v_ref, og_ref, gc_ref, gr_ref, bc_ref, br_ref, cw_ref, cb_ref, mlg_ref,
                  c0_ref, n0_ref, m0_ref, conv0_ref,
                  out_ref, cn_ref, nn_ref, mn_ref, convn_ref,
                  xp_ref, c_s, n_s, m_s, *, L):
    c = pl.program_id(1)

    @pl.when(c == 0)
    def _():
        c_s[...] = c0_ref[...]
        n_s[...] = n0_ref[...]
        m_s[...] = m0_ref[...]
        xp_ref[0:CONV_PAD, :] = jnp.zeros((CONV_PAD, ML_CONV_CH), F32)
        xp_ref[CONV_PAD - (CONV_W - 1):CONV_PAD, :] = conv0_ref[...]

    qk = qk_ref[...]
    xp_ref[CONV_PAD:CONV_PAD + L, :] = qk
    cw = cw_ref[...]
    conv = cb_ref[...] + qk * cw[CONV_W - 1:CONV_W, :]
    for t in range(CONV_W - 1):
        off = CONV_PAD - (CONV_W - 1) + t
        conv = conv + xp_ref[off:off + L, :] * cw[t:t + 1, :]
    tail = xp_ref[CONV_PAD + L - (CONV_W - 1):CONV_PAD + L, :]
    xp_ref[CONV_PAD - (CONV_W - 1):CONV_PAD, :] = tail
    act = _silu(conv)

    gcb = gc_ref[...] + bc_ref[...]
    grb = gr_ref[...] + br_ref[...]
    row = lax.broadcasted_iota(jnp.int32, (L, L), 0)
    col = lax.broadcasted_iota(jnp.int32, (L, L), 1)
    causal = col <= row
    b_c = jnp.dot(causal.astype(F32), _log_sigmoid(gcb), precision=lax.Precision.HIGHEST,
                  preferred_element_type=F32)
    b_r = jnp.dot(_log_sigmoid(grb), (row <= col).astype(F32), precision=lax.Precision.HIGHEST,
                  preferred_element_type=F32)

    v_all = v_ref[...]
    hs = []
    for h in range(ML_HEADS):
        q = act[:, h * ML_QK_DIM:(h + 1) * ML_QK_DIM]
        k = act[:, ML_HEADS * ML_QK_DIM + h * ML_QK_DIM:ML_HEADS * ML_QK_DIM + (h + 1) * ML_QK_DIM]
        k = k * (ML_QK_DIM ** -0.5)
        qb = q.astype(BF16)
        vb = v_all[:, h * ML_V_DIM:(h + 1) * ML_V_DIM].astype(BF16)
        ig_r = grb[h:h + 1, :]
        ig_c = gcb[:, h:h + 1]
        bh_r = b_r[ML_HEADS + h:ML_HEADS + h + 1, :]
        bh_c = b_c[:, ML_HEADS + h:ML_HEADS + h + 1]
        m_prev = m_s[:, h:h + 1]
        n_prev = n_s[h:h + 1, :]
        c_prev = c_s[h]

        d = jnp.where(causal, bh_c - bh_r + ig_r, NEG_INF)
        inter = bh_c + m_prev
        m_t = jnp.maximum(inter, jnp.max(d, axis=-1, keepdims=True))
        w_intra = jnp.exp(d - m_t)
        w_inter = jnp.exp(inter - m_t)
        s = lax.dot_general(qb, k.astype(BF16), NT, preferred_element_type=F32) * w_intra
        num = (jnp.dot(s.astype(BF16), vb, preferred_element_type=F32)
               + w_inter * lax.dot_general(qb, c_prev.astype(BF16), NT, preferred_element_type=F32))
        den = (jnp.sum(s, axis=-1, keepdims=True)
               + w_inter * jnp.sum(q * n_prev, axis=-1, keepdims=True))
        hs.append(num * (1.0 / jnp.maximum(jnp.abs(den), jnp.exp(-m_t))))

        m_new = m_t[L - 1:L, :]
        b_last = bh_c[L - 1:L, :]
        w_state = jnp.exp(b_last - bh_c + ig_c - m_new)
        decay = jnp.exp(b_last + m_prev - m_new)
        kw = k * w_state
        c_s[h] = decay * c_prev + lax.dot_general(vb, kw.astype(BF16), TN, preferred_element_type=F32)
        n_s[h:h + 1, :] = decay * n_prev + jnp.sum(kw, axis=0, keepdims=True)
        m_s[:, h:h + 1] = m_new

    gated = jax.nn.sigmoid(og_ref[...]) * jnp.concatenate(hs, axis=-1)
    out_ref[...] = (_rms(gated) * mlg_ref[...]).astype(BF16)

    @pl.when(c == pl.num_programs(1) - 1)
    def _():
        cn_ref[...] = c_s[...]
        nn_ref[...] = n_s[...]
        mn_ref[...] = m_s[...]
        convn_ref[...] = tail


def _mlstm(proj, gates, n_streams, seq, L, bias_c, bias_r, conv_w, conv_b, ml_g, state0, li0):
    nc = seq // L
    rows = n_streams * seq
    g_rows = gates[:, :N_GATES].reshape(n_streams * nc, L, N_GATES).transpose(0, 2, 1)

    def tok(piece):
        return pl.BlockSpec((None, L, PIECE), lambda s, c: (piece, s * nc + c, 0))

    def const(shape):
        return pl.BlockSpec(shape, lambda s, c: (0,) * len(shape))

    state_dims = ((ML_HEADS, ML_V_DIM, ML_QK_DIM), (ML_HEADS, ML_QK_DIM), (1, ML_HEADS), (CONV_W - 1, ML_CONV_CH))

    state_in = [pl.BlockSpec((None, None) + d, lambda s, c, n=len(d): (li0, s) + (0,) * n) for d in state_dims]
    state_out = [pl.BlockSpec((None,) + d, lambda s, c, n=len(d): (s,) + (0,) * n) for d in state_dims]
    return pl.pallas_call(
        functools.partial(_mlstm_kernel, L=L),
        out_shape=(jax.ShapeDtypeStruct((rows, ML_WIDTH), BF16),
                   *(jax.ShapeDtypeStruct((n_streams,) + d, F32) for d in state_dims)),
        grid=(n_streams, nc),
        in_specs=[
            tok(0), tok(1), tok(2),
            pl.BlockSpec((L, LANES), lambda s, c: (s * nc + c, 0)),
            pl.BlockSpec((None, N_GATES, L), lambda s, c: (s * nc + c, 0, 0)),
            const((1, LANES)), const((N_GATES, 1)), const((CONV_W, ML_CONV_CH)), const((1, ML_CONV_CH)),
            const((1, ML_WIDTH)),
            *state_in,
        ],
        out_specs=(pl.BlockSpec((L, ML_WIDTH), lambda s, c: (s * nc + c, 0)), *state_out),
        scratch_shapes=[
            pltpu.VMEM((CONV_PAD + L, ML_CONV_CH), F32),
            pltpu.VMEM((ML_HEADS, ML_V_DIM, ML_QK_DIM), F32),
            pltpu.VMEM((ML_HEADS, ML_QK_DIM), F32),
            pltpu.VMEM((1, ML_HEADS), F32),
        ],
        compiler_params=_params("arbitrary", "arbitrary"),
        name="mlstm",
    )(proj, proj, proj, gates, g_rows, bias_c, bias_r, conv_w, conv_b, ml_g, *state0)


def _lambda(lq1_ref, lk1_ref, lq2_ref, lk2_ref, lam_init):
    return (jnp.exp(jnp.sum(lq1_ref[...] * lk1_ref[...], axis=-1, keepdims=True))
            - jnp.exp(jnp.sum(lq2_ref[...] * lk2_ref[...], axis=-1, keepdims=True)) + lam_init)


def _lane_tiles(x):
    return [x[:, c * LANES:(c + 1) * LANES] for c in range(x.shape[1] // LANES)]


def _fold(op, x):
    return functools.reduce(op, _lane_tiles(x))


def _exp2_rows(s, m_b):
    return jnp.concatenate([jnp.exp2(t - m_b) for t in _lane_tiles(s)], axis=-1)


def _replicate(col):
    return jnp.broadcast_to(col, (col.shape[0], LANES))


def _da_finish(out1, out2, lam, g, lam_init):
    att = out1 - lam * out2
    return ((_rms(att) * g) * (1.0 - lam_init)).astype(BF16)


LOG2E = math.log2(math.e)
Q_SCALE = DA_HEAD_DIM ** -0.5 * LOG2E


def _attn_prompt_kernel(q_ref, k_ref, v_ref, lq1_ref, lk1_ref, lq2_ref, lk2_ref, g_ref, o_ref,
                        kb_ref, vb_ref, s_ref, m_ref, l_ref, acc_ref, *, lam_init, tq):
    i = pl.program_id(2)
    hd = DA_HEAD_DIM

    @pl.when(i == 0)
    def _():
        kb_ref[...] = k_ref[...].astype(BF16)
        vb_ref[...] = v_ref[...].astype(BF16)

    qs = (q_ref[...] * Q_SCALE).astype(BF16)

    def rows(ref, j, nb):
        return ref[pl.ds(pl.multiple_of(j * tq, tq), nb * tq), :]

    def scores(kj, t):
        return lax.dot_general(qs[:, t * hd:(t + 1) * hd], kj[:, t * hd:(t + 1) * hd], NT,
                               preferred_element_type=F32)

    def note_scores(j, nb, t, s):
        for b in range(nb):
            s_ref[t, j + b] = s[:, b * tq:(b + 1) * tq]
        m_ref[t] = jnp.maximum(m_ref[t], _fold(jnp.maximum, s))

    def for_blocks(n, fn):
        def pair(k, carry):
            fn(2 * k, 2)
            return carry

        lax.fori_loop(0, n // 2, pair, 0)
        pl.when(n % 2 == 1)(lambda: fn(n - 1, 1))

    m_ref[...] = jnp.full_like(m_ref, NEG_INF)

    def sweep_max(j, nb):
        kj = rows(kb_ref, j, nb)
        for t in range(2):
            note_scores(j, nb, t, scores(kj, t))

    for_blocks(i, sweep_max)
    diag = (lax.broadcasted_iota(jnp.int32, (tq, 1), 0) // CHUNK
            >= lax.broadcasted_iota(jnp.int32, (1, tq), 1) // CHUNK)
    kd = rows(kb_ref, i, 1)
    for t in range(2):
        note_scores(i, 1, t, jnp.where(diag, scores(kd, t), NEG_INF))
        m_ref[t] = _replicate(jnp.max(m_ref[t], axis=-1, keepdims=True))

    l_ref[...] = jnp.zeros_like(l_ref)
    acc_ref[...] = jnp.zeros_like(acc_ref)

    def sweep_pv(j, carry):
        vj = rows(vb_ref, j, 1)
        for t in range(2):
            p = _exp2_rows(s_ref[t, j], m_ref[t])
            acc_ref[t] += jnp.dot(p.astype(BF16), vj, preferred_element_type=F32)
            l_ref[t] += _fold(jnp.add, p)
        return carry

    lax.fori_loop(0, i + 1, sweep_pv, 0)
    outs = [acc_ref[t] * (1.0 / jnp.sum(l_ref[t], axis=-1, keepdims=True)) for t in range(2)]
    lam = _lambda(lq1_ref, lk1_ref, lq2_ref, lk2_ref, lam_init)
    o_ref[...] = _da_finish(outs[0], outs[1], lam, g_ref[...], lam_init)


def _attn_prompt(proj, n_streams, seq, lq1, lk1, lq2, lk2, da_g, lam_init):
    tq = _tile(seq, 512)
    nq = seq // tq
    dq = 2 * DA_HEAD_DIM

    def const(shape):
        return pl.BlockSpec(shape, lambda b, h, i: (0,) * len(shape))

    return pl.pallas_call(
        functools.partial(_attn_prompt_kernel, lam_init=lam_init, tq=tq),
        out_shape=jax.ShapeDtypeStruct((n_streams * seq, DA_WIDTH), BF16),
        grid=(n_streams, DA_HEADS, nq),
        in_specs=[
            pl.BlockSpec((None, tq, dq), lambda b, h, i: (3, b * nq + i, h)),
            pl.BlockSpec((None, seq, dq), lambda b, h, i: (4, b, h)),
            pl.BlockSpec((None, seq, DA_V_DIM), lambda b, h, i: (5, b, h)),
            const((1, DA_HEAD_DIM)), const((1, DA_HEAD_DIM)), const((1, DA_HEAD_DIM)), const((1, DA_HEAD_DIM)),
            const((1, DA_V_DIM)),
        ],
        out_specs=pl.BlockSpec((tq, DA_V_DIM), lambda b, h, i: (b * nq + i, h)),
        scratch_shapes=[
            pltpu.VMEM((seq, dq), BF16),
            pltpu.VMEM((seq, DA_V_DIM), BF16),
            pltpu.VMEM((2, nq, tq, tq), F32),
            pltpu.VMEM((2, tq, LANES), F32),
            pltpu.VMEM((2, tq, LANES), F32),
            pltpu.VMEM((2, tq, DA_V_DIM), F32),
        ],
        compiler_params=_params("parallel", "parallel", "arbitrary"),
        name="attn_prompt",
    )(proj, proj, proj, lq1, lk1, lq2, lk2, da_g)


def _attn_sample_kernel(q_ref, kn_ref, vn_ref, ck_ref, cv_ref, lq1_ref, lk1_ref, lq2_ref, lk2_ref, g_ref,
                        o_ref, kbuf, vbuf, sem, *, lam_init, li):
    hd = DA_HEAD_DIM
    s = pl.program_id(0)
    slot = s % 2

    def cache_copies(stream, slot_, h):
        return (pltpu.make_async_copy(ck_ref.at[li, stream, :, h, :], kbuf.at[slot_, h], sem.at[0, slot_, h]),
                pltpu.make_async_copy(cv_ref.at[li, stream, :, h, :], vbuf.at[slot_, h], sem.at[1, slot_, h]))

    def start_stream(stream, slot_):
        for h in range(DA_HEADS):
            for cp in cache_copies(stream, slot_, h):
                cp.start()

    @pl.when(s == 0)
    def _():
        start_stream(0, 0)

    @pl.when(s + 1 < pl.num_programs(0))
    def _():
        start_stream(s + 1, 1 - slot)

    for h in range(DA_HEADS):
        for cp in cache_copies(s, slot, h):
            cp.wait()

    rows = q_ref.shape[0]
    lam = _lambda(lq1_ref, lk1_ref, lq2_ref, lk2_ref, lam_init)
    for h in range(DA_HEADS):
        cols = slice(h * DA_V_DIM, (h + 1) * DA_V_DIM)
        qs = (q_ref[:, cols] * Q_SCALE).astype(BF16)
        kc, vc = kbuf[slot, h].astype(BF16), vbuf[slot, h].astype(BF16)
        kn, vn = kn_ref[:, cols].astype(BF16), vn_ref[:, cols].astype(BF16)
        pcs, pns, ls = [], [], []
        for t in range(2):
            sl = slice(t * hd, (t + 1) * hd)
            sc = lax.dot_general(qs[:, sl], kc[:, sl], NT, preferred_element_type=F32)
            sn = lax.dot_general(qs[:, sl], kn[:, sl], NT, preferred_element_type=F32)
            m = jnp.maximum(jnp.max(_fold(jnp.maximum, sc), axis=-1, keepdims=True),
                            jnp.max(sn, axis=-1, keepdims=True))
            pc = _exp2_rows(sc, _replicate(m))
            pn = jnp.exp2(sn - m)
            ls.append(jnp.sum(_fold(jnp.add, pc), axis=-1, keepdims=True) + jnp.sum(pn, axis=-1, keepdims=True))
            pcs.append(pc.astype(BF16))
            pns.append(pn.astype(BF16))
        acc = (jnp.dot(jnp.concatenate(pcs, axis=0), vc, preferred_element_type=F32)
               + jnp.dot(jnp.concatenate(pns, axis=0), vn, preferred_element_type=F32))
        outs = [acc[t * rows:(t + 1) * rows] * (1.0 / ls[t]) for t in range(2)]
        o_ref[:, cols] = _da_finish(outs[0], outs[1], lam, g_ref[...], lam_init)


def _attn_sample(proj, row0, n_streams, seq, cache_k, cache_v, li, lq1, lk1, lq2, lk2, da_g, lam_init):
    past = cache_k.shape[2]
    base = row0 // seq
    dq = 2 * DA_HEAD_DIM

    def const(shape):
        return pl.BlockSpec(shape, lambda s: (0,) * len(shape))

    def new(piece):
        return pl.BlockSpec((None, seq, PIECE), lambda s: (piece, base + s, 0))

    cache_spec = pl.BlockSpec(memory_space=pl.ANY)
    return pl.pallas_call(
        functools.partial(_attn_sample_kernel, lam_init=lam_init, li=li),
        out_shape=jax.ShapeDtypeStruct((n_streams * seq, DA_WIDTH), BF16),
        grid=(n_streams,),
        in_specs=[
            new(3), new(4), new(5), cache_spec, cache_spec,
            const((1, DA_HEAD_DIM)), const((1, DA_HEAD_DIM)), const((1, DA_HEAD_DIM)), const((1, DA_HEAD_DIM)),
            const((1, DA_V_DIM)),
        ],
        out_specs=pl.BlockSpec((seq, DA_WIDTH), lambda s: (s, 0)),
        scratch_shapes=[
            pltpu.VMEM((2, DA_HEADS, past, dq), F32),
            pltpu.VMEM((2, DA_HEADS, past, DA_V_DIM), F32),
            pltpu.SemaphoreType.DMA((2, 2, DA_HEADS)),
        ],
        compiler_params=_params("arbitrary"),
        name="attn_sample",
    )(proj, proj, proj, cache_k, cache_v, lq1, lk1, lq2, lk2, da_g)


def _outproj_kernel(x_ref, ml_ref, da_ref, w1_ref, w2_ref, post_ref, o_ref):
    y = (jnp.dot(ml_ref[...], w1_ref[...], preferred_element_type=F32)
         + jnp.dot(da_ref[...], w2_ref[...], preferred_element_type=F32))
    o_ref[...] = x_ref[...] + _rms(y) * post_ref[...]


def _outproj(x, ml_out, da_out, w_out, post_g, li):
    T, D = x.shape
    tm = _tile(T, 512)
    return pl.pallas_call(
        _outproj_kernel,
        out_shape=jax.ShapeDtypeStruct((T, D), F32),
        grid=(T // tm,),
        in_specs=[
            pl.BlockSpec((tm, D), lambda i: (i, 0)),
            pl.BlockSpec((tm, ML_WIDTH), lambda i: (i, 0)),
            pl.BlockSpec((tm, DA_WIDTH), lambda i: (i, 0)),
            pl.BlockSpec((None, ML_WIDTH, D), lambda i: (li, 0, 0)),
            pl.BlockSpec((None, DA_WIDTH, D), lambda i: (li, 1, 0)),
            pl.BlockSpec((1, D), lambda i: (0, 0)),
        ],
        out_specs=pl.BlockSpec((tm, D), lambda i: (i, 0)),
        compiler_params=_params("parallel"),
        name="outproj",
    )(x, ml_out, da_out, w_out, w_out, post_g)


def kernel(x_prompt, x_sample, cache_k, cache_v, state_C, state_n, state_m, state_conv,
           ffn1_pre_g, ffn1_wg, ffn1_wu, ffn1_wd, ffn1_post_g,
           mix_pre_g, w_in, b_i, b_f, conv_w, conv_b, ml_norm_g,
           lam_q1, lam_k1, lam_q2, lam_k2, da_norm_g, w_out, mix_post_g,
           ffn2_pre_g, ffn2_wg, ffn2_wu, ffn2_wd, ffn2_post_g):
    Bp, S, D = x_prompt.shape
    Bs, Ls, _ = x_sample.shape
    depth = w_in.shape[0]
    past = cache_k.shape[2]
    Tp, Ts = Bp * S, Bs * Ls
    assert Ls == CHUNK and past % LANES == 0 and S % CHUNK == 0 and Tp % Ls == 0
    Lp = 256 if S % 256 == 0 else CHUNK

    xp, xs = x_prompt.reshape(Tp, D), x_sample.reshape(Ts, D)
    zero_state = (jnp.zeros((1, Bp, ML_HEADS, ML_V_DIM, ML_QK_DIM), F32), jnp.zeros((1, Bp, ML_HEADS, ML_QK_DIM), F32),
                  jnp.zeros((1, Bp, 1, ML_HEADS), F32), jnp.zeros((1, Bp, CONV_W - 1, ML_CONV_CH), F32))
    stream_state = (state_C, state_n, state_m.reshape(depth, Bs, 1, ML_HEADS), state_conv)

    def row(v):
        return v.reshape(1, -1)

    w_a = w_in[:, :, :GATE_OFF].astype(BF16)
    w_b = w_in[:, :, GATE_OFF + N_GATES:].astype(BF16)
    w_gate = jnp.pad(w_in[:, :, GATE_OFF:GATE_OFF + N_GATES].astype(BF16), ((0, 0), (0, 0), (0, LANES - N_GATES)))
    w_o = w_out.astype(BF16)

    kv_p, kv_s, st_p, st_s = (), (), [], []
    for li in range(depth):
        lam_init = 0.8 - 0.6 * math.exp(-0.3 * li)
        ffn1 = (row(ffn1_pre_g[li]), ffn1_wg, ffn1_wu, ffn1_wd, row(ffn1_post_g[li]), li)
        ffn2 = (row(ffn2_pre_g[li]), ffn2_wg, ffn2_wu, ffn2_wd, row(ffn2_post_g[li]), li)
        bias = jnp.concatenate([b_i[li], b_f[li]])
        bias_c = jnp.pad(bias, (0, LANES - N_GATES)).reshape(1, LANES)
        bias_r = bias.reshape(N_GATES, 1)
        ml_args = (bias_c, bias_r, conv_w[li], row(conv_b[li]), row(ml_norm_g[li]))
        lam_args = (row(lam_q1[li]), row(lam_k1[li]), row(lam_q2[li]), row(lam_k2[li]), row(da_norm_g[li]), lam_init)

        xp = _ffn(xp, *ffn1)
        xs = _ffn(xs, *ffn1)
        proj_p, gates_p, *kv_p = _inproj(xp, row(mix_pre_g[li]), w_a, w_b, w_gate, li, depth, kv_p)
        proj_s, gates_s, *kv_s = _inproj(xs, row(mix_pre_g[li]), w_a, w_b, w_gate, li, depth, kv_s)
        ml_p, *st = _mlstm(proj_p, gates_p, Bp, S, Lp, *ml_args, zero_state, 0)
        st_p.append(st)
        ml_s, *st = _mlstm(proj_s, gates_s, Bs, Ls, Ls, *ml_args, stream_state, li)
        st_s.append(st)
        da_p = _attn_prompt(proj_p, Bp, S, *lam_args)
        da_s = _attn_sample(proj_s, 0, Bs, Ls, cache_k, cache_v, li, *lam_args)
        xp = _ffn(_outproj(xp, ml_p, da_p, w_o, row(mix_post_g[li]), li), *ffn2)
        xs = _ffn(_outproj(xs, ml_s, da_s, w_o, row(mix_post_g[li]), li), *ffn2)

    def states(kv, st, n_streams, seq):
        c, n, m, conv = (jnp.stack(e) for e in zip(*st))
        return (kv[0].reshape(depth, n_streams, seq, DA_HEADS, 2 * DA_HEAD_DIM),
                kv[1].reshape(depth, n_streams, seq, DA_HEADS, DA_V_DIM),
                c, n, m.reshape(depth, n_streams, ML_HEADS), conv)

    return (xp.reshape(Bp, S, D), xs.reshape(Bs, Ls, D), *states(kv_p, st_p, Bp, S), *states(kv_s, st_s, Bs, Ls))
```

```python
import functools
import math

import jax
import jax.numpy as jnp
from jax import lax
from jax.experimental import pallas as pl
from jax.experimental.pallas import tpu as pltpu

F32 = jnp.float32
BF16 = jnp.bfloat16

CHUNK = 64
ML_HEADS = 4
ML_QK_DIM = 128
ML_V_DIM = 256
DA_HEADS = 4
DA_HEAD_DIM = 128
DA_V_DIM = 2 * DA_HEAD_DIM
CONV_W = 4
EPS = 1e-6
ML_WIDTH = ML_HEADS * ML_V_DIM
DA_WIDTH = DA_HEADS * DA_V_DIM
ML_CONV_CH = 2 * ML_HEADS * ML_QK_DIM
DA_QK_WIDTH = DA_HEADS * 2 * DA_HEAD_DIM
PIECE = 1024
N_PIECES = 6
N_GATES = 2 * ML_HEADS
GATE_OFF = 3 * PIECE
assert ML_CONV_CH == ML_WIDTH == DA_QK_WIDTH == DA_WIDTH == PIECE

LANES = 128
CONV_PAD = 8
VMEM_LIMIT = 56 * 1024 * 1024

NEG_INF = float("-inf")
NT = (((1,), (1,)), ((), ()))
TN = (((0,), (0,)), ((), ()))


def _params(*sem):
    return pltpu.CompilerParams(dimension_semantics=sem, vmem_limit_bytes=VMEM_LIMIT)


def _rms(xf):
    return xf * lax.rsqrt(jnp.mean(xf * xf, axis=-1, keepdims=True) + EPS)


def _silu(x):
    return x * jax.nn.sigmoid(x)


def _log_sigmoid(x):
    return jnp.minimum(x, 0.0) - jnp.log1p(jnp.exp(-jnp.abs(x)))


def _tile(n, pref):
    return pref if n % pref == 0 else n


def _ffn_kernel(x_ref, pre_ref, wg_ref, wu_ref, wd_ref, post_ref, o_ref, xn_ref, *, tail):
    j = pl.program_id(1)
    last = pl.num_programs(1) - 1

    def slab(xn, valid):
        g = jnp.dot(xn, wg_ref[...].astype(BF16), preferred_element_type=F32)
        u = jnp.dot(xn, wu_ref[...].astype(BF16), preferred_element_type=F32)
        h = _silu(g) * u
        wd = wd_ref[...]
        if valid:
            h = jnp.where(lax.broadcasted_iota(jnp.int32, h.shape, 1) < valid, h, 0.0)
            wd = jnp.where(lax.broadcasted_iota(jnp.int32, wd.shape, 0) < valid, wd, 0.0)
        return jnp.dot(h.astype(BF16), wd.astype(BF16), preferred_element_type=F32)

    @pl.when(j == 0)
    def _():
        xn = (_rms(x_ref[...]) * pre_ref[...]).astype(BF16)
        xn_ref[...] = xn
        o_ref[...] = slab(xn, 0)

    @pl.when((j > 0) & (j < last))
    def _():
        o_ref[...] += slab(xn_ref[...], 0)

    @pl.when(j == last)
    def _():
        y = o_ref[...] + slab(xn_ref[...], tail)
        o_ref[...] = x_ref[...] + 0.5 * (_rms(y) * post_ref[...])


def _ffn(x, pre_g, wg, wu, wd, post_g, li):
    T, D = x.shape
    F = wg.shape[2]
    tm = _tile(T, 1024)
    tf = 256 if F > 256 else F
    nf = pl.cdiv(F, tf)
    return pl.pallas_call(
        functools.partial(_ffn_kernel, tail=F % tf),
        out_shape=jax.ShapeDtypeStruct((T, D), F32),
        grid=(T // tm, nf),
        in_specs=[
            pl.BlockSpec((tm, D), lambda i, j: (i, 0)),
            pl.BlockSpec((1, D), lambda i, j: (0, 0)),
            pl.BlockSpec((None, D, tf), lambda i, j: (li, 0, j)),
            pl.BlockSpec((None, D, tf), lambda i, j: (li, 0, j)),
            pl.BlockSpec((None, tf, D), lambda i, j: (li, j, 0)),
            pl.BlockSpec((1, D), lambda i, j: (0, 0)),
        ],
        out_specs=pl.BlockSpec((tm, D), lambda i, j: (i, 0)),
        scratch_shapes=[pltpu.VMEM((tm, D), BF16)],
        compiler_params=_params("parallel", "arbitrary"),
        name="ffn",
    )(x, pre_g, wg, wu, wd, post_g)


K_PIECE, V_PIECE = 4, 5
WAIT_PIECE = 2
assert 0 < WAIT_PIECE < K_PIECE


def _inproj_kernel(*refs, li, depth):
    x_ref, pre_ref, w_ref, wgate_ref = refs[:4]
    o_ref, gate_ref, k5_ref, v5_ref, xn_ref, kv_ref, sem, *zero_scratch = refs[4 + (2 if li else 0):]
    i, j = pl.program_id(0), pl.program_id(1)
    final = (i == pl.num_programs(0) - 1) & (j == pl.num_programs(1) - 1)

    def head_copies(tile, which=(0, 1)):
        return [pltpu.make_async_copy(kv_ref.at[n, :, pl.ds(h * DA_V_DIM, DA_V_DIM)],
                                      (k5_ref, v5_ref)[n].at[li, tile, :, h, :], sem.at[n, h])
                for n in which for h in range(DA_HEADS)]

    def piece(xn):
        return jnp.dot(xn, w_ref[...], preferred_element_type=F32)

    @pl.when(j == 0)
    def _():
        xn = (_rms(x_ref[...]) * pre_ref[...]).astype(BF16)
        xn_ref[...] = xn
        gate_ref[...] = jnp.dot(xn, wgate_ref[...], preferred_element_type=F32)
        o_ref[...] = piece(xn)

    @pl.when((j > 0) & (j < K_PIECE))
    def _():
        o_ref[...] = piece(xn_ref[...])

    @pl.when((j == WAIT_PIECE) & (i > 0))
    def _():
        for cp in head_copies(i - 1):
            cp.wait()

    for n, j_kv in enumerate((K_PIECE, V_PIECE)):
        @pl.when(j == j_kv)
        def _(n=n):
            res = piece(xn_ref[...])
            o_ref[...] = res
            kv_ref[n] = res
            for cp in head_copies(i, (n,)):
                cp.start()

    @pl.when(final)
    def _():
        for cp in head_copies(i):
            cp.wait()

    if zero_scratch:
        zero_ref, zero_sem = zero_scratch

        def zero_copies(tile):
            return [pltpu.make_async_copy(zero_ref, dst.at[l, tile, :, h, :], zero_sem.at[n, l - 1, h])
                    for n, dst in enumerate((k5_ref, v5_ref)) for l in range(1, depth) for h in range(DA_HEADS)]

        @pl.when((i == 0) & (j == 0))
        def _():
            zero_ref[...] = jnp.zeros_like(zero_ref)

        @pl.when(j == 1)
        def _():
            for cp in zero_copies(i):
                cp.start()

        @pl.when(j == 2)
        def _():
            for cp in zero_copies(i):
                cp.wait()


def _kv_tile(T):
    return _tile(T, 1024)


def _inproj(x, pre_g, w_main, w_gate, li, depth, kv):
    T, D = x.shape
    tm = _kv_tile(T)
    assert V_PIECE == N_PIECES - 1 and K_PIECE == V_PIECE - 1 and K_PIECE > 2
    kv_shape = jax.ShapeDtypeStruct((depth, T // tm, tm, DA_HEADS, DA_V_DIM), F32)
    any_spec = pl.BlockSpec(memory_space=pl.ANY)
    zero_fill = li == 0 and depth > 1
    assert len(kv) == (2 if li else 0)
    return pl.pallas_call(
        functools.partial(_inproj_kernel, li=li, depth=depth),
        out_shape=(jax.ShapeDtypeStruct((N_PIECES, T, PIECE), F32), jax.ShapeDtypeStruct((T, LANES), F32),
                   kv_shape, kv_shape),
        grid=(T // tm, N_PIECES),
        in_specs=[
            pl.BlockSpec((tm, D), lambda i, j: (i, 0)),
            pl.BlockSpec((1, D), lambda i, j: (0, 0)),
            pl.BlockSpec((None, D, PIECE), lambda i, j: (li, 0, j)),
            pl.BlockSpec((None, D, LANES), lambda i, j: (li, 0, 0)),
        ] + [any_spec] * len(kv),
        out_specs=(pl.BlockSpec((None, tm, PIECE), lambda i, j: (j, i, 0)),
                   pl.BlockSpec((tm, LANES), lambda i, j: (i, 0)), any_spec, any_spec),
        scratch_shapes=[pltpu.VMEM((tm, D), BF16), pltpu.VMEM((2, tm, PIECE), F32),
                        pltpu.SemaphoreType.DMA((2, DA_HEADS))]
        + ([pltpu.VMEM((tm, DA_V_DIM), F32), pltpu.SemaphoreType.DMA((2, depth - 1, DA_HEADS))] if zero_fill else []),
        input_output_aliases={4: 2, 5: 3} if li else {},
        compiler_params=_params("arbitrary", "arbitrary"),
        name="inproj",
    )(x, pre_g, w_main, w_gate, *kv)


def _mlstm_kernel(qk_ref, v_ref, og_ref, gc_ref, gr_ref, bc_ref, br_ref, cw_ref, cb_ref, mlg_ref,
                  c0_ref, n0_ref, m0_ref, conv0_ref,
                  out_ref, cn_ref, nn_ref, mn_ref, convn_ref,
                  xp_ref, c_s, n_s, m_s, *, L):
    c = pl.program_id(1)

    @pl.when(c == 0)
    def _():
        c_s[...] = c0_ref[...]
        n_s[...] = n0_ref[...]
        m_s[...] = m0_ref[...]
        xp_ref[0:CONV_PAD, :] = jnp.zeros((CONV_PAD, ML_CONV_CH), F32)
        xp_ref[CONV_PAD - (CONV_W - 1):CONV_PAD, :] = conv0_ref[...]

    qk = qk_ref[...]
    xp_ref[CONV_PAD:CONV_PAD + L, :] = qk
    cw = cw_ref[...]
    conv = cb_ref[...] + qk * cw[CONV_W - 1:CONV_W, :]
    for t in range(CONV_W - 1):
        off = CONV_PAD - (CONV_W - 1) + t
        conv = conv + xp_ref[off:off + L, :] * cw[t:t + 1, :]
    tail = xp_ref[CONV_PAD + L - (CONV_W - 1):CONV_PAD + L, :]
    xp_ref[CONV_PAD - (CONV_W - 1):CONV_PAD, :] = tail
    act = _silu(conv)

    gcb = gc_ref[...] + bc_ref[...]
    grb = gr_ref[...] + br_ref[...]
    row = lax.broadcasted_iota(jnp.int32, (L, L), 0)
    col = lax.broadcasted_iota(jnp.int32, (L, L), 1)
    causal = col <= row
    b_c = jnp.dot(causal.astype(F32), _log_sigmoid(gcb), precision=lax.Precision.HIGHEST,
                  preferred_element_type=F32)
    b_r = jnp.dot(_log_sigmoid(grb), (row <= col).astype(F32), precision=lax.Precision.HIGHEST,
                  preferred_element_type=F32)

    v_all = v_ref[...]
    hs = []
    for h in range(ML_HEADS):
        q = act[:, h * ML_QK_DIM:(h + 1) * ML_QK_DIM]
        k = act[:, ML_HEADS * ML_QK_DIM + h * ML_QK_DIM:ML_HEADS * ML_QK_DIM + (h + 1) * ML_QK_DIM]
        k = k * (ML_QK_DIM ** -0.5)
        qb = q.astype(BF16)
        vb = v_all[:, h * ML_V_DIM:(h + 1) * ML_V_DIM].astype(BF16)
        ig_r = grb[h:h + 1, :]
        ig_c = gcb[:, h:h + 1]
        bh_r = b_r[ML_HEADS + h:ML_HEADS + h + 1, :]
        bh_c = b_c[:, ML_HEADS + h:ML_HEADS + h + 1]
        m_prev = m_s[:, h:h + 1]
        n_prev = n_s[h:h + 1, :]
        c_prev = c_s[h]

        d = jnp.where(causal, bh_c - bh_r + ig_r, NEG_INF)
        inter = bh_c + m_prev
        m_t = jnp.maximum(inter, jnp.max(d, axis=-1, keepdims=True))
        w_intra = jnp.exp(d - m_t)
        w_inter = jnp.exp(inter - m_t)
        s = lax.dot_general(qb, k.astype(BF16), NT, preferred_element_type=F32) * w_intra
        num = (jnp.dot(s.astype(BF16), vb, preferred_element_type=F32)
               + w_inter * lax.dot_general(qb, c_prev.astype(BF16), NT, preferred_element_type=F32))
        den = (jnp.sum(s, axis=-1, keepdims=True)
               + w_inter * jnp.sum(q * n_prev, axis=-1, keepdims=True))
        hs.append(num * (1.0 / jnp.maximum(jnp.abs(den), jnp.exp(-m_t))))

        m_new = m_t[L - 1:L, :]
        b_last = bh_c[L - 1:L, :]
        w_state = jnp.exp(b_last - bh_c + ig_c - m_new)
        decay = jnp.exp(b_last + m_prev - m_new)
        kw = k * w_state
        c_s[h] = decay * c_prev + lax.dot_general(vb, kw.astype(BF16), TN, preferred_element_type=F32)
        n_s[h:h + 1, :] = decay * n_prev + jnp.sum(kw, axis=0, keepdims=True)
        m_s[:, h:h + 1] = m_new

    gated = jax.nn.sigmoid(og_ref[...]) * jnp.concatenate(hs, axis=-1)
    out_ref[...] = (_rms(gated) * mlg_ref[...]).astype(BF16)

    @pl.when(c == pl.num_programs(1) - 1)
    def _():
        cn_ref[...] = c_s[...]
        nn_ref[...] = n_s[...]
        mn_ref[...] = m_s[...]
        convn_ref[...] = tail


def _mlstm(proj, gates, n_streams, seq, L, bias_c, bias_r, conv_w, conv_b, ml_g, state0, li0):
    nc = seq // L
    rows = n_streams * seq
    g_rows = gates[:, :N_GATES].reshape(n_streams * nc, L, N_GATES).transpose(0, 2, 1)

    def tok(piece):
        return pl.BlockSpec((None, L, PIECE), lambda s, c: (piece, s * nc + c, 0))

    def const(shape):
        return pl.BlockSpec(shape, lambda s, c: (0,) * len(shape))

    state_dims = ((ML_HEADS, ML_V_DIM, ML_QK_DIM), (ML_HEADS, ML_QK_DIM), (1, ML_HEADS), (CONV_W - 1, ML_CONV_CH))

    state_in = [pl.BlockSpec((None, None) + d, lambda s, c, n=len(d): (li0, s) + (0,) * n) for d in state_dims]
    state_out = [pl.BlockSpec((None,) + d, lambda s, c, n=len(d): (s,) + (0,) * n) for d in state_dims]
    return pl.pallas_call(
        functools.partial(_mlstm_kernel, L=L),
        out_shape=(jax.ShapeDtypeStruct((rows, ML_WIDTH), BF16),
                   *(jax.ShapeDtypeStruct((n_streams,) + d, F32) for d in state_dims)),
        grid=(n_streams, nc),
        in_specs=[
            tok(0), tok(1), tok(2),
            pl.BlockSpec((L, LANES), lambda s, c: (s * nc + c, 0)),
            pl.BlockSpec((None, N_GATES, L), lambda s, c: (s * nc + c, 0, 0)),
            const((1, LANES)), const((N_GATES, 1)), const((CONV_W, ML_CONV_CH)), const((1, ML_CONV_CH)),
            const((1, ML_WIDTH)),
            *state_in,
        ],
        out_specs=(pl.BlockSpec((L, ML_WIDTH), lambda s, c: (s * nc + c, 0)), *state_out),
        scratch_shapes=[
            pltpu.VMEM((CONV_PAD + L, ML_CONV_CH), F32),
            pltpu.VMEM((ML_HEADS, ML_V_DIM, ML_QK_DIM), F32),
            pltpu.VMEM((ML_HEADS, ML_QK_DIM), F32),
            pltpu.VMEM((1, ML_HEADS), F32),
        ],
        compiler_params=_params("arbitrary", "arbitrary"),
        name="mlstm",
    )(proj, proj, proj, gates, g_rows, bias_c, bias_r, conv_w, conv_b, ml_g, *state0)


def _lambda(lq1_ref, lk1_ref, lq2_ref, lk2_ref, lam_init):
    return (jnp.exp(jnp.sum(lq1_ref[...] * lk1_ref[...], axis=-1, keepdims=True))
            - jnp.exp(jnp.sum(lq2_ref[...] * lk2_ref[...], axis=-1, keepdims=True)) + lam_init)


def _lane_tiles(x):
    return [x[:, c * LANES:(c + 1) * LANES] for c in range(x.shape[1] // LANES)]


def _fold(op, x):
    return functools.reduce(op, _lane_tiles(x))


def _exp2_rows(s, m_b):
    return jnp.concatenate([jnp.exp2(t - m_b) for t in _lane_tiles(s)], axis=-1)


def _replicate(col):
    return jnp.broadcast_to(col, (col.shape[0], LANES))


def _da_finish(out1, out2, lam, g, lam_init):
    att = out1 - lam * out2
    return ((_rms(att) * g) * (1.0 - lam_init)).astype(BF16)


LOG2E = math.log2(math.e)
Q_SCALE = DA_HEAD_DIM ** -0.5 * LOG2E


def _attn_prompt_kernel(q_ref, k_ref, v_ref, lq1_ref, lk1_ref, lq2_ref, lk2_ref, g_ref, o_ref,
                        kb_ref, vb_ref, s_ref, m_ref, l_ref, acc_ref, *, lam_init, tq):
    i = pl.program_id(2)
    hd = DA_HEAD_DIM

    @pl.when(i == 0)
    def _():
        kb_ref[...] = k_ref[...].astype(BF16)
        vb_ref[...] = v_ref[...].astype(BF16)

    qs = (q_ref[...] * Q_SCALE).astype(BF16)

    def rows(ref, j, nb):
        return ref[pl.ds(pl.multiple_of(j * tq, tq), nb * tq), :]

    def scores(kj, t):
        return lax.dot_general(qs[:, t * hd:(t + 1) * hd], kj[:, t * hd:(t + 1) * hd], NT,
                               preferred_element_type=F32)

    def note_scores(j, nb, t, s):
        for b in range(nb):
            s_ref[t, j + b] = s[:, b * tq:(b + 1) * tq]
        m_ref[t] = jnp.maximum(m_ref[t], _fold(jnp.maximum, s))

    def for_blocks(n, fn):
        def pair(k, carry):
            fn(2 * k, 2)
            return carry

        lax.fori_loop(0, n // 2, pair, 0)
        pl.when(n % 2 == 1)(lambda: fn(n - 1, 1))

    m_ref[...] = jnp.full_like(m_ref, NEG_INF)

    def sweep_max(j, nb):
        kj = rows(kb_ref, j, nb)
        for t in range(2):
            note_scores(j, nb, t, scores(kj, t))

    for_blocks(i, sweep_max)
    diag = (lax.broadcasted_iota(jnp.int32, (tq, 1), 0) // CHUNK
            >= lax.broadcasted_iota(jnp.int32, (1, tq), 1) // CHUNK)
    kd = rows(kb_ref, i, 1)
    for t in range(2):
        note_scores(i, 1, t, jnp.where(diag, scores(kd, t), NEG_INF))
        m_ref[t] = _replicate(jnp.max(m_ref[t], axis=-1, keepdims=True))

    l_ref[...] = jnp.zeros_like(l_ref)
    acc_ref[...] = jnp.zeros_like(acc_ref)

    def sweep_pv(j, carry):
        vj = rows(vb_ref, j, 1)
        for t in range(2):
            p = _exp2_rows(s_ref[t, j], m_ref[t])
            acc_ref[t] += jnp.dot(p.astype(BF16), vj, preferred_element_type=F32)
            l_ref[t] += _fold(jnp.add, p)
        return carry

    lax.fori_loop(0, i + 1, sweep_pv, 0)
    outs = [acc_ref[t] * (1.0 / jnp.sum(l_ref[t], axis=-1, keepdims=True)) for t in range(2)]
    lam = _lambda(lq1_ref, lk1_ref, lq2_ref, lk2_ref, lam_init)
    o_ref[...] = _da_finish(outs[0], outs[1], lam, g_ref[...], lam_init)


def _attn_prompt(proj, n_streams, seq, lq1, lk1, lq2, lk2, da_g, lam_init):
    tq = _tile(seq, 512)
    nq = seq // tq
    dq = 2 * DA_HEAD_DIM

    def const(shape):
        return pl.BlockSpec(shape, lambda b, h, i: (0,) * len(shape))

    return pl.pallas_call(
        functools.partial(_attn_prompt_kernel, lam_init=lam_init, tq=tq),
        out_shape=jax.ShapeDtypeStruct((n_streams * seq, DA_WIDTH), BF16),
        grid=(n_streams, DA_HEADS, nq),
        in_specs=[
            pl.BlockSpec((None, tq, dq), lambda b, h, i: (3, b * nq + i, h)),
            pl.BlockSpec((None, seq, dq), lambda b, h, i: (4, b, h)),
            pl.BlockSpec((None, seq, DA_V_DIM), lambda b, h, i: (5, b, h)),
            const((1, DA_HEAD_DIM)), const((1, DA_HEAD_DIM)), const((1, DA_HEAD_DIM)), const((1, DA_HEAD_DIM)),
            const((1, DA_V_DIM)),
        ],
        out_specs=pl.BlockSpec((tq, DA_V_DIM), lambda b, h, i: (b * nq + i, h)),
        scratch_shapes=[
            pltpu.VMEM((seq, dq), BF16),
            pltpu.VMEM((seq, DA_V_DIM), BF16),
            pltpu.VMEM((2, nq, tq, tq), F32),
            pltpu.VMEM((2, tq, LANES), F32),
            pltpu.VMEM((2, tq, LANES), F32),
            pltpu.VMEM((2, tq, DA_V_DIM), F32),
        ],
        compiler_params=_params("parallel", "parallel", "arbitrary"),
        name="attn_prompt",
    )(proj, proj, proj, lq1, lk1, lq2, lk2, da_g)


def _attn_sample_kernel(q_ref, kn_ref, vn_ref, ck_ref, cv_ref, lq1_ref, lk1_ref, lq2_ref, lk2_ref, g_ref,
                        o_ref, kbuf, vbuf, sem, *, lam_init, li):
    hd = DA_HEAD_DIM
    s = pl.program_id(0)
    slot = s % 2

    def cache_copies(stream, slot_, h):
        return (pltpu.make_async_copy(ck_ref.at[li, stream, :, h, :], kbuf.at[slot_, h], sem.at[0, slot_, h]),
                pltpu.make_async_copy(cv_ref.at[li, stream, :, h, :], vbuf.at[slot_, h], sem.at[1, slot_, h]))

    def start_stream(stream, slot_):
        for h in range(DA_HEADS):
            for cp in cache_copies(stream, slot_, h):
                cp.start()

    @pl.when(s == 0)
    def _():
        start_stream(0, 0)

    @pl.when(s + 1 < pl.num_programs(0))
    def _():
        start_stream(s + 1, 1 - slot)

    for h in range(DA_HEADS):
        for cp in cache_copies(s, slot, h):
            cp.wait()

    rows = q_ref.shape[0]
    lam = _lambda(lq1_ref, lk1_ref, lq2_ref, lk2_ref, lam_init)
    for h in range(DA_HEADS):
        cols = slice(h * DA_V_DIM, (h + 1) * DA_V_DIM)
        qs = (q_ref[:, cols] * Q_SCALE).astype(BF16)
        kc, vc = kbuf[slot, h].astype(BF16), vbuf[slot, h].astype(BF16)
        kn, vn = kn_ref[:, cols].astype(BF16), vn_ref[:, cols].astype(BF16)
        pcs, pns, ls = [], [], []
        for t in range(2):
            sl = slice(t * hd, (t + 1) * hd)
            sc = lax.dot_general(qs[:, sl], kc[:, sl], NT, preferred_element_type=F32)
            sn = lax.dot_general(qs[:, sl], kn[:, sl], NT, preferred_element_type=F32)
            m = jnp.maximum(jnp.max(_fold(jnp.maximum, sc), axis=-1, keepdims=True),
                            jnp.max(sn, axis=-1, keepdims=True))
            pc = _exp2_rows(sc, _replicate(m))
            pn = jnp.exp2(sn - m)
            ls.append(jnp.sum(_fold(jnp.add, pc), axis=-1, keepdims=True) + jnp.sum(pn, axis=-1, keepdims=True))
            pcs.append(pc.astype(BF16))
            pns.append(pn.astype(BF16))
        acc = (jnp.dot(jnp.concatenate(pcs, axis=0), vc, preferred_element_type=F32)
               + jnp.dot(jnp.concatenate(pns, axis=0), vn, preferred_element_type=F32))
        outs = [acc[t * rows:(t + 1) * rows] * (1.0 / ls[t]) for t in range(2)]
        o_ref[:, cols] = _da_finish(outs[0], outs[1], lam, g_ref[...], lam_init)


def _attn_sample(proj, row0, n_streams, seq, cache_k, cache_v, li, lq1, lk1, lq2, lk2, da_g, lam_init):
    past = cache_k.shape[2]
    base = row0 // seq
    dq = 2 * DA_HEAD_DIM

    def const(shape):
        return pl.BlockSpec(shape, lambda s: (0,) * len(shape))

    def new(piece):
        return pl.BlockSpec((None, seq, PIECE), lambda s: (piece, base + s, 0))

    cache_spec = pl.BlockSpec(memory_space=pl.ANY)
    return pl.pallas_call(
        functools.partial(_attn_sample_kernel, lam_init=lam_init, li=li),
        out_shape=jax.ShapeDtypeStruct((n_streams * seq, DA_WIDTH), BF16),
        grid=(n_streams,),
        in_specs=[
            new(3), new(4), new(5), cache_spec, cache_spec,
            const((1, DA_HEAD_DIM)), const((1, DA_HEAD_DIM)), const((1, DA_HEAD_DIM)), const((1, DA_HEAD_DIM)),
            const((1, DA_V_DIM)),
        ],
        out_specs=pl.BlockSpec((seq, DA_WIDTH), lambda s: (s, 0)),
        scratch_shapes=[
            pltpu.VMEM((2, DA_HEADS, past, dq), F32),
            pltpu.VMEM((2, DA_HEADS, past, DA_V_DIM), F32),
            pltpu.SemaphoreType.DMA((2, 2, DA_HEADS)),
        ],
        compiler_params=_params("arbitrary"),
        name="attn_sample",
    )(proj, proj, proj, cache_k, cache_v, lq1, lk1, lq2, lk2, da_g)


def _outproj_kernel(x_ref, ml_ref, da_ref, w1_ref, w2_ref, post_ref, o_ref):
    y = (jnp.dot(ml_ref[...], w1_ref[...], preferred_element_type=F32)
         + jnp.dot(da_ref[...], w2_ref[...], preferred_element_type=F32))
    o_ref[...] = x_ref[...] + _rms(y) * post_ref[...]


def _outproj(x, ml_out, da_out, w_out, post_g, li):
    T, D = x.shape
    tm = _tile(T, 512)
    return pl.pallas_call(
        _outproj_kernel,
        out_shape=jax.ShapeDtypeStruct((T, D), F32),
        grid=(T // tm,),
        in_specs=[
            pl.BlockSpec((tm, D), lambda i: (i, 0)),
            pl.BlockSpec((tm, ML_WIDTH), lambda i: (i, 0)),
            pl.BlockSpec((tm, DA_WIDTH), lambda i: (i, 0)),
            pl.BlockSpec((None, ML_WIDTH, D), lambda i: (li, 0, 0)),
            pl.BlockSpec((None, DA_WIDTH, D), lambda i: (li, 1, 0)),
            pl.BlockSpec((1, D), lambda i: (0, 0)),
        ],
        out_specs=pl.BlockSpec((tm, D), lambda i: (i, 0)),
        compiler_params=_params("parallel"),
        name="outproj",
    )(x, ml_out, da_out, w_out, w_out, post_g)


def kernel(x_prompt, x_sample, cache_k, cache_v, state_C, state_n, state_m, state_conv,
           ffn1_pre_g, ffn1_wg, ffn1_wu, ffn1_wd, ffn1_post_g,
           mix_pre_g, w_in, b_i, b_f, conv_w, conv_b, ml_norm_g,
           lam_q1, lam_k1, lam_q2, lam_k2, da_norm_g, w_out, mix_post_g,
           ffn2_pre_g, ffn2_wg, ffn2_wu, ffn2_wd, ffn2_post_g):
    Bp, S, D = x_prompt.shape
    Bs, Ls, _ = x_sample.shape
    depth = w_in.shape[0]
    past = cache_k.shape[2]
    Tp, Ts = Bp * S, Bs * Ls
    assert Ls == CHUNK and past % LANES == 0 and S % CHUNK == 0 and Tp % Ls == 0
    Lp = 256 if S % 256 == 0 else CHUNK

    xp, xs = x_prompt.reshape(Tp, D), x_sample.reshape(Ts, D)
    zero_state = (jnp.zeros((1, Bp, ML_HEADS, ML_V_DIM, ML_QK_DIM), F32), jnp.zeros((1, Bp, ML_HEADS, ML_QK_DIM), F32),
                  jnp.zeros((1, Bp, 1, ML_HEADS), F32), jnp.zeros((1, Bp, CONV_W - 1, ML_CONV_CH), F32))
    stream_state = (state_C, state_n, state_m.reshape(depth, Bs, 1, ML_HEADS), state_conv)

    def row(v):
        return v.reshape(1, -1)

    w_main = jnp.concatenate([w_in[:, :, :GATE_OFF], w_in[:, :, GATE_OFF + N_GATES:]], axis=2).astype(BF16)
    w_gate = jnp.pad(w_in[:, :, GATE_OFF:GATE_OFF + N_GATES].astype(BF16), ((0, 0), (0, 0), (0, LANES - N_GATES)))
    w_o = w_out.astype(BF16)

    kv_p, kv_s, st_p, st_s = (), (), [], []
    for li in range(depth):
        lam_init = 0.8 - 0.6 * math.exp(-0.3 * li)
        ffn1 = (row(ffn1_pre_g[li]), ffn1_wg, ffn1_wu, ffn1_wd, row(ffn1_post_g[li]), li)
        ffn2 = (row(ffn2_pre_g[li]), ffn2_wg, ffn2_wu, ffn2_wd, row(ffn2_post_g[li]), li)
        bias = jnp.concatenate([b_i[li], b_f[li]])
        bias_c = jnp.pad(bias, (0, LANES - N_GATES)).reshape(1, LANES)
        bias_r = bias.reshape(N_GATES, 1)
        ml_args = (bias_c, bias_r, conv_w[li], row(conv_b[li]), row(ml_norm_g[li]))
        lam_args = (row(lam_q1[li]), row(lam_k1[li]), row(lam_q2[li]), row(lam_k2[li]), row(da_norm_g[li]), lam_init)

        xp = _ffn(xp, *ffn1)
        xs = _ffn(xs, *ffn1)
        proj_p, gates_p, *kv_p = _inproj(xp, row(mix_pre_g[li]), w_main, w_gate, li, depth, kv_p)
        proj_s, gates_s, *kv_s = _inproj(xs, row(mix_pre_g[li]), w_main, w_gate, li, depth, kv_s)
        ml_p, *st = _mlstm(proj_p, gates_p, Bp, S, Lp, *ml_args, zero_state, 0)
        st_p.append(st)
        ml_s, *st = _mlstm(proj_s, gates_s, Bs, Ls, Ls, *ml_args, stream_state, li)
        st_s.append(st)
        da_p = _attn_prompt(proj_p, Bp, S, *lam_args)
        da_s = _attn_sample(proj_s, 0, Bs, Ls, cache_k, cache_v, li, *lam_args)
        xp = _ffn(_outproj(xp, ml_p, da_p, w_o, row(mix_post_g[li]), li), *ffn2)
        xs = _ffn(_outproj(xs, ml_s, da_s, w_o, row(mix_post_g[li]), li), *ffn2)

    def states(kv, st, n_streams, seq):
        c, n, m, conv = (jnp.stack(e) for e in zip(*st))
        return (kv[0].reshape(depth, n_streams, seq, DA_HEADS, 2 * DA_HEAD_DIM),
                kv[1].reshape(depth, n_streams, seq, DA_HEADS, DA_V_DIM),
                c, n, m.reshape(depth, n_streams, ML_HEADS), conv)

    return (xp.reshape(Bp, S, D), xs.reshape(Bs, Ls, D), *states(kv_p, st_p, Bp, S), *states(kv_s, st_s, Bs, Ls))
```

```python
import functools
import math

import jax
import jax.numpy as jnp
from jax import lax
from jax.experimental import pallas as pl
from jax.experimental.pallas import tpu as pltpu

F32 = jnp.float32
BF16 = jnp.bfloat16

CHUNK = 64
ML_HEADS = 4
ML_QK_DIM = 128
ML_V_DIM = 256
DA_HEADS = 4
DA_HEAD_DIM = 128
DA_V_DIM = 2 * DA_HEAD_DIM
CONV_W = 4
EPS = 1e-6
ML_WIDTH = ML_HEADS * ML_V_DIM
DA_WIDTH = DA_HEADS * DA_V_DIM
ML_CONV_CH = 2 * ML_HEADS * ML_QK_DIM
DA_QK_WIDTH = DA_HEADS * 2 * DA_HEAD_DIM
PIECE = 1024
N_PIECES = 6
N_GATES = 2 * ML_HEADS
GATE_OFF = 3 * PIECE
assert ML_CONV_CH == ML_WIDTH == DA_QK_WIDTH == DA_WIDTH == PIECE

LANES = 128
CONV_PAD = 8
VMEM_LIMIT = 56 * 1024 * 1024

NEG_INF = float("-inf")
NT = (((1,), (1,)), ((), ()))
TN = (((0,), (0,)), ((), ()))


def _params(*sem):
    return pltpu.CompilerParams(dimension_semantics=sem, vmem_limit_bytes=VMEM_LIMIT)


def _rms(xf):
    return xf * lax.rsqrt(jnp.mean(xf * xf, axis=-1, keepdims=True) + EPS)


def _silu(x):
    return x * jax.nn.sigmoid(x)


def _log_sigmoid(x):
    return jnp.minimum(x, 0.0) - jnp.log1p(jnp.exp(-jnp.abs(x)))


def _tile(n, pref):
    return pref if n % pref == 0 else n


def _ffn_kernel(x_ref, pre_ref, wg_ref, wu_ref, wd_ref, post_ref, o_ref, xn_ref, *, tail):
    j = pl.program_id(1)
    last = pl.num_programs(1) - 1

    def slab(xn, valid):
        g = jnp.dot(xn, wg_ref[...].astype(BF16), preferred_element_type=F32)
        u = jnp.dot(xn, wu_ref[...].astype(BF16), preferred_element_type=F32)
        h = _silu(g) * u
        wd = wd_ref[...]
        if valid:
            h = jnp.where(lax.broadcasted_iota(jnp.int32, h.shape, 1) < valid, h, 0.0)
            wd = jnp.where(lax.broadcasted_iota(jnp.int32, wd.shape, 0) < valid, wd, 0.0)
        return jnp.dot(h.astype(BF16), wd.astype(BF16), preferred_element_type=F32)

    @pl.when(j == 0)
    def _():
        xn = (_rms(x_ref[...]) * pre_ref[...]).astype(BF16)
        xn_ref[...] = xn
        o_ref[...] = slab(xn, 0)

    @pl.when((j > 0) & (j < last))
    def _():
        o_ref[...] += slab(xn_ref[...], 0)

    @pl.when(j == last)
    def _():
        y = o_ref[...] + slab(xn_ref[...], tail)
        o_ref[...] = x_ref[...] + 0.5 * (_rms(y) * post_ref[...])


def _ffn(x, pre_g, wg, wu, wd, post_g, li):
    T, D = x.shape
    F = wg.shape[2]
    tm = _tile(T, 1024)
    tf = 256 if F > 256 else F
    nf = pl.cdiv(F, tf)
    assert nf >= 2
    return pl.pallas_call(
        functools.partial(_ffn_kernel, tail=F % tf),
        out_shape=jax.ShapeDtypeStruct((T, D), F32),
        grid=(T // tm, nf),
        in_specs=[
            pl.BlockSpec((tm, D), lambda i, j: (i, 0)),
            pl.BlockSpec((1, D), lambda i, j: (0, 0)),
            pl.BlockSpec((None, D, tf), lambda i, j: (li, 0, j)),
            pl.BlockSpec((None, D, tf), lambda i, j: (li, 0, j)),
            pl.BlockSpec((None, tf, D), lambda i, j: (li, j, 0)),
            pl.BlockSpec((1, D), lambda i, j: (0, 0)),
        ],
        out_specs=pl.BlockSpec((tm, D), lambda i, j: (i, 0)),
        scratch_shapes=[pltpu.VMEM((tm, D), BF16)],
        compiler_params=_params("parallel", "arbitrary"),
        name="ffn",
    )(x, pre_g, wg, wu, wd, post_g)


K_PIECE, V_PIECE = 4, 5
HALF_PIECES = N_PIECES // 2
assert HALF_PIECES * PIECE == GATE_OFF


def _inproj_kernel(*refs, li, depth):
    x_ref, pre_ref, wa_ref, wb_ref, wgate_ref = refs[:5]
    o_ref, gate_ref, k5_ref, v5_ref, xn_ref, kv_ref, sem, *zero_scratch = refs[5 + (2 if li else 0):]
    i, j = pl.program_id(0), pl.program_id(1)
    final = (i == pl.num_programs(0) - 1) & (j == pl.num_programs(1) - 1)

    def head_copies(which, dst_ref, tile):
        return [pltpu.make_async_copy(kv_ref.at[:, pl.ds(h * DA_V_DIM, DA_V_DIM)],
                                      dst_ref.at[li, tile, :, h, :], sem.at[which, h]) for h in range(DA_HEADS)]

    @pl.when(j == 0)
    def _():
        xn = (_rms(x_ref[...]) * pre_ref[...]).astype(BF16)
        xn_ref[...] = xn
        gate_ref[...] = jnp.dot(xn, wgate_ref[...], preferred_element_type=F32)
        o_ref[...] = jnp.dot(xn, wa_ref[...], preferred_element_type=F32)

    @pl.when((j > 0) & (j < HALF_PIECES))
    def _():
        o_ref[...] = jnp.dot(xn_ref[...], wa_ref[...], preferred_element_type=F32)

    @pl.when(j >= HALF_PIECES)
    def _():
        o_ref[...] = jnp.dot(xn_ref[...], wb_ref[...], preferred_element_type=F32)

    @pl.when((j == 0) & (i > 0))
    def _():
        for cp in head_copies(1, v5_ref, i - 1):
            cp.wait()

    @pl.when(j == K_PIECE)
    def _():
        kv_ref[...] = o_ref[...]
        for cp in head_copies(0, k5_ref, i):
            cp.start()

    @pl.when(j == V_PIECE)
    def _():
        for cp in head_copies(0, k5_ref, i):
            cp.wait()
        kv_ref[...] = o_ref[...]
        for cp in head_copies(1, v5_ref, i):
            cp.start()

    @pl.when(final)
    def _():
        for cp in head_copies(1, v5_ref, i):
            cp.wait()

    if zero_scratch:
        zero_ref, zero_sem = zero_scratch

        def zero_copies(tile):
            return [pltpu.make_async_copy(zero_ref, dst.at[l, tile, :, h, :], zero_sem.at[n, l - 1, h])
                    for n, dst in enumerate((k5_ref, v5_ref)) for l in range(1, depth) for h in range(DA_HEADS)]

        @pl.when((i == 0) & (j == 0))
        def _():
            zero_ref[...] = jnp.zeros_like(zero_ref)

        @pl.when(j == 1)
        def _():
            for cp in zero_copies(i):
                cp.start()

        @pl.when(j == 2)
        def _():
            for cp in zero_copies(i):
                cp.wait()


def _kv_tile(T):
    return _tile(T, 1024)


def _inproj(x, pre_g, w_a, w_b, w_gate, li, depth, kv):
    T, D = x.shape
    tm = _kv_tile(T)
    assert V_PIECE == N_PIECES - 1 and K_PIECE == V_PIECE - 1 and K_PIECE > 2
    kv_shape = jax.ShapeDtypeStruct((depth, T // tm, tm, DA_HEADS, DA_V_DIM), F32)
    any_spec = pl.BlockSpec(memory_space=pl.ANY)
    zero_fill = li == 0 and depth > 1
    assert len(kv) == (2 if li else 0)
    return pl.pallas_call(
        functools.partial(_inproj_kernel, li=li, depth=depth),
        out_shape=(jax.ShapeDtypeStruct((N_PIECES, T, PIECE), F32), jax.ShapeDtypeStruct((T, LANES), F32),
                   kv_shape, kv_shape),
        grid=(T // tm, N_PIECES),
        in_specs=[
            pl.BlockSpec((tm, D), lambda i, j: (i, 0)),
            pl.BlockSpec((1, D), lambda i, j: (0, 0)),
            pl.BlockSpec((None, D, PIECE), lambda i, j: (li, 0, jnp.minimum(j, HALF_PIECES - 1))),
            pl.BlockSpec((None, D, PIECE), lambda i, j: (li, 0, jnp.maximum(j - HALF_PIECES, 0))),
            pl.BlockSpec((None, D, LANES), lambda i, j: (li, 0, 0)),
        ] + [any_spec] * len(kv),
        out_specs=(pl.BlockSpec((None, tm, PIECE), lambda i, j: (j, i, 0)),
                   pl.BlockSpec((tm, LANES), lambda i, j: (i, 0)), any_spec, any_spec),
        scratch_shapes=[pltpu.VMEM((tm, D), BF16), pltpu.VMEM((tm, PIECE), F32),
                        pltpu.SemaphoreType.DMA((2, DA_HEADS))]
        + ([pltpu.VMEM((tm, DA_V_DIM), F32), pltpu.SemaphoreType.DMA((2, depth - 1, DA_HEADS))] if zero_fill else []),
        input_output_aliases={5: 2, 6: 3} if li else {},
        compiler_params=_params("arbitrary", "arbitrary"),
        name="inproj",
    )(x, pre_g, w_a, w_b, w_gate, *kv)


def _mlstm_kernel(qk_ref, v_ref, og_ref, gc_ref, gr_ref, bc_ref, br_ref, cw_ref, cb_ref, mlg_ref,
                  c0_ref, n0_ref, m0_ref, conv0_ref,
                  out_ref, cn_ref, nn_ref, mn_ref, convn_ref,
                  xp_ref, c_s, n_s, m_s, *, L):
    c = pl.program_id(1)

    @pl.when(c == 0)
    def _():
        c_s[...] = c0_ref[...]
        n_s[...] = n0_ref[...]
        m_s[...] = m0_ref[...]
        xp_ref[0:CONV_PAD, :] = jnp.zeros((CONV_PAD, ML_CONV_CH), F32)
        xp_ref[CONV_PAD - (CONV_W - 1):CONV_PAD, :] = conv0_ref[...]

    qk = qk_ref[...]
    xp_ref[CONV_PAD:CONV_PAD + L, :] = qk
    cw = cw_ref[...]
    conv = cb_ref[...] + qk * cw[CONV_W - 1:CONV_W, :]
    for t in range(CONV_W - 1):
        off = CONV_PAD - (CONV_W - 1) + t
        conv = conv + xp_ref[off:off + L, :] * cw[t:t + 1, :]
    tail = xp_ref[CONV_PAD + L - (CONV_W - 1):CONV_PAD + L, :]
    xp_ref[CONV_PAD - (CONV_W - 1):CONV_PAD, :] = tail
    act = _silu(conv)

    gcb = gc_ref[...] + bc_ref[...]
    grb = gr_ref[...] + br_ref[...]
    row = lax.broadcasted_iota(jnp.int32, (L, L), 0)
    col = lax.broadcasted_iota(jnp.int32, (L, L), 1)
    causal = col <= row
    b_c = jnp.dot(causal.astype(F32), _log_sigmoid(gcb), precision=lax.Precision.HIGHEST,
                  preferred_element_type=F32)
    b_r = jnp.dot(_log_sigmoid(grb), (row <= col).astype(F32), precision=lax.Precision.HIGHEST,
                  preferred_element_type=F32)

    v_all = v_ref[...]
    hs = []
    for h in range(ML_HEADS):
        q = act[:, h * ML_QK_DIM:(h + 1) * ML_QK_DIM]
        k = act[:, ML_HEADS * ML_QK_DIM + h * ML_QK_DIM:ML_HEADS * ML_QK_DIM + (h + 1) * ML_QK_DIM]
        k = k * (ML_QK_DIM ** -0.5)
        qb = q.astype(BF16)
        vb = v_all[:, h * ML_V_DIM:(h + 1) * ML_V_DIM].astype(BF16)
        ig_r = grb[h:h + 1, :]
        ig_c = gcb[:, h:h + 1]
        bh_r = b_r[ML_HEADS + h:ML_HEADS + h + 1, :]
        bh_c = b_c[:, ML_HEADS + h:ML_HEADS + h + 1]
        m_prev = m_s[:, h:h + 1]
        n_prev = n_s[h:h + 1, :]
        c_prev = c_s[h]

        d = jnp.where(causal, bh_c - bh_r + ig_r, NEG_INF)
        inter = bh_c + m_prev
        m_t = jnp.maximum(inter, jnp.max(d, axis=-1, keepdims=True))
        w_intra = jnp.exp(d - m_t)
        w_inter = jnp.exp(inter - m_t)
        s = lax.dot_general(qb, k.astype(BF16), NT, preferred_element_type=F32) * w_intra
        num = (jnp.dot(s.astype(BF16), vb, preferred_element_type=F32)
               + w_inter * lax.dot_general(qb, c_prev.astype(BF16), NT, preferred_element_type=F32))
        den = (jnp.sum(s, axis=-1, keepdims=True)
               + w_inter * jnp.sum(q * n_prev, axis=-1, keepdims=True))
        hs.append(num * (1.0 / jnp.maximum(jnp.abs(den), jnp.exp(-m_t))))

        m_new = m_t[L - 1:L, :]
        b_last = bh_c[L - 1:L, :]
        w_state = jnp.exp(b_last - bh_c + ig_c - m_new)
        decay = jnp.exp(b_last + m_prev - m_new)
        kw = k * w_state
        c_s[h] = decay * c_prev + lax.dot_general(vb, kw.astype(BF16), TN, preferred_element_type=F32)
        n_s[h:h + 1, :] = decay * n_prev + jnp.sum(kw, axis=0, keepdims=True)
        m_s[:, h:h + 1] = m_new

    gated = jax.nn.sigmoid(og_ref[...]) * jnp.concatenate(hs, axis=-1)
    out_ref[...] = (_rms(gated) * mlg_ref[...]).astype(BF16)

    @pl.when(c == pl.num_programs(1) - 1)
    def _():
        cn_ref[...] = c_s[...]
        nn_ref[...] = n_s[...]
        mn_ref[...] = m_s[...]
        convn_ref[...] = tail


def _mlstm(proj, gates, n_streams, seq, L, bias_c, bias_r, conv_w, conv_b, ml_g, state0, li0):
    nc = seq // L
    rows = n_streams * seq
    g_rows = gates[:, :N_GATES].reshape(n_streams * nc, L, N_GATES).transpose(0, 2, 1)

    def tok(piece):
        return pl.BlockSpec((None, L, PIECE), lambda s, c: (piece, s * nc + c, 0))

    def const(shape):
        return pl.BlockSpec(shape, lambda s, c: (0,) * len(shape))

    state_dims = ((ML_HEADS, ML_V_DIM, ML_QK_DIM), (ML_HEADS, ML_QK_DIM), (1, ML_HEADS), (CONV_W - 1, ML_CONV_CH))

    state_in = [pl.BlockSpec((None, None) + d, lambda s, c, n=len(d): (li0, s) + (0,) * n) for d in state_dims]
    state_out = [pl.BlockSpec((None,) + d, lambda s, c, n=len(d): (s,) + (0,) * n) for d in state_dims]
    return pl.pallas_call(
        functools.partial(_mlstm_kernel, L=L),
        out_shape=(jax.ShapeDtypeStruct((rows, ML_WIDTH), BF16),
                   *(jax.ShapeDtypeStruct((n_streams,) + d, F32) for d in state_dims)),
        grid=(n_streams, nc),
        in_specs=[
            tok(0), tok(1), tok(2),
            pl.BlockSpec((L, LANES), lambda s, c: (s * nc + c, 0)),
            pl.BlockSpec((None, N_GATES, L), lambda s, c: (s * nc + c, 0, 0)),
            const((1, LANES)), const((N_GATES, 1)), const((CONV_W, ML_CONV_CH)), const((1, ML_CONV_CH)),
            const((1, ML_WIDTH)),
            *state_in,
        ],
        out_specs=(pl.BlockSpec((L, ML_WIDTH), lambda s, c: (s * nc + c, 0)), *state_out),
        scratch_shapes=[
            pltpu.VMEM((CONV_PAD + L, ML_CONV_CH), F32),
            pltpu.VMEM((ML_HEADS, ML_V_DIM, ML_QK_DIM), F32),
            pltpu.VMEM((ML_HEADS, ML_QK_DIM), F32),
            pltpu.VMEM((1, ML_HEADS), F32),
        ],
        compiler_params=_params("arbitrary", "arbitrary"),
        name="mlstm",
    )(proj, proj, proj, gates, g_rows, bias_c, bias_r, conv_w, conv_b, ml_g, *state0)


def _lambda(lq1_ref, lk1_ref, lq2_ref, lk2_ref, lam_init):
    return (jnp.exp(jnp.sum(lq1_ref[...] * lk1_ref[...], axis=-1, keepdims=True))
            - jnp.exp(jnp.sum(lq2_ref[...] * lk2_ref[...], axis=-1, keepdims=True)) + lam_init)


def _lane_tiles(x):
    return [x[:, c * LANES:(c + 1) * LANES] for c in range(x.shape[1] // LANES)]


def _fold(op, x):
    return functools.reduce(op, _lane_tiles(x))


def _exp2_rows(s, m_b):
    return jnp.concatenate([jnp.exp2(t - m_b) for t in _lane_tiles(s)], axis=-1)


def _replicate(col):
    return jnp.broadcast_to(col, (col.shape[0], LANES))


def _da_finish(out1, out2, lam, g, lam_init):
    att = out1 - lam * out2
    return ((_rms(att) * g) * (1.0 - lam_init)).astype(BF16)


LOG2E = math.log2(math.e)
Q_SCALE = DA_HEAD_DIM ** -0.5 * LOG2E


def _attn_prompt_kernel(q_ref, k_ref, v_ref, lq1_ref, lk1_ref, lq2_ref, lk2_ref, g_ref, o_ref,
                        kb_ref, vb_ref, s_ref, m_ref, l_ref, acc_ref, *, lam_init, tq):
    i = pl.program_id(2)
    hd = DA_HEAD_DIM

    @pl.when(i == 0)
    def _():
        kb_ref[...] = k_ref[...].astype(BF16)
        vb_ref[...] = v_ref[...].astype(BF16)

    qs = (q_ref[...] * Q_SCALE).astype(BF16)

    def rows(ref, j, nb):
        return ref[pl.ds(pl.multiple_of(j * tq, tq), nb * tq), :]

    def scores(kj, t):
        return lax.dot_general(qs[:, t * hd:(t + 1) * hd], kj[:, t * hd:(t + 1) * hd], NT,
                               preferred_element_type=F32)

    def note_scores(j, nb, t, s):
        for b in range(nb):
            s_ref[t, j + b] = s[:, b * tq:(b + 1) * tq]
        m_ref[t] = jnp.maximum(m_ref[t], _fold(jnp.maximum, s))

    def for_blocks(n, fn):
        def pair(k, carry):
            fn(2 * k, 2)
            return carry

        lax.fori_loop(0, n // 2, pair, 0)
        pl.when(n % 2 == 1)(lambda: fn(n - 1, 1))

    m_ref[...] = jnp.full_like(m_ref, NEG_INF)

    def sweep_max(j, nb):
        kj = rows(kb_ref, j, nb)
        for t in range(2):
            note_scores(j, nb, t, scores(kj, t))

    for_blocks(i, sweep_max)
    diag = (lax.broadcasted_iota(jnp.int32, (tq, 1), 0) // CHUNK
            >= lax.broadcasted_iota(jnp.int32, (1, tq), 1) // CHUNK)
    kd = rows(kb_ref, i, 1)
    for t in range(2):
        note_scores(i, 1, t, jnp.where(diag, scores(kd, t), NEG_INF))
        m_ref[t] = _replicate(jnp.max(m_ref[t], axis=-1, keepdims=True))

    l_ref[...] = jnp.zeros_like(l_ref)
    acc_ref[...] = jnp.zeros_like(acc_ref)

    def sweep_pv(j, carry):
        vj = rows(vb_ref, j, 1)
        for t in range(2):
            p = _exp2_rows(s_ref[t, j], m_ref[t])
            acc_ref[t] += jnp.dot(p.astype(BF16), vj, preferred_element_type=F32)
            l_ref[t] += _fold(jnp.add, p)
        return carry

    lax.fori_loop(0, i + 1, sweep_pv, 0)
    outs = [acc_ref[t] * (1.0 / jnp.sum(l_ref[t], axis=-1, keepdims=True)) for t in range(2)]
    lam = _lambda(lq1_ref, lk1_ref, lq2_ref, lk2_ref, lam_init)
    o_ref[...] = _da_finish(outs[0], outs[1], lam, g_ref[...], lam_init)


def _attn_prompt(proj, n_streams, seq, lq1, lk1, lq2, lk2, da_g, lam_init):
    tq = _tile(seq, 512)
    nq = seq // tq
    dq = 2 * DA_HEAD_DIM

    def const(shape):
        return pl.BlockSpec(shape, lambda b, h, i: (0,) * len(shape))

    return pl.pallas_call(
        functools.partial(_attn_prompt_kernel, lam_init=lam_init, tq=tq),
        out_shape=jax.ShapeDtypeStruct((n_streams * seq, DA_WIDTH), BF16),
        grid=(n_streams, DA_HEADS, nq),
        in_specs=[
            pl.BlockSpec((None, tq, dq), lambda b, h, i: (3, b * nq + i, h)),
            pl.BlockSpec((None, seq, dq), lambda b, h, i: (4, b, h)),
            pl.BlockSpec((None, seq, DA_V_DIM), lambda b, h, i: (5, b, h)),
            const((1, DA_HEAD_DIM)), const((1, DA_HEAD_DIM)), const((1, DA_HEAD_DIM)), const((1, DA_HEAD_DIM)),
            const((1, DA_V_DIM)),
        ],
        out_specs=pl.BlockSpec((tq, DA_V_DIM), lambda b, h, i: (b * nq + i, h)),
        scratch_shapes=[
            pltpu.VMEM((seq, dq), BF16),
            pltpu.VMEM((seq, DA_V_DIM), BF16),
            pltpu.VMEM((2, nq, tq, tq), F32),
            pltpu.VMEM((2, tq, LANES), F32),
            pltpu.VMEM((2, tq, LANES), F32),
            pltpu.VMEM((2, tq, DA_V_DIM), F32),
        ],
        compiler_params=_params("parallel", "parallel", "arbitrary"),
        name="attn_prompt",
    )(proj, proj, proj, lq1, lk1, lq2, lk2, da_g)


def _attn_sample_kernel(q_ref, kn_ref, vn_ref, ck_ref, cv_ref, lq1_ref, lk1_ref, lq2_ref, lk2_ref, g_ref,
                        o_ref, kbuf, vbuf, sem, *, lam_init, li):
    hd = DA_HEAD_DIM
    s = pl.program_id(0)
    slot = s % 2

    def cache_copies(stream, slot_, h):
        return (pltpu.make_async_copy(ck_ref.at[li, stream, :, h, :], kbuf.at[slot_, h], sem.at[0, slot_, h]),
                pltpu.make_async_copy(cv_ref.at[li, stream, :, h, :], vbuf.at[slot_, h], sem.at[1, slot_, h]))

    def start_stream(stream, slot_):
        for h in range(DA_HEADS):
            for cp in cache_copies(stream, slot_, h):
                cp.start()

    @pl.when(s == 0)
    def _():
        start_stream(0, 0)

    @pl.when(s + 1 < pl.num_programs(0))
    def _():
        start_stream(s + 1, 1 - slot)

    for h in range(DA_HEADS):
        for cp in cache_copies(s, slot, h):
            cp.wait()

    rows = q_ref.shape[0]
    lam = _lambda(lq1_ref, lk1_ref, lq2_ref, lk2_ref, lam_init)
    for h in range(DA_HEADS):
        cols = slice(h * DA_V_DIM, (h + 1) * DA_V_DIM)
        qs = (q_ref[:, cols] * Q_SCALE).astype(BF16)
        kc, vc = kbuf[slot, h].astype(BF16), vbuf[slot, h].astype(BF16)
        kn, vn = kn_ref[:, cols].astype(BF16), vn_ref[:, cols].astype(BF16)
        pcs, pns, ls = [], [], []
        for t in range(2):
            sl = slice(t * hd, (t + 1) * hd)
            sc = lax.dot_general(qs[:, sl], kc[:, sl], NT, preferred_element_type=F32)
            sn = lax.dot_general(qs[:, sl], kn[:, sl], NT, preferred_element_type=F32)
            m = jnp.maximum(jnp.max(_fold(jnp.maximum, sc), axis=-1, keepdims=True),
                            jnp.max(sn, axis=-1, keepdims=True))
            pc = _exp2_rows(sc, _replicate(m))
            pn = jnp.exp2(sn - m)
            ls.append(jnp.sum(_fold(jnp.add, pc), axis=-1, keepdims=True) + jnp.sum(pn, axis=-1, keepdims=True))
            pcs.append(pc.astype(BF16))
            pns.append(pn.astype(BF16))
        acc = (jnp.dot(jnp.concatenate(pcs, axis=0), vc, preferred_element_type=F32)
               + jnp.dot(jnp.concatenate(pns, axis=0), vn, preferred_element_type=F32))
        outs = [acc[t * rows:(t + 1) * rows] * (1.0 / ls[t]) for t in range(2)]
        o_ref[:, cols] = _da_finish(outs[0], outs[1], lam, g_ref[...], lam_init)


def _attn_sample(proj, n_streams, seq, cache_k, cache_v, li, lq1, lk1, lq2, lk2, da_g, lam_init):
    past = cache_k.shape[2]
    dq = 2 * DA_HEAD_DIM

    def const(shape):
        return pl.BlockSpec(shape, lambda s: (0,) * len(shape))

    def new(piece):
        return pl.BlockSpec((None, seq, PIECE), lambda s: (piece, s, 0))

    cache_spec = pl.BlockSpec(memory_space=pl.ANY)
    return pl.pallas_call(
        functools.partial(_attn_sample_kernel, lam_init=lam_init, li=li),
        out_shape=jax.ShapeDtypeStruct((n_streams * seq, DA_WIDTH), BF16),
        grid=(n_streams,),
        in_specs=[
            new(3), new(4), new(5), cache_spec, cache_spec,
            const((1, DA_HEAD_DIM)), const((1, DA_HEAD_DIM)), const((1, DA_HEAD_DIM)), const((1, DA_HEAD_DIM)),
            const((1, DA_V_DIM)),
        ],
        out_specs=pl.BlockSpec((seq, DA_WIDTH), lambda s: (s, 0)),
        scratch_shapes=[
            pltpu.VMEM((2, DA_HEADS, past, dq), F32),
            pltpu.VMEM((2, DA_HEADS, past, DA_V_DIM), F32),
            pltpu.SemaphoreType.DMA((2, 2, DA_HEADS)),
        ],
        compiler_params=_params("arbitrary"),
        name="attn_sample",
    )(proj, proj, proj, cache_k, cache_v, lq1, lk1, lq2, lk2, da_g)


def _outproj_kernel(x_ref, ml_ref, da_ref, w1_ref, w2_ref, post_ref, o_ref):
    y = (jnp.dot(ml_ref[...], w1_ref[...], preferred_element_type=F32)
         + jnp.dot(da_ref[...], w2_ref[...], preferred_element_type=F32))
    o_ref[...] = x_ref[...] + _rms(y) * post_ref[...]


def _outproj(x, ml_out, da_out, w_out, post_g, li):
    T, D = x.shape
    tm = _tile(T, 512)
    return pl.pallas_call(
        _outproj_kernel,
        out_shape=jax.ShapeDtypeStruct((T, D), F32),
        grid=(T // tm,),
        in_specs=[
            pl.BlockSpec((tm, D), lambda i: (i, 0)),
            pl.BlockSpec((tm, ML_WIDTH), lambda i: (i, 0)),
            pl.BlockSpec((tm, DA_WIDTH), lambda i: (i, 0)),
            pl.BlockSpec((None, ML_WIDTH, D), lambda i: (li, 0, 0)),
            pl.BlockSpec((None, DA_WIDTH, D), lambda i: (li, 1, 0)),
            pl.BlockSpec((1, D), lambda i: (0, 0)),
        ],
        out_specs=pl.BlockSpec((tm, D), lambda i: (i, 0)),
        compiler_params=_params("parallel"),
        name="outproj",
    )(x, ml_out, da_out, w_out, w_out, post_g)


def kernel(x_prompt, x_sample, cache_k, cache_v, state_C, state_n, state_m, state_conv,
           ffn1_pre_g, ffn1_wg, ffn1_wu, ffn1_wd, ffn1_post_g,
           mix_pre_g, w_in, b_i, b_f, conv_w, conv_b, ml_norm_g,
           lam_q1, lam_k1, lam_q2, lam_k2, da_norm_g, w_out, mix_post_g,
           ffn2_pre_g, ffn2_wg, ffn2_wu, ffn2_wd, ffn2_post_g):
    Bp, S, D = x_prompt.shape
    Bs, Ls, _ = x_sample.shape
    depth = w_in.shape[0]
    past = cache_k.shape[2]
    Tp, Ts = Bp * S, Bs * Ls
    assert Ls == CHUNK and past % LANES == 0 and S % CHUNK == 0
    Lp = 256 if S % 256 == 0 else CHUNK

    xp, xs = x_prompt.reshape(Tp, D), x_sample.reshape(Ts, D)
    zero_state = (jnp.zeros((1, Bp, ML_HEADS, ML_V_DIM, ML_QK_DIM), F32), jnp.zeros((1, Bp, ML_HEADS, ML_QK_DIM), F32),
                  jnp.zeros((1, Bp, 1, ML_HEADS), F32), jnp.zeros((1, Bp, CONV_W - 1, ML_CONV_CH), F32))
    stream_state = (state_C, state_n, state_m.reshape(depth, Bs, 1, ML_HEADS), state_conv)

    def row(v):
        return v.reshape(1, -1)

    w_a = w_in[:, :, :GATE_OFF].astype(BF16)
    w_b = w_in[:, :, GATE_OFF + N_GATES:].astype(BF16)
    w_gate = jnp.pad(w_in[:, :, GATE_OFF:GATE_OFF + N_GATES].astype(BF16), ((0, 0), (0, 0), (0, LANES - N_GATES)))
    w_o = w_out.astype(BF16)

    kv_p, kv_s, st_p, st_s = (), (), [], []
    for li in range(depth):
        lam_init = 0.8 - 0.6 * math.exp(-0.3 * li)
        ffn1 = (row(ffn1_pre_g[li]), ffn1_wg, ffn1_wu, ffn1_wd, row(ffn1_post_g[li]), li)
        ffn2 = (row(ffn2_pre_g[li]), ffn2_wg, ffn2_wu, ffn2_wd, row(ffn2_post_g[li]), li)
        bias = jnp.concatenate([b_i[li], b_f[li]])
        bias_c = jnp.pad(bias, (0, LANES - N_GATES)).reshape(1, LANES)
        bias_r = bias.reshape(N_GATES, 1)
        ml_args = (bias_c, bias_r, conv_w[li], row(conv_b[li]), row(ml_norm_g[li]))
        lam_args = (row(lam_q1[li]), row(lam_k1[li]), row(lam_q2[li]), row(lam_k2[li]), row(da_norm_g[li]), lam_init)

        xp = _ffn(xp, *ffn1)
        xs = _ffn(xs, *ffn1)
        proj_p, gates_p, *kv_p = _inproj(xp, row(mix_pre_g[li]), w_a, w_b, w_gate, li, depth, kv_p)
        proj_s, gates_s, *kv_s = _inproj(xs, row(mix_pre_g[li]), w_a, w_b, w_gate, li, depth, kv_s)
        ml_p, *st = _mlstm(proj_p, gates_p, Bp, S, Lp, *ml_args, zero_state, 0)
        st_p.append(st)
        ml_s, *st = _mlstm(proj_s, gates_s, Bs, Ls, Ls, *ml_args, stream_state, li)
        st_s.append(st)
        da_p = _attn_prompt(proj_p, Bp, S, *lam_args)
        da_s = _attn_sample(proj_s, Bs, Ls, cache_k, cache_v, li, *lam_args)
        xp = _ffn(_outproj(xp, ml_p, da_p, w_o, row(mix_post_g[li]), li), *ffn2)
        xs = _ffn(_outproj(xs, ml_s, da_s, w_o, row(mix_post_g[li]), li), *ffn2)

    def states(kv, st, n_streams, seq):
        c, n, m, conv = (jnp.stack(e) for e in zip(*st))
        return (kv[0].reshape(depth, n_streams, seq, DA_HEADS, 2 * DA_HEAD_DIM),
                kv[1].reshape(depth, n_streams, seq, DA_HEADS, DA_V_DIM),
                c, n, m.reshape(depth, n_streams, ML_HEADS), conv)

    return (xp.reshape(Bp, S, D), xs.reshape(Bs, Ls, D), *states(kv_p, st_p, Bp, S), *states(kv_s, st_s, Bs, Ls))
```

```python
import functools
import math

import jax
import jax.numpy as jnp
from jax import lax
from jax.experimental import pallas as pl
from jax.experimental.pallas import tpu as pltpu

F32 = jnp.float32
BF16 = jnp.bfloat16

CHUNK = 64
ML_HEADS = 4
ML_QK_DIM = 128
ML_V_DIM = 256
DA_HEADS = 4
DA_HEAD_DIM = 128
DA_V_DIM = 2 * DA_HEAD_DIM
CONV_W = 4
EPS = 1e-6
ML_WIDTH = ML_HEADS * ML_V_DIM
DA_WIDTH = DA_HEADS * DA_V_DIM
ML_CONV_CH = 2 * ML_HEADS * ML_QK_DIM
DA_QK_WIDTH = DA_HEADS * 2 * DA_HEAD_DIM
PIECE = 1024
N_PIECES = 6
N_GATES = 2 * ML_HEADS
GATE_OFF = 3 * PIECE
assert ML_CONV_CH == ML_WIDTH == DA_QK_WIDTH == DA_WIDTH == PIECE

LANES = 128
CONV_PAD = 8
VMEM_LIMIT = 56 * 1024 * 1024

NEG_INF = float("-inf")
NT = (((1,), (1,)), ((), ()))
TN = (((0,), (0,)), ((), ()))


def _params(*sem):
    return pltpu.CompilerParams(dimension_semantics=sem, vmem_limit_bytes=VMEM_LIMIT)


def _rms(xf):
    return xf * lax.rsqrt(jnp.mean(xf * xf, axis=-1, keepdims=True) + EPS)


def _silu(x):
    return x * jax.nn.sigmoid(x)


def _log_sigmoid(x):
    return jnp.minimum(x, 0.0) - jnp.log1p(jnp.exp(-jnp.abs(x)))


def _tile(n, pref):
    return pref if n % pref == 0 else n


def _ffn_kernel(x_ref, pre_ref, wg_ref, wu_ref, wd_ref, post_ref, o_ref, xn_ref, *, tail):
    j = pl.program_id(1)
    last = pl.num_programs(1) - 1

    def slab(xn, valid):
        g = jnp.dot(xn, wg_ref[...].astype(BF16), preferred_element_type=F32)
        u = jnp.dot(xn, wu_ref[...].astype(BF16), preferred_element_type=F32)
        h = _silu(g) * u
        wd = wd_ref[...]
        if valid:
            h = jnp.where(lax.broadcasted_iota(jnp.int32, h.shape, 1) < valid, h, 0.0)
            wd = jnp.where(lax.broadcasted_iota(jnp.int32, wd.shape, 0) < valid, wd, 0.0)
        return jnp.dot(h.astype(BF16), wd.astype(BF16), preferred_element_type=F32)

    @pl.when(j == 0)
    def _():
        xn = (_rms(x_ref[...]) * pre_ref[...]).astype(BF16)
        xn_ref[...] = xn
        o_ref[...] = slab(xn, 0)

    @pl.when((j > 0) & (j < last))
    def _():
        o_ref[...] += slab(xn_ref[...], 0)

    @pl.when(j == last)
    def _():
        y = o_ref[...] + slab(xn_ref[...], tail)
        o_ref[...] = x_ref[...] + 0.5 * (_rms(y) * post_ref[...])


def _ffn(x, pre_g, wg, wu, wd, post_g, li):
    T, D = x.shape
    F = wg.shape[2]
    tm = _tile(T, 1024)
    tf = 256 if F > 256 else F
    nf = pl.cdiv(F, tf)
    assert nf >= 2
    return pl.pallas_call(
        functools.partial(_ffn_kernel, tail=F % tf),
        out_shape=jax.ShapeDtypeStruct((T, D), F32),
        grid=(T // tm, nf),
        in_specs=[
            pl.BlockSpec((tm, D), lambda i, j: (i, 0)),
            pl.BlockSpec((1, D), lambda i, j: (0, 0)),
            pl.BlockSpec((None, D, tf), lambda i, j: (li, 0, j)),
            pl.BlockSpec((None, D, tf), lambda i, j: (li, 0, j)),
            pl.BlockSpec((None, tf, D), lambda i, j: (li, j, 0)),
            pl.BlockSpec((1, D), lambda i, j: (0, 0)),
        ],
        out_specs=pl.BlockSpec((tm, D), lambda i, j: (i, 0)),
        scratch_shapes=[pltpu.VMEM((tm, D), BF16)],
        compiler_params=_params("parallel", "arbitrary"),
        name="ffn",
    )(x, pre_g, wg, wu, wd, post_g)


K_PIECE, V_PIECE = 4, 5
HALF_PIECES = N_PIECES // 2
assert HALF_PIECES * PIECE == GATE_OFF


def _inproj_kernel(*refs, li, depth):
    x_ref, pre_ref, wa_ref, wb_ref, wgate_ref = refs[:5]
    o_ref, gate_ref, k5_ref, v5_ref, xn_ref, kv_ref, sem, *zero_scratch = refs[5 + (2 if li else 0):]
    i, j = pl.program_id(0), pl.program_id(1)
    final = (i == pl.num_programs(0) - 1) & (j == pl.num_programs(1) - 1)

    def head_copies(which, dst_ref, tile):
        return [pltpu.make_async_copy(kv_ref.at[:, pl.ds(h * DA_V_DIM, DA_V_DIM)],
                                      dst_ref.at[li, tile, :, h, :], sem.at[which, h]) for h in range(DA_HEADS)]

    @pl.when(j == 0)
    def _():
        xn = (_rms(x_ref[...]) * pre_ref[...]).astype(BF16)
        xn_ref[...] = xn
        gate_ref[...] = jnp.dot(xn, wgate_ref[...], preferred_element_type=F32)
        o_ref[...] = jnp.dot(xn, wa_ref[...], preferred_element_type=F32)

    @pl.when((j > 0) & (j < HALF_PIECES))
    def _():
        o_ref[...] = jnp.dot(xn_ref[...], wa_ref[...], preferred_element_type=F32)

    @pl.when(j >= HALF_PIECES)
    def _():
        o_ref[...] = jnp.dot(xn_ref[...], wb_ref[...], preferred_element_type=F32)

    @pl.when((j == 0) & (i > 0))
    def _():
        for cp in head_copies(1, v5_ref, i - 1):
            cp.wait()

    @pl.when(j == K_PIECE)
    def _():
        kv_ref[...] = o_ref[...]
        for h, cp in enumerate(head_copies(0, k5_ref, i)):
            cp.start(priority=h % 2)

    @pl.when(j == V_PIECE)
    def _():
        for cp in head_copies(0, k5_ref, i):
            cp.wait()
        kv_ref[...] = o_ref[...]
        for h, cp in enumerate(head_copies(1, v5_ref, i)):
            cp.start(priority=h % 2)

    @pl.when(final)
    def _():
        for cp in head_copies(1, v5_ref, i):
            cp.wait()

    if zero_scratch:
        zero_ref, zero_sem = zero_scratch

        def zero_copies(tile):
            return [pltpu.make_async_copy(zero_ref, dst.at[l, tile, :, h, :], zero_sem.at[n, l - 1, h])
                    for n, dst in enumerate((k5_ref, v5_ref)) for l in range(1, depth) for h in range(DA_HEADS)]

        @pl.when((i == 0) & (j == 0))
        def _():
            zero_ref[...] = jnp.zeros_like(zero_ref)

        @pl.when(j == 1)
        def _():
            for cp in zero_copies(i):
                cp.start()

        @pl.when(j == 2)
        def _():
            for cp in zero_copies(i):
                cp.wait()


def _kv_tile(T):
    return _tile(T, 1024)


def _inproj(x, pre_g, w_a, w_b, w_gate, li, depth, kv):
    T, D = x.shape
    tm = _kv_tile(T)
    assert V_PIECE == N_PIECES - 1 and K_PIECE == V_PIECE - 1 and K_PIECE > 2
    kv_shape = jax.ShapeDtypeStruct((depth, T // tm, tm, DA_HEADS, DA_V_DIM), F32)
    any_spec = pl.BlockSpec(memory_space=pl.ANY)
    zero_fill = li == 0 and depth > 1
    assert len(kv) == (2 if li else 0)
    return pl.pallas_call(
        functools.partial(_inproj_kernel, li=li, depth=depth),
        out_shape=(jax.ShapeDtypeStruct((N_PIECES, T, PIECE), F32), jax.ShapeDtypeStruct((T, LANES), F32),
                   kv_shape, kv_shape),
        grid=(T // tm, N_PIECES),
        in_specs=[
            pl.BlockSpec((tm, D), lambda i, j: (i, 0)),
            pl.BlockSpec((1, D), lambda i, j: (0, 0)),
            pl.BlockSpec((None, D, PIECE), lambda i, j: (li, 0, jnp.minimum(j, HALF_PIECES - 1))),
            pl.BlockSpec((None, D, PIECE), lambda i, j: (li, 0, jnp.maximum(j - HALF_PIECES, 0))),
            pl.BlockSpec((None, D, LANES), lambda i, j: (li, 0, 0)),
        ] + [any_spec] * len(kv),
        out_specs=(pl.BlockSpec((None, tm, PIECE), lambda i, j: (j, i, 0)),
                   pl.BlockSpec((tm, LANES), lambda i, j: (i, 0)), any_spec, any_spec),
        scratch_shapes=[pltpu.VMEM((tm, D), BF16), pltpu.VMEM((tm, PIECE), F32),
                        pltpu.SemaphoreType.DMA((2, DA_HEADS))]
        + ([pltpu.VMEM((tm, DA_V_DIM), F32), pltpu.SemaphoreType.DMA((2, depth - 1, DA_HEADS))] if zero_fill else []),
        input_output_aliases={5: 2, 6: 3} if li else {},
        compiler_params=_params("arbitrary", "arbitrary"),
        name="inproj",
    )(x, pre_g, w_a, w_b, w_gate, *kv)


def _mlstm_kernel(qk_ref, v_ref, og_ref, gc_ref, gr_ref, bc_ref, br_ref, cw_ref, cb_ref, mlg_ref,
                  c0_ref, n0_ref, m0_ref, conv0_ref,
                  out_ref, cn_ref, nn_ref, mn_ref, convn_ref,
                  xp_ref, c_s, n_s, m_s, *, L):
    c = pl.program_id(1)

    @pl.when(c == 0)
    def _():
        c_s[...] = c0_ref[...]
        n_s[...] = n0_ref[...]
        m_s[...] = m0_ref[...]
        xp_ref[0:CONV_PAD, :] = jnp.zeros((CONV_PAD, ML_CONV_CH), F32)
        xp_ref[CONV_PAD - (CONV_W - 1):CONV_PAD, :] = conv0_ref[...]

    qk = qk_ref[...]
    xp_ref[CONV_PAD:CONV_PAD + L, :] = qk
    cw = cw_ref[...]
    conv = cb_ref[...] + qk * cw[CONV_W - 1:CONV_W, :]
    for t in range(CONV_W - 1):
        off = CONV_PAD - (CONV_W - 1) + t
        conv = conv + xp_ref[off:off + L, :] * cw[t:t + 1, :]
    tail = xp_ref[CONV_PAD + L - (CONV_W - 1):CONV_PAD + L, :]
    xp_ref[CONV_PAD - (CONV_W - 1):CONV_PAD, :] = tail
    act = _silu(conv)

    gcb = gc_ref[...] + bc_ref[...]
    grb = gr_ref[...] + br_ref[...]
    row = lax.broadcasted_iota(jnp.int32, (L, L), 0)
    col = lax.broadcasted_iota(jnp.int32, (L, L), 1)
    causal = col <= row
    b_c = jnp.dot(causal.astype(F32), _log_sigmoid(gcb), precision=lax.Precision.HIGHEST,
                  preferred_element_type=F32)
    b_r = jnp.dot(_log_sigmoid(grb), (row <= col).astype(F32), precision=lax.Precision.HIGHEST,
                  preferred_element_type=F32)

    v_all = v_ref[...]
    hs = []
    for h in range(ML_HEADS):
        q = act[:, h * ML_QK_DIM:(h + 1) * ML_QK_DIM]
        k = act[:, ML_HEADS * ML_QK_DIM + h * ML_QK_DIM:ML_HEADS * ML_QK_DIM + (h + 1) * ML_QK_DIM]
        k = k * (ML_QK_DIM ** -0.5)
        qb = q.astype(BF16)
        vb = v_all[:, h * ML_V_DIM:(h + 1) * ML_V_DIM].astype(BF16)
        ig_r = grb[h:h + 1, :]
        ig_c = gcb[:, h:h + 1]
        bh_r = b_r[ML_HEADS + h:ML_HEADS + h + 1, :]
        bh_c = b_c[:, ML_HEADS + h:ML_HEADS + h + 1]
        m_prev = m_s[:, h:h + 1]
        n_prev = n_s[h:h + 1, :]
        c_prev = c_s[h]

        d = jnp.where(causal, bh_c - bh_r + ig_r, NEG_INF)
        inter = bh_c + m_prev
        m_t = jnp.maximum(inter, jnp.max(d, axis=-1, keepdims=True))
        w_intra = jnp.exp(d - m_t)
        w_inter = jnp.exp(inter - m_t)
        s = lax.dot_general(qb, k.astype(BF16), NT, preferred_element_type=F32) * w_intra
        num = (jnp.dot(s.astype(BF16), vb, preferred_element_type=F32)
               + w_inter * lax.dot_general(qb, c_prev.astype(BF16), NT, preferred_element_type=F32))
        den = (jnp.sum(s, axis=-1, keepdims=True)
               + w_inter * jnp.sum(q * n_prev, axis=-1, keepdims=True))
        hs.append(num * (1.0 / jnp.maximum(jnp.abs(den), jnp.exp(-m_t))))

        m_new = m_t[L - 1:L, :]
        b_last = bh_c[L - 1:L, :]
        w_state = jnp.exp(b_last - bh_c + ig_c - m_new)
        decay = jnp.exp(b_last + m_prev - m_new)
        kw = k * w_state
        c_s[h] = decay * c_prev + lax.dot_general(vb, kw.astype(BF16), TN, preferred_element_type=F32)
        n_s[h:h + 1, :] = decay * n_prev + jnp.sum(kw, axis=0, keepdims=True)
        m_s[:, h:h + 1] = m_new

    gated = jax.nn.sigmoid(og_ref[...]) * jnp.concatenate(hs, axis=-1)
    out_ref[...] = (_rms(gated) * mlg_ref[...]).astype(BF16)

    @pl.when(c == pl.num_programs(1) - 1)
    def _():
        cn_ref[...] = c_s[...]
        nn_ref[...] = n_s[...]
        mn_ref[...] = m_s[...]
        convn_ref[...] = tail


def _mlstm(proj, gates, n_streams, seq, L, bias_c, bias_r, conv_w, conv_b, ml_g, state0, li0):
    nc = seq // L
    rows = n_streams * seq
    g_rows = gates[:, :N_GATES].reshape(n_streams * nc, L, N_GATES).transpose(0, 2, 1)

    def tok(piece):
        return pl.BlockSpec((None, L, PIECE), lambda s, c: (piece, s * nc + c, 0))

    def const(shape):
        return pl.BlockSpec(shape, lambda s, c: (0,) * len(shape))

    state_dims = ((ML_HEADS, ML_V_DIM, ML_QK_DIM), (ML_HEADS, ML_QK_DIM), (1, ML_HEADS), (CONV_W - 1, ML_CONV_CH))

    state_in = [pl.BlockSpec((None, None) + d, lambda s, c, n=len(d): (li0, s) + (0,) * n) for d in state_dims]
    state_out = [pl.BlockSpec((None,) + d, lambda s, c, n=len(d): (s,) + (0,) * n) for d in state_dims]
    return pl.pallas_call(
        functools.partial(_mlstm_kernel, L=L),
        out_shape=(jax.ShapeDtypeStruct((rows, ML_WIDTH), BF16),
                   *(jax.ShapeDtypeStruct((n_streams,) + d, F32) for d in state_dims)),
        grid=(n_streams, nc),
        in_specs=[
            tok(0), tok(1), tok(2),
            pl.BlockSpec((L, LANES), lambda s, c: (s * nc + c, 0)),
            pl.BlockSpec((None, N_GATES, L), lambda s, c: (s * nc + c, 0, 0)),
            const((1, LANES)), const((N_GATES, 1)), const((CONV_W, ML_CONV_CH)), const((1, ML_CONV_CH)),
            const((1, ML_WIDTH)),
            *state_in,
        ],
        out_specs=(pl.BlockSpec((L, ML_WIDTH), lambda s, c: (s * nc + c, 0)), *state_out),
        scratch_shapes=[
            pltpu.VMEM((CONV_PAD + L, ML_CONV_CH), F32),
            pltpu.VMEM((ML_HEADS, ML_V_DIM, ML_QK_DIM), F32),
            pltpu.VMEM((ML_HEADS, ML_QK_DIM), F32),
            pltpu.VMEM((1, ML_HEADS), F32),
        ],
        compiler_params=_params("arbitrary", "arbitrary"),
        name="mlstm",
    )(proj, proj, proj, gates, g_rows, bias_c, bias_r, conv_w, conv_b, ml_g, *state0)


def _lambda(lq1_ref, lk1_ref, lq2_ref, lk2_ref, lam_init):
    return (jnp.exp(jnp.sum(lq1_ref[...] * lk1_ref[...], axis=-1, keepdims=True))
            - jnp.exp(jnp.sum(lq2_ref[...] * lk2_ref[...], axis=-1, keepdims=True)) + lam_init)


def _lane_tiles(x):
    return [x[:, c * LANES:(c + 1) * LANES] for c in range(x.shape[1] // LANES)]


def _fold(op, x):
    return functools.reduce(op, _lane_tiles(x))


def _exp2_rows(s, m_b):
    return jnp.concatenate([jnp.exp2(t - m_b) for t in _lane_tiles(s)], axis=-1)


def _replicate(col):
    return jnp.broadcast_to(col, (col.shape[0], LANES))


def _da_finish(out1, out2, lam, g, lam_init):
    att = out1 - lam * out2
    return ((_rms(att) * g) * (1.0 - lam_init)).astype(BF16)


LOG2E = math.log2(math.e)
Q_SCALE = DA_HEAD_DIM ** -0.5 * LOG2E


def _attn_prompt_kernel(q_ref, k_ref, v_ref, lq1_ref, lk1_ref, lq2_ref, lk2_ref, g_ref, o_ref,
                        kb_ref, vb_ref, s_ref, m_ref, l_ref, acc_ref, *, lam_init, tq):
    i = pl.program_id(2)
    hd = DA_HEAD_DIM

    @pl.when(i == 0)
    def _():
        kb_ref[...] = k_ref[...].astype(BF16)
        vb_ref[...] = v_ref[...].astype(BF16)

    qs = (q_ref[...] * Q_SCALE).astype(BF16)

    def rows(ref, j, nb):
        return ref[pl.ds(pl.multiple_of(j * tq, tq), nb * tq), :]

    def scores(kj, t):
        return lax.dot_general(qs[:, t * hd:(t + 1) * hd], kj[:, t * hd:(t + 1) * hd], NT,
                               preferred_element_type=F32)

    def note_scores(j, nb, t, s):
        for b in range(nb):
            s_ref[t, j + b] = s[:, b * tq:(b + 1) * tq]
        m_ref[t] = jnp.maximum(m_ref[t], _fold(jnp.maximum, s))

    def for_blocks(n, fn):
        def pair(k, carry):
            fn(2 * k, 2)
            return carry

        lax.fori_loop(0, n // 2, pair, 0)
        pl.when(n % 2 == 1)(lambda: fn(n - 1, 1))

    m_ref[...] = jnp.full_like(m_ref, NEG_INF)

    def sweep_max(j, nb):
        kj = rows(kb_ref, j, nb)
        for t in range(2):
            note_scores(j, nb, t, scores(kj, t))

    for_blocks(i, sweep_max)
    diag = (lax.broadcasted_iota(jnp.int32, (tq, 1), 0) // CHUNK
            >= lax.broadcasted_iota(jnp.int32, (1, tq), 1) // CHUNK)
    kd = rows(kb_ref, i, 1)
    for t in range(2):
        note_scores(i, 1, t, jnp.where(diag, scores(kd, t), NEG_INF))
        m_ref[t] = _replicate(jnp.max(m_ref[t], axis=-1, keepdims=True))

    l_ref[...] = jnp.zeros_like(l_ref)
    acc_ref[...] = jnp.zeros_like(acc_ref)

    def sweep_pv(j, carry):
        vj = rows(vb_ref, j, 1)
        for t in range(2):
            p = _exp2_rows(s_ref[t, j], m_ref[t])
            acc_ref[t] += jnp.dot(p.astype(BF16), vj, preferred_element_type=F32)
            l_ref[t] += _fold(jnp.add, p)
        return carry

    lax.fori_loop(0, i + 1, sweep_pv, 0)
    outs = [acc_ref[t] * (1.0 / jnp.sum(l_ref[t], axis=-1, keepdims=True)) for t in range(2)]
    lam = _lambda(lq1_ref, lk1_ref, lq2_ref, lk2_ref, lam_init)
    o_ref[...] = _da_finish(outs[0], outs[1], lam, g_ref[...], lam_init)


def _attn_prompt(proj, n_streams, seq, lq1, lk1, lq2, lk2, da_g, lam_init):
    tq = _tile(seq, 512)
    nq = seq // tq
    dq = 2 * DA_HEAD_DIM

    def const(shape):
        return pl.BlockSpec(shape, lambda b, h, i: (0,) * len(shape))

    return pl.pallas_call(
        functools.partial(_attn_prompt_kernel, lam_init=lam_init, tq=tq),
        out_shape=jax.ShapeDtypeStruct((n_streams * seq, DA_WIDTH), BF16),
        grid=(n_streams, DA_HEADS, nq),
        in_specs=[
            pl.BlockSpec((None, tq, dq), lambda b, h, i: (3, b * nq + i, h)),
            pl.BlockSpec((None, seq, dq), lambda b, h, i: (4, b, h)),
            pl.BlockSpec((None, seq, DA_V_DIM), lambda b, h, i: (5, b, h)),
            const((1, DA_HEAD_DIM)), const((1, DA_HEAD_DIM)), const((1, DA_HEAD_DIM)), const((1, DA_HEAD_DIM)),
            const((1, DA_V_DIM)),
        ],
        out_specs=pl.BlockSpec((tq, DA_V_DIM), lambda b, h, i: (b * nq + i, h)),
        scratch_shapes=[
            pltpu.VMEM((seq, dq), BF16),
            pltpu.VMEM((seq, DA_V_DIM), BF16),
            pltpu.VMEM((2, nq, tq, tq), F32),
            pltpu.VMEM((2, tq, LANES), F32),
            pltpu.VMEM((2, tq, LANES), F32),
            pltpu.VMEM((2, tq, DA_V_DIM), F32),
        ],
        compiler_params=_params("parallel", "parallel", "arbitrary"),
        name="attn_prompt",
    )(proj, proj, proj, lq1, lk1, lq2, lk2, da_g)


def _attn_sample_kernel(q_ref, kn_ref, vn_ref, ck_ref, cv_ref, lq1_ref, lk1_ref, lq2_ref, lk2_ref, g_ref,
                        o_ref, kbuf, vbuf, sem, *, lam_init, li):
    hd = DA_HEAD_DIM
    s = pl.program_id(0)
    slot = s % 2

    def cache_copies(stream, slot_, h):
        return (pltpu.make_async_copy(ck_ref.at[li, stream, :, h, :], kbuf.at[slot_, h], sem.at[0, slot_, h]),
                pltpu.make_async_copy(cv_ref.at[li, stream, :, h, :], vbuf.at[slot_, h], sem.at[1, slot_, h]))

    def start_stream(stream, slot_):
        for h in range(DA_HEADS):
            for thread, cp in enumerate(cache_copies(stream, slot_, h)):
                cp.start(priority=thread)

    @pl.when(s == 0)
    def _():
        start_stream(0, 0)

    @pl.when(s + 1 < pl.num_programs(0))
    def _():
        start_stream(s + 1, 1 - slot)

    for h in range(DA_HEADS):
        for cp in cache_copies(s, slot, h):
            cp.wait()

    rows = q_ref.shape[0]
    lam = _lambda(lq1_ref, lk1_ref, lq2_ref, lk2_ref, lam_init)
    for h in range(DA_HEADS):
        cols = slice(h * DA_V_DIM, (h + 1) * DA_V_DIM)
        qs = (q_ref[:, cols] * Q_SCALE).astype(BF16)
        kc, vc = kbuf[slot, h].astype(BF16), vbuf[slot, h].astype(BF16)
        kn, vn = kn_ref[:, cols].astype(BF16), vn_ref[:, cols].astype(BF16)
        pcs, pns, ls = [], [], []
        for t in range(2):
            sl = slice(t * hd, (t + 1) * hd)
            sc = lax.dot_general(qs[:, sl], kc[:, sl], NT, preferred_element_type=F32)
            sn = lax.dot_general(qs[:, sl], kn[:, sl], NT, preferred_element_type=F32)
            m = jnp.maximum(jnp.max(_fold(jnp.maximum, sc), axis=-1, keepdims=True),
                            jnp.max(sn, axis=-1, keepdims=True))
            pc = _exp2_rows(sc, _replicate(m))
            pn = jnp.exp2(sn - m)
            ls.append(jnp.sum(_fold(jnp.add, pc), axis=-1, keepdims=True) + jnp.sum(pn, axis=-1, keepdims=True))
            pcs.append(pc.astype(BF16))
            pns.append(pn.astype(BF16))
        acc = (jnp.dot(jnp.concatenate(pcs, axis=0), vc, preferred_element_type=F32)
               + jnp.dot(jnp.concatenate(pns, axis=0), vn, preferred_element_type=F32))
        outs = [acc[t * rows:(t + 1) * rows] * (1.0 / ls[t]) for t in range(2)]
        o_ref[:, cols] = _da_finish(outs[0], outs[1], lam, g_ref[...], lam_init)


def _attn_sample(proj, n_streams, seq, cache_k, cache_v, li, lq1, lk1, lq2, lk2, da_g, lam_init):
    past = cache_k.shape[2]
    dq = 2 * DA_HEAD_DIM

    def const(shape):
        return pl.BlockSpec(shape, lambda s: (0,) * len(shape))

    def new(piece):
        return pl.BlockSpec((None, seq, PIECE), lambda s: (piece, s, 0))

    cache_spec = pl.BlockSpec(memory_space=pl.ANY)
    return pl.pallas_call(
        functools.partial(_attn_sample_kernel, lam_init=lam_init, li=li),
        out_shape=jax.ShapeDtypeStruct((n_streams * seq, DA_WIDTH), BF16),
        grid=(n_streams,),
        in_specs=[
            new(3), new(4), new(5), cache_spec, cache_spec,
            const((1, DA_HEAD_DIM)), const((1, DA_HEAD_DIM)), const((1, DA_HEAD_DIM)), const((1, DA_HEAD_DIM)),
            const((1, DA_V_DIM)),
        ],
        out_specs=pl.BlockSpec((seq, DA_WIDTH), lambda s: (s, 0)),
        scratch_shapes=[
            pltpu.VMEM((2, DA_HEADS, past, dq), F32),
            pltpu.VMEM((2, DA_HEADS, past, DA_V_DIM), F32),
            pltpu.SemaphoreType.DMA((2, 2, DA_HEADS)),
        ],
        compiler_params=_params("arbitrary"),
        name="attn_sample",
    )(proj, proj, proj, cache_k, cache_v, lq1, lk1, lq2, lk2, da_g)


def _outproj_kernel(x_ref, ml_ref, da_ref, w1_ref, w2_ref, post_ref, o_ref):
    y = (jnp.dot(ml_ref[...], w1_ref[...], preferred_element_type=F32)
         + jnp.dot(da_ref[...], w2_ref[...], preferred_element_type=F32))
    o_ref[...] = x_ref[...] + _rms(y) * post_ref[...]


def _outproj(x, ml_out, da_out, w_out, post_g, li):
    T, D = x.shape
    tm = _tile(T, 512)
    return pl.pallas_call(
        _outproj_kernel,
        out_shape=jax.ShapeDtypeStruct((T, D), F32),
        grid=(T // tm,),
        in_specs=[
            pl.BlockSpec((tm, D), lambda i: (i, 0)),
            pl.BlockSpec((tm, ML_WIDTH), lambda i: (i, 0)),
            pl.BlockSpec((tm, DA_WIDTH), lambda i: (i, 0)),
            pl.BlockSpec((None, ML_WIDTH, D), lambda i: (li, 0, 0)),
            pl.BlockSpec((None, DA_WIDTH, D), lambda i: (li, 1, 0)),
            pl.BlockSpec((1, D), lambda i: (0, 0)),
        ],
        out_specs=pl.BlockSpec((tm, D), lambda i: (i, 0)),
        compiler_params=_params("parallel"),
        name="outproj",
    )(x, ml_out, da_out, w_out, w_out, post_g)


def kernel(x_prompt, x_sample, cache_k, cache_v, state_C, state_n, state_m, state_conv,
           ffn1_pre_g, ffn1_wg, ffn1_wu, ffn1_wd, ffn1_post_g,
           mix_pre_g, w_in, b_i, b_f, conv_w, conv_b, ml_norm_g,
           lam_q1, lam_k1, lam_q2, lam_k2, da_norm_g, w_out, mix_post_g,
           ffn2_pre_g, ffn2_wg, ffn2_wu, ffn2_wd, ffn2_post_g):
    Bp, S, D = x_prompt.shape
    Bs, Ls, _ = x_sample.shape
    depth = w_in.shape[0]
    past = cache_k.shape[2]
    Tp, Ts = Bp * S, Bs * Ls
    assert Ls == CHUNK and past % LANES == 0 and S % CHUNK == 0
    Lp = 256 if S % 256 == 0 else CHUNK

    xp, xs = x_prompt.reshape(Tp, D), x_sample.reshape(Ts, D)
    zero_state = (jnp.zeros((1, Bp, ML_HEADS, ML_V_DIM, ML_QK_DIM), F32), jnp.zeros((1, Bp, ML_HEADS, ML_QK_DIM), F32),
                  jnp.zeros((1, Bp, 1, ML_HEADS), F32), jnp.zeros((1, Bp, CONV_W - 1, ML_CONV_CH), F32))
    stream_state = (state_C, state_n, state_m.reshape(depth, Bs, 1, ML_HEADS), state_conv)

    def row(v):
        return v.reshape(1, -1)

    w_a = w_in[:, :, :GATE_OFF].astype(BF16)
    w_b = w_in[:, :, GATE_OFF + N_GATES:].astype(BF16)
    w_gate = jnp.pad(w_in[:, :, GATE_OFF:GATE_OFF + N_GATES].astype(BF16), ((0, 0), (0, 0), (0, LANES - N_GATES)))
    w_o = w_out.astype(BF16)

    kv_p, kv_s, st_p, st_s = (), (), [], []
    for li in range(depth):
        lam_init = 0.8 - 0.6 * math.exp(-0.3 * li)
        ffn1 = (row(ffn1_pre_g[li]), ffn1_wg, ffn1_wu, ffn1_wd, row(ffn1_post_g[li]), li)
        ffn2 = (row(ffn2_pre_g[li]), ffn2_wg, ffn2_wu, ffn2_wd, row(ffn2_post_g[li]), li)
        bias = jnp.concatenate([b_i[li], b_f[li]])
        bias_c = jnp.pad(bias, (0, LANES - N_GATES)).reshape(1, LANES)
        bias_r = bias.reshape(N_GATES, 1)
        ml_args = (bias_c, bias_r, conv_w[li], row(conv_b[li]), row(ml_norm_g[li]))
        lam_args = (row(lam_q1[li]), row(lam_k1[li]), row(lam_q2[li]), row(lam_k2[li]), row(da_norm_g[li]), lam_init)

        xp = _ffn(xp, *ffn1)
        xs = _ffn(xs, *ffn1)
        proj_p, gates_p, *kv_p = _inproj(xp, row(mix_pre_g[li]), w_a, w_b, w_gate, li, depth, kv_p)
        proj_s, gates_s, *kv_s = _inproj(xs, row(mix_pre_g[li]), w_a, w_b, w_gate, li, depth, kv_s)
        ml_p, *st = _mlstm(proj_p, gates_p, Bp, S, Lp, *ml_args, zero_state, 0)
        st_p.append(st)
        ml_s, *st = _mlstm(proj_s, gates_s, Bs, Ls, Ls, *ml_args, stream_state, li)
        st_s.append(st)
        da_p = _attn_prompt(proj_p, Bp, S, *lam_args)
        da_s = _attn_sample(proj_s, Bs, Ls, cache_k, cache_v, li, *lam_args)
        xp = _ffn(_outproj(xp, ml_p, da_p, w_o, row(mix_post_g[li]), li), *ffn2)
        xs = _ffn(_outproj(xs, ml_s, da_s, w_o, row(mix_post_g[li]), li), *ffn2)

    def states(kv, st, n_streams, seq):
        c, n, m, conv = (jnp.stack(e) for e in zip(*st))
        return (kv[0].reshape(depth, n_streams, seq, DA_HEADS, 2 * DA_HEAD_DIM),
                kv[1].reshape(depth, n_streams, seq, DA_HEADS, DA_V_DIM),
                c, n, m.reshape(depth, n_streams, ML_HEADS), conv)

    return (xp.reshape(Bp, S, D), xs.reshape(Bs, Ls, D), *states(kv_p, st_p, Bp, S), *states(kv_s, st_s, Bs, Ls))
```
